```python
import jax, jax.numpy as jnp
from jax import lax
import numpy as np

D_MODEL = 1024
BATCH = 2
SEQ = 16384
DEPTH = 1
DEC_BATCH = 16
DEC_SEQ = 32
PAST_LEN = 4096

CHUNK = 64
EPS = 1e-6
D_MIX = D_MODEL
MLA_HEADS = 8
MLA_NOPE_DIM = 64
MLA_ROPE_DIM = 32
MLA_V_DIM = 64
MLA_Q_RANK = 256
MLA_KV_RANK = 128
ROPE_THETA = 10000.0
MLA_WIDTH = MLA_HEADS * MLA_V_DIM
MLA_SCALE = (MLA_NOPE_DIM + MLA_ROPE_DIM) ** -0.5
Q_BLOCK = 128
LRU_WIDTH = D_MIX - MLA_WIDTH
LRU_BLOCKS = 8
LRU_BLOCK_DIM = LRU_WIDTH // LRU_BLOCKS
CONV_WIDTH = 4
LRU_C = 8.0
D_IN = MLA_Q_RANK + MLA_KV_RANK + MLA_ROPE_DIM + 2 * LRU_WIDTH
PEER_HEADS = 8
PEER_NKEYS = 128
PEER_EXPERTS = PEER_NKEYS * PEER_NKEYS
PEER_KEY_DIM = 256
PEER_HALF = PEER_KEY_DIM // 2
PEER_TOPK = 16
PEER_TOKEN_BLOCK = 128

kernel_name = "hybrid_mla_rglru_peer_stream_step"


def rmsnorm(x, g):
    xf = x.astype(jnp.float32)
    y = xf * lax.rsqrt(jnp.mean(xf * xf, axis=-1, keepdims=True) + EPS)
    return (y * g.astype(jnp.float32)).astype(x.dtype)


def rope(x, pos):
    half = x.shape[-1] // 2
    inv = ROPE_THETA ** (-jnp.arange(half, dtype=jnp.float32) / half)
    ang = pos.astype(jnp.float32)[:, None] * inv[None, :]
    ang = ang.reshape((ang.shape[0],) + (1,) * (x.ndim - 3) + (half,))
    cos, sin = jnp.cos(ang), jnp.sin(ang)
    xf = x.astype(jnp.float32)
    x1, x2 = xf[..., :half], xf[..., half:]
    return jnp.concatenate([x1 * cos - x2 * sin, x2 * cos + x1 * sin], axis=-1).astype(x.dtype)


def mla_attend(q_lat, q_rope, q_pos, ckv, krope, k_pos):
    s = jnp.einsum("bqhr,bkr->bhqk", q_lat, ckv) + jnp.einsum("bqhe,bke->bhqk", q_rope, krope)
    s = s.astype(jnp.float32) * MLA_SCALE
    mask = (k_pos[None, :] // CHUNK) <= (q_pos[:, None] // CHUNK)
    s = jnp.where(mask[None, None], s, -jnp.inf)
    p = jax.nn.softmax(s, axis=-1).astype(ckv.dtype)
    return jnp.einsum("bhqk,bkr->bqhr", p, ckv)


def mla_prompt(q_lat, q_rope, ckv, krope):
    B, S, H, R = q_lat.shape
    nb = S // Q_BLOCK
    k_pos = jnp.arange(S)
    ql = q_lat.reshape(B, nb, Q_BLOCK, H, R).swapaxes(0, 1)
    qr = q_rope.reshape(B, nb, Q_BLOCK, H, MLA_ROPE_DIM).swapaxes(0, 1)

    def one(args):
        i, qlb, qrb = args
        return mla_attend(qlb, qrb, i * Q_BLOCK + jnp.arange(Q_BLOCK), ckv, krope, k_pos)

    o = lax.map(one, (jnp.arange(nb), ql, qr))
    return o.swapaxes(0, 1).reshape(B, S, H, R)


def causal_conv(xb, buf, w, b):
    S = xb.shape[1]
    xp = jnp.concatenate([buf.astype(xb.dtype), xb], axis=1)
    y = b + sum(xp[:, k:k + S] * w[k] for k in range(CONV_WIDTH))
    return y, xp[:, -(CONV_WIDTH - 1):]


def rg_lru(xc, h0, wa, ba, wi, bi, lam):
    B, S, W = xc.shape
    xf = xc.astype(jnp.float32)
    xb = xf.reshape(B, S, LRU_BLOCKS, LRU_BLOCK_DIM)
    r = jax.nn.sigmoid(jnp.einsum("bsnd,nde->bsne", xb, wa.astype(jnp.float32)).reshape(B, S, W) + ba.astype(jnp.float32))
    i = jax.nn.sigmoid(jnp.einsum("bsnd,nde->bsne", xb, wi.astype(jnp.float32)).reshape(B, S, W) + bi.astype(jnp.float32))
    log_a = -LRU_C * r * jax.nn.softplus(-lam.astype(jnp.float32))
    a = jnp.exp(log_a)
    bterm = jnp.sqrt(-jnp.expm1(2.0 * log_a)) * (i * xf)
    bterm = bterm.at[:, 0].add(a[:, 0] * h0.astype(jnp.float32))

    def comb(lft, rgt):
        return (lft[0] * rgt[0], rgt[0] * lft[1] + rgt[1])

    _, h = lax.associative_scan(comb, (a, bterm), axis=1)
    return h


def peer_block(xt, wq, k1, k2, u, v):
    T = xt.shape[0]
    q = (xt @ wq).reshape(T, PEER_HEADS, PEER_KEY_DIM)
    s1 = jnp.einsum("thd,hnd->thn", q[..., :PEER_HALF], k1).astype(jnp.float32)
    s2 = jnp.einsum("thd,hnd->thn", q[..., PEER_HALF:], k2).astype(jnp.float32)
    v1, i1 = lax.top_k(s1, PEER_TOPK)
    v2, i2 = lax.top_k(s2, PEER_TOPK)
    cand = (v1[..., :, None] + v2[..., None, :]).reshape(T, PEER_HEADS, PEER_TOPK * PEER_TOPK)
    vs, ic = lax.top_k(cand, PEER_TOPK)
    e1 = jnp.take_along_axis(i1, ic // PEER_TOPK, axis=-1)
    e2 = jnp.take_along_axis(i2, ic % PEER_TOPK, axis=-1)
    idx = e1 * PEER_NKEYS + e2
    g = jax.nn.softmax(vs, axis=-1)
    act = jax.nn.gelu(jnp.einsum("thkd,td->thk", u[idx], xt).astype(jnp.float32), approximate=False)
    return jnp.einsum("thk,thkd->td", (g * act).astype(xt.dtype), v[idx])


def peer(x, wq, k1, k2, u, v):
    B, S, D = x.shape
    T = B * S
    nb = -(-T // PEER_TOKEN_BLOCK)
    xt = jnp.pad(x.reshape(T, D), ((0, nb * PEER_TOKEN_BLOCK - T), (0, 0)))
    y = lax.map(lambda blk: peer_block(blk, wq, k1, k2, u, v), xt.reshape(nb, PEER_TOKEN_BLOCK, D))
    return y.reshape(nb * PEER_TOKEN_BLOCK, D)[:T].reshape(B, S, D)


def layer(x, pos, past_ckv, past_krope, conv_buf, h0, lw):
    B, S, _ = x.shape
    xn = rmsnorm(x, lw["norm_mix"])
    z = xn @ lw["w_in"]
    o1 = MLA_Q_RANK
    o2 = o1 + MLA_KV_RANK
    o3 = o2 + MLA_ROPE_DIM
    o4 = o3 + LRU_WIDTH
    cq, ckv_raw, kr_raw, xr, xg = z[..., :o1], z[..., o1:o2], z[..., o2:o3], z[..., o3:o4], z[..., o4:]
    q = (rmsnorm(cq, lw["norm_q"]) @ lw["w_uq"]).reshape(B, S, MLA_HEADS, MLA_NOPE_DIM + MLA_ROPE_DIM)
    q_rope = rope(q[..., MLA_NOPE_DIM:], pos)
    q_lat = jnp.einsum("bshd,rhd->bshr", q[..., :MLA_NOPE_DIM], lw["w_uk"])
    ckv = rmsnorm(ckv_raw, lw["norm_kv"])
    krope = rope(kr_raw, pos)
    if past_ckv is None:
        att = mla_prompt(q_lat, q_rope, ckv, krope)
    else:
        k_ckv = jnp.concatenate([past_ckv.astype(ckv.dtype), ckv], axis=1)
        k_rope = jnp.concatenate([past_krope.astype(krope.dtype), krope], axis=1)
        att = mla_attend(q_lat, q_rope, pos, k_ckv, k_rope, jnp.arange(k_ckv.shape[1]))
    mla_out = jnp.einsum("bshr,rhd->bshd", att, lw["w_uv"]).reshape(B, S, MLA_WIDTH)
    xc, new_buf = causal_conv(xr, conv_buf, lw["conv_w"], lw["conv_b"])
    h = rg_lru(xc, h0, lw["lru_wa"], lw["lru_ba"], lw["lru_wi"], lw["lru_bi"], lw["lru_lambda"])
    lru_out = h.astype(x.dtype) * jax.nn.gelu(xg, approximate=False)
    mix = jnp.concatenate([rmsnorm(mla_out, lw["norm_mla_out"]), rmsnorm(lru_out, lw["norm_lru_out"])], axis=-1)
    x = x + mix @ lw["w_out"]
    x = x + peer(rmsnorm(x, lw["norm_ffn"]), lw["peer_wq"], lw["peer_keys1"], lw["peer_keys2"], lw["peer_u"], lw["peer_v"])
    return x, ckv, krope, h[:, -1].astype(h0.dtype), new_buf


def setup_inputs(seed: int = 0) -> dict:
    key = jax.random.key(seed)
    ks = iter(jax.random.split(key, 40))

    def nrm(shape, scale):
        return jax.random.normal(next(ks), shape, jnp.float32) * scale

    def gain(shape):
        return 1.0 + nrm(shape, 0.02)

    u_a = jax.random.uniform(next(ks), (DEPTH, LRU_WIDTH), jnp.float32, minval=0.9, maxval=0.999)
    sa = u_a ** (1.0 / LRU_C)
    lam = jnp.log(sa) - jnp.log1p(-sa)
    return {
        "x_prompt": nrm((BATCH, SEQ, D_MODEL), 1.0),
        "x_sample": nrm((DEC_BATCH, DEC_SEQ, D_MODEL), 1.0),
        "cache_mla_ckv": nrm((DEPTH, DEC_BATCH, PAST_LEN, MLA_KV_RANK), 1.0),
        "cache_mla_krope": nrm((DEPTH, DEC_BATCH, PAST_LEN, MLA_ROPE_DIM), 1.0),
        "state_lru_h": nrm((DEPTH, DEC_BATCH, LRU_WIDTH), 0.5),
        "state_lru_conv": nrm((DEPTH, DEC_BATCH, CONV_WIDTH - 1, LRU_WIDTH), 1.0),
        "norm_mix": gain((DEPTH, D_MODEL)),
        "w_in": nrm((DEPTH, D_MODEL, D_IN), D_MODEL ** -0.5),
        "norm_q": gain((DEPTH, MLA_Q_RANK)),
        "w_uq": nrm((DEPTH, MLA_Q_RANK, MLA_HEADS * (MLA_NOPE_DIM + MLA_ROPE_DIM)), MLA_Q_RANK ** -0.5),
        "norm_kv": gain((DEPTH, MLA_KV_RANK)),
        "w_uk": nrm((DEPTH, MLA_KV_RANK, MLA_HEADS, MLA_NOPE_DIM), MLA_KV_RANK ** -0.5),
        "w_uv": nrm((DEPTH, MLA_KV_RANK, MLA_HEADS, MLA_V_DIM), MLA_KV_RANK ** -0.5),
        "conv_w": nrm((DEPTH, CONV_WIDTH, LRU_WIDTH), 0.5),
        "conv_b": nrm((DEPTH, LRU_WIDTH), 0.01),
        "lru_wa": nrm((DEPTH, LRU_BLOCKS, LRU_BLOCK_DIM, LRU_BLOCK_DIM), LRU_BLOCK_DIM ** -0.5),
        "lru_ba": nrm((DEPTH, LRU_WIDTH), 0.01),
        "lru_wi": nrm((DEPTH, LRU_BLOCKS, LRU_BLOCK_DIM, LRU_BLOCK_DIM), LRU_BLOCK_DIM ** -0.5),
        "lru_bi": nrm((DEPTH, LRU_WIDTH), 0.01),
        "lru_lambda": lam,
        "norm_mla_out": gain((DEPTH, MLA_WIDTH)),
        "norm_lru_out": gain((DEPTH, LRU_WIDTH)),
        "w_out": nrm((DEPTH, D_MIX, D_MODEL), D_MIX ** -0.5),
        "norm_ffn": gain((DEPTH, D_MODEL)),
        "peer_wq": nrm((DEPTH, D_MODEL, PEER_HEADS * PEER_KEY_DIM), D_MODEL ** -0.5),
        "peer_keys1": nrm((DEPTH, PEER_HEADS, PEER_NKEYS, PEER_HALF), PEER_HALF ** -0.5),
        "peer_keys2": nrm((DEPTH, PEER_HEADS, PEER_NKEYS, PEER_HALF), PEER_HALF ** -0.5),
        "peer_u": nrm((DEPTH, PEER_EXPERTS, D_MODEL), D_MODEL ** -0.5),
        "peer_v": nrm((DEPTH, PEER_EXPERTS, D_MODEL), 0.5),
        "norm_final": gain((D_MODEL,)),
    }


def reference(x_prompt, x_sample, cache_mla_ckv, cache_mla_krope, state_lru_h, state_lru_conv,
              norm_mix, w_in, norm_q, w_uq, norm_kv, w_uk, w_uv, conv_w, conv_b,
              lru_wa, lru_ba, lru_wi, lru_bi, lru_lambda, norm_mla_out, norm_lru_out, w_out,
              norm_ffn, peer_wq, peer_keys1, peer_keys2, peer_u, peer_v, norm_final):
    Bp, Sp, _ = x_prompt.shape
    Sd = x_sample.shape[1]
    past_len = cache_mla_ckv.shape[2]
    pos_p = jnp.arange(Sp)
    pos_s = past_len + jnp.arange(Sd)
    xp, xs = x_prompt, x_sample
    ckv_p, kr_p, h_p, cv_p = [], [], [], []
    ckv_s, kr_s, h_s, cv_s = [], [], [], []
    for l in range(DEPTH):
        lw = {
            "norm_mix": norm_mix[l], "w_in": w_in[l], "norm_q": norm_q[l], "w_uq": w_uq[l],
            "norm_kv": norm_kv[l], "w_uk": w_uk[l], "w_uv": w_uv[l], "conv_w": conv_w[l],
            "conv_b": conv_b[l], "lru_wa": lru_wa[l], "lru_ba": lru_ba[l], "lru_wi": lru_wi[l],
            "lru_bi": lru_bi[l], "lru_lambda": lru_lambda[l], "norm_mla_out": norm_mla_out[l],
            "norm_lru_out": norm_lru_out[l], "w_out": w_out[l], "norm_ffn": norm_ffn[l],
            "peer_wq": peer_wq[l], "peer_keys1": peer_keys1[l], "peer_keys2": peer_keys2[l],
            "peer_u": peer_u[l], "peer_v": peer_v[l],
        }
        buf0 = jnp.zeros((Bp, CONV_WIDTH - 1, LRU_WIDTH), xp.dtype)
        h00 = jnp.zeros((Bp, LRU_WIDTH), xp.dtype)
        xp, c1, k1, hh1, b1 = layer(xp, pos_p, None, None, buf0, h00, lw)
        xs, c2, k2, hh2, b2 = layer(xs, pos_s, cache_mla_ckv[l], cache_mla_krope[l],
                                    state_lru_conv[l], state_lru_h[l], lw)
        ckv_p.append(c1); kr_p.append(k1); h_p.append(hh1); cv_p.append(b1)
        ckv_s.append(c2); kr_s.append(k2); h_s.append(hh2); cv_s.append(b2)
    y_prompt = rmsnorm(xp, norm_final)
    y_sample = rmsnorm(xs, norm_final)
    return (y_prompt, y_sample,
            jnp.stack(ckv_p), jnp.stack(kr_p), jnp.stack(h_p), jnp.stack(cv_p),
            jnp.stack(ckv_s), jnp.stack(kr_s), jnp.stack(h_s), jnp.stack(cv_s))
```

```python
import functools

import jax
import jax.numpy as jnp
from jax import lax
from jax.experimental import pallas as pl
from jax.experimental.pallas import tpu as pltpu

F32 = jnp.float32
BF16 = jnp.bfloat16

D_MODEL = 1024
CHUNK = 64
EPS = 1e-6
MLA_HEADS = 8
MLA_NOPE = 64
MLA_ROPE = 32
MLA_VDIM = 64
MLA_QRANK = 256
MLA_KVRANK = 128
ROPE_THETA = 10000.0
MLA_WIDTH = MLA_HEADS * MLA_VDIM
MLA_SCALE = (MLA_NOPE + MLA_ROPE) ** -0.5
QK_PAD = 256
LRU_WIDTH = 512
LRU_BLOCKS = 8
LRU_BLOCK_DIM = LRU_WIDTH // LRU_BLOCKS
CONV_WIDTH = 4
LRU_C = 8.0
PEER_HEADS = 8
PEER_NKEYS = 128
PEER_KEY_DIM = 256
PEER_HALF = PEER_KEY_DIM // 2
PEER_TOPK = 16
LANES = 128
SUBLANES = 8
VMEM_LIMIT = 56 * 1024 * 1024

ZC_CQ, ZC_CKV, ZC_KR, ZC_KRS, ZC_XR, ZC_XG, ZC_END = 0, 256, 384, 512, 640, 1152, 1664
QC_NOPE, QC_ROPE, QC_ROPES, QC_END = 0, 512, 1536, 2560

_NT = (((1,), (1,)), ((), ()))


def _rms(x, g):
    return x * lax.rsqrt(jnp.mean(x * x, axis=-1, keepdims=True) + EPS) * g


def _expm1(y):
    u = jnp.exp(y)
    um1 = u - 1.0
    return jnp.where(um1 == 0.0, y, jnp.where(um1 == -1.0, -1.0, um1 * y / jnp.log(u)))


def _gelu(x):
    return 0.5 * x * (1.0 + lax.erf(x * (2.0 ** -0.5)))


def _params(*sem):
    return pltpu.CompilerParams(dimension_semantics=sem, vmem_limit_bytes=VMEM_LIMIT)


def _full(shape):
    n = len(shape)
    return pl.BlockSpec(shape, lambda *_: (0,) * n)


def _proj_kernel(x_ref, tab_ref, gmix_ref, win_ref, gq_ref, wuq_ref, wuk_ref, gkv_ref,
                 q_ref, kcat_ref, ckv_ref, kr_ref, xr_ref, xg_ref):
    x = x_ref[0]
    xn = _rms(x, gmix_ref[...]).astype(BF16)
    z = jnp.dot(xn, win_ref[...], preferred_element_type=F32)
    ctab = tab_ref[:, :LANES]
    stab = tab_ref[:, LANES:]
    ckv = _rms(z[:, ZC_CKV:ZC_KR], gkv_ref[...])
    kr = z[:, ZC_KR:ZC_KRS] * ctab + z[:, ZC_KRS:ZC_XR] * stab
    ckv_ref[0] = ckv
    kr_ref[0] = kr[:, :MLA_ROPE]
    kcat_ref[0, :, :LANES] = ckv.astype(BF16)
    lane = lax.broadcasted_iota(jnp.int32, kr.shape, 1)
    kcat_ref[0, :, LANES:] = jnp.where(lane < MLA_ROPE, kr, 1.0).astype(BF16)
    xr_ref[0] = z[:, ZC_XR:ZC_XG]
    xg_ref[0] = z[:, ZC_XG:ZC_END]
    cqn = _rms(z[:, ZC_CQ:ZC_CKV], gq_ref[...]).astype(BF16)
    q = jnp.dot(cqn, wuq_ref[...], preferred_element_type=F32)
    qlat = jnp.dot(q[:, QC_NOPE:QC_ROPE].astype(BF16), wuk_ref[...], preferred_element_type=F32)
    for h in range(MLA_HEADS):
        q_ref[0, h, :, :LANES] = (qlat[:, h * LANES:(h + 1) * LANES] * MLA_SCALE).astype(BF16)
        rp = (q[:, QC_ROPE + h * LANES:QC_ROPE + (h + 1) * LANES] * ctab
              + q[:, QC_ROPES + h * LANES:QC_ROPES + (h + 1) * LANES] * stab)
        q_ref[0, h, :, LANES:] = (rp * MLA_SCALE).astype(BF16)


def _proj(x, tab, gmix, win, gq, wuq, wuk, gkv, tm):
    b, s, d = x.shape
    grid = (b, s // tm)
    out_shape = (
        jax.ShapeDtypeStruct((b, MLA_HEADS, s, QK_PAD), BF16),
        jax.ShapeDtypeStruct((b, s, QK_PAD), BF16),
        jax.ShapeDtypeStruct((b, s, MLA_KVRANK), F32),
        jax.ShapeDtypeStruct((b, s, MLA_ROPE), F32),
        jax.ShapeDtypeStruct((b, s, LRU_WIDTH), F32),
        jax.ShapeDtypeStruct((b, s, LRU_WIDTH), F32),
    )
    row = lambda w: pl.BlockSpec((1, tm, w), lambda bi, i: (bi, i, 0))
    return pl.pallas_call(
        _proj_kernel,
        grid=grid,
        in_specs=[row(d), pl.BlockSpec((tm, 2 * LANES), lambda bi, i: (i, 0)),
                  _full(gmix.shape), _full(win.shape), _full(gq.shape), _full(wuq.shape),
                  _full(wuk.shape), _full(gkv.shape)],
        out_specs=(pl.BlockSpec((1, MLA_HEADS, tm, QK_PAD), lambda bi, i: (bi, 0, i, 0)),
                   row(QK_PAD), row(MLA_KVRANK), row(MLA_ROPE), row(LRU_WIDTH), row(LRU_WIDTH)),
        out_shape=out_shape,
        compiler_params=_params("parallel", "parallel"),
        name="proj",
    )(x, tab, gmix, win, gq, wuq, wuk, gkv)


def _attn_finish(acc, wuv_ref, g_ref, qb):
    att = (acc[:, :LANES] / acc[:, QK_PAD - 1:QK_PAD]).astype(BF16)
    mla = jnp.zeros((qb, MLA_WIDTH), F32)
    for h in range(MLA_HEADS):
        mla = mla + jnp.dot(att[h * qb:(h + 1) * qb], wuv_ref[h], preferred_element_type=F32)
    return _rms(mla, g_ref[...]).astype(BF16)


def _attn_prompt_kernel(q_ref, k_ref, wuv_ref, g_ref, o_ref, m_sc, acc_sc, *, qb):
    i = pl.program_id(1)
    rows = MLA_HEADS * qb
    q = q_ref[0].reshape(rows, QK_PAD)
    m_sc[...] = jnp.full(m_sc.shape, -jnp.inf, F32)
    acc_sc[...] = jnp.zeros(acc_sc.shape, F32)

    def step(j, masked):
        k = k_ref[0, pl.ds(pl.multiple_of(j * qb, qb), qb), :]
        s = lax.dot_general(q, k, _NT, preferred_element_type=F32)
        if masked:
            tq = (lax.broadcasted_iota(jnp.int32, s.shape, 0) % qb) // CHUNK
            tk = lax.broadcasted_iota(jnp.int32, s.shape, 1) // CHUNK
            s = jnp.where(tk <= tq, s, -jnp.inf)
        m_prev = m_sc[...]
        m_new = jnp.maximum(m_prev, jnp.max(s, axis=-1, keepdims=True))
        p = jnp.exp(s - m_new).astype(BF16)
        acc_sc[...] = jnp.exp(m_prev - m_new) * acc_sc[...] + jnp.dot(p, k, preferred_element_type=F32)
        m_sc[...] = m_new

    def body(j, carry):
        step(j, False)
        return carry

    lax.fori_loop(0, i, body, 0)
    step(i, True)
    o_ref[0] = _attn_finish(acc_sc[...], wuv_ref, g_ref, qb)


def _attn_prompt(q, kcat, wuv, g, qb):
    b, _, s, _ = q.shape
    rows = MLA_HEADS * qb
    return pl.pallas_call(
        functools.partial(_attn_prompt_kernel, qb=qb),
        grid=(b, s // qb),
        in_specs=[pl.BlockSpec((1, MLA_HEADS, qb, QK_PAD), lambda bi, i: (bi, 0, i, 0)),
                  pl.BlockSpec((1, s, QK_PAD), lambda bi, i: (bi, 0, 0)),
                  _full(wuv.shape), _full(g.shape)],
        out_specs=pl.BlockSpec((1, qb, MLA_WIDTH), lambda bi, i: (bi, i, 0)),
        out_shape=jax.ShapeDtypeStruct((b, s, MLA_WIDTH), BF16),
        scratch_shapes=[pltpu.VMEM((rows, 1), F32), pltpu.VMEM((rows, QK_PAD), F32)],
        compiler_params=_params("parallel", "arbitrary"),
        name="attn_prompt",
    )(q, kcat, wuv, g)


def _attn_sample_kernel(q_ref, pckv_ref, pkr_ref, k_ref, wuv_ref, g_ref, o_ref, *, sd, past):
    rows = MLA_HEADS * sd
    q = q_ref[0].reshape(rows, QK_PAD)
    knew = k_ref[0]
    s_past = (lax.dot_general(q[:, :LANES], pckv_ref[0].astype(BF16), _NT, preferred_element_type=F32)
              + lax.dot_general(q[:, LANES:LANES + MLA_ROPE], pkr_ref[0].astype(BF16), _NT,
                                preferred_element_type=F32))
    s_new = lax.dot_general(q, knew, _NT, preferred_element_type=F32)

    def mask(s, k0):
        qc = (past + lax.broadcasted_iota(jnp.int32, s.shape, 0) % sd) // CHUNK
        kc = (k0 + lax.broadcasted_iota(jnp.int32, s.shape, 1)) // CHUNK
        return jnp.where(kc <= qc, s, -jnp.inf)

    s_past = mask(s_past, 0)
    s_new = mask(s_new, past)
    m = jnp.maximum(jnp.max(s_past, axis=-1, keepdims=True), jnp.max(s_new, axis=-1, keepdims=True))
    p_past = jnp.exp(s_past - m).astype(BF16)
    p_new = jnp.exp(s_new - m).astype(BF16)
    acc = jnp.dot(p_new, knew, preferred_element_type=F32)
    num = acc[:, :LANES] + jnp.dot(p_past, pckv_ref[0].astype(BF16), preferred_element_type=F32)
    den = acc[:, QK_PAD - 1:QK_PAD] + jnp.sum(p_past.astype(F32), axis=-1, keepdims=True)
    att = (num / den).astype(BF16)
    mla = jnp.zeros((sd, MLA_WIDTH), F32)
    for h in range(MLA_HEADS):
        mla = mla + jnp.dot(att[h * sd:(h + 1) * sd], wuv_ref[h], preferred_element_type=F32)
    o_ref[0] = _rms(mla, g_ref[...]).astype(BF16)


def _attn_sample(q, pckv, pkr, kcat, wuv, g, nb, sd):
    past = pckv.shape[1]
    return pl.pallas_call(
        functools.partial(_attn_sample_kernel, sd=sd, past=past),
        grid=(nb,),
        in_specs=[pl.BlockSpec((1, MLA_HEADS, sd, QK_PAD), lambda b: (0, 0, b, 0)),
                  pl.BlockSpec((1, past, MLA_KVRANK), lambda b: (b, 0, 0)),
                  pl.BlockSpec((1, past, MLA_ROPE), lambda b: (b, 0, 0)),
                  pl.BlockSpec((1, sd, QK_PAD), lambda b: (0, b, 0)),
                  _full(wuv.shape), _full(g.shape)],
        out_specs=pl.BlockSpec((1, sd, MLA_WIDTH), lambda b: (0, b, 0)),
        out_shape=jax.ShapeDtypeStruct((1, nb * sd, MLA_WIDTH), BF16),
        compiler_params=_params("parallel"),
        name="attn_sample",
    )(q, pckv, pkr, kcat, wuv, g)


def _lru_kernel(xr_ref, xg_ref, buf0_ref, h0_ref, cw_ref, cb_ref, wa_ref, ba_ref, wi_ref, bi_ref,
                lam_ref, g_ref, o_ref, hlast_ref, newbuf_ref, xp_sc, h_sc, a_sc, b_sc, hh_sc, *, ts):
    t = pl.program_id(1)
    nt = pl.num_programs(1)
    pad = SUBLANES

    @pl.when(t == 0)
    def _():
        xp_sc[pad - (CONV_WIDTH - 1):pad, :] = buf0_ref[0]
        h_sc[...] = h0_ref[0]

    xr = xr_ref[0]
    xp_sc[pad:pad + ts, :] = xr
    xc = cb_ref[...] + cw_ref[CONV_WIDTH - 1:CONV_WIDTH, :] * xr
    for k in range(CONV_WIDTH - 1):
        xc = xc + cw_ref[k:k + 1, :] * xp_sc[pad - (CONV_WIDTH - 1) + k:pad - (CONV_WIDTH - 1) + k + ts, :]
    tail = xp_sc[ts + pad - (CONV_WIDTH - 1):ts + pad, :]
    xp_sc[pad - (CONV_WIDTH - 1):pad, :] = tail

    xcb = xc.astype(BF16)
    r = jax.nn.sigmoid(jnp.dot(xcb, wa_ref[...], preferred_element_type=F32) + ba_ref[...])
    ig = jax.nn.sigmoid(jnp.dot(xcb, wi_ref[...], preferred_element_type=F32) + bi_ref[...])
    log_a = -LRU_C * r * jax.nn.softplus(-lam_ref[...])
    a = jnp.exp(log_a)
    bt = jnp.sqrt(-_expm1(2.0 * log_a)) * (ig * xc)

    ng = ts // SUBLANES
    a3 = a.reshape(ng, SUBLANES, LRU_WIDTH)
    b3 = bt.reshape(ng, SUBLANES, LRU_WIDTH)
    row = lax.broadcasted_iota(jnp.int32, a3.shape, 1)
    for d in (1, 2, 4):
        valid = row >= d
        a_sh = pltpu.roll(a3, d, axis=1)
        b_sh = pltpu.roll(b3, d, axis=1)
        b3 = jnp.where(valid, a3 * b_sh + b3, b3)
        a3 = jnp.where(valid, a3 * a_sh, a3)
    a_sc[...] = a3.reshape(ts, LRU_WIDTH)
    b_sc[...] = b3.reshape(ts, LRU_WIDTH)

    def group(gi, hprev):
        rs = pl.ds(pl.multiple_of(gi * SUBLANES, SUBLANES), SUBLANES)
        hg = a_sc[rs, :] * hprev + b_sc[rs, :]
        hh_sc[rs, :] = hg
        return hg[SUBLANES - 1:SUBLANES, :]

    hfin = lax.fori_loop(0, ng, group, h_sc[...])
    h_sc[...] = hfin

    lru_out = hh_sc[...] * _gelu(xg_ref[0])
    o_ref[0] = _rms(lru_out, g_ref[...]).astype(BF16)

    @pl.when(t == nt - 1)
    def _():
        hlast_ref[0] = hfin
        newbuf_ref[0] = tail


def _lru(xr, xg, buf0, h0, cw, cb, wa, ba, wi, bi, lam, g, ts):
    b, s, w = xr.shape
    row = pl.BlockSpec((1, ts, w), lambda bi_, t: (bi_, t, 0))
    return pl.pallas_call(
        functools.partial(_lru_kernel, ts=ts),
        grid=(b, s // ts),
        in_specs=[row, row,
                  pl.BlockSpec((1, CONV_WIDTH - 1, w), lambda bi_, t: (bi_, 0, 0)),
                  pl.BlockSpec((1, 1, w), lambda bi_, t: (bi_, 0, 0)),
                  _full(cw.shape), _full(cb.shape), _full(wa.shape), _full(ba.shape),
                  _full(wi.shape), _full(bi.shape), _full(lam.shape), _full(g.shape)],
        out_specs=(row,
                   pl.BlockSpec((1, 1, w), lambda bi_, t: (bi_, 0, 0)),
                   pl.BlockSpec((1, CONV_WIDTH - 1, w), lambda bi_, t: (bi_, 0, 0))),
        out_shape=(jax.ShapeDtypeStruct((b, s, w), BF16),
                   jax.ShapeDtypeStruct((b, 1, w), F32),
                   jax.ShapeDtypeStruct((b, CONV_WIDTH - 1, w), F32)),
        scratch_shapes=[pltpu.VMEM((ts + SUBLANES, w), F32), pltpu.VMEM((1, w), F32),
                        pltpu.VMEM((ts, w), F32), pltpu.VMEM((ts, w), F32), pltpu.VMEM((ts, w), F32)],
        compiler_params=_params("parallel", "arbitrary"),
        name="lru",
    )(xr, xg, buf0, h0, cw, cb, wa, ba, wi, bi, lam, g)


def _top_rows(cur, n):
    rows = []
    for _ in range(n):
        m = jnp.max(cur, axis=0, keepdims=True)
        rows.append(m)
        cur = jnp.where(cur == m, -jnp.inf, cur)
    return rows


def _mix_kernel(x_ref, mla_ref, lru_ref, wo_ref, gffn_ref, wqt_ref, k1_ref, k2_ref,
                x1_ref, xn2_ref, s1_ref, s2_ref, e1_ref, e2_ref, tau_ref):
    x1 = (x_ref[...]
          + jnp.dot(mla_ref[...], wo_ref[:MLA_WIDTH, :], preferred_element_type=F32)
          + jnp.dot(lru_ref[...], wo_ref[MLA_WIDTH:, :], preferred_element_type=F32))
    x1_ref[...] = x1
    xn2 = _rms(x1, gffn_ref[...]).astype(BF16)
    xn2_ref[...] = xn2
    qt = lax.dot_general(wqt_ref[...], xn2, _NT, preferred_element_type=F32).astype(BF16)
    k = PEER_TOPK
    for h in range(PEER_HEADS):
        base = h * PEER_KEY_DIM
        s1 = jnp.dot(k1_ref[h], qt[base:base + PEER_HALF], preferred_element_type=F32)
        s2 = jnp.dot(k2_ref[h], qt[base + PEER_HALF:base + PEER_KEY_DIM], preferred_element_type=F32)
        v1 = _top_rows(s1, k)
        v2 = _top_rows(s2, k)
        v1a = jnp.concatenate(v1, axis=0)
        cand = [v1a + v2[0]]
        cand += [v1a[:SUBLANES] + v2[j] for j in range(1, SUBLANES)]
        cand += [v1[0] + jnp.concatenate(v2[SUBLANES:], axis=0)]
        vs = _top_rows(jnp.concatenate(cand, axis=0), k)
        z = jnp.ones_like(vs[0])
        for r in range(1, k):
            z = z + jnp.exp(vs[r] - vs[0])
        s1_ref[h] = s1
        s2_ref[h] = s2
        e1_ref[h] = jnp.exp(s1 - v1[0])
        e2_ref[h] = jnp.exp(s2 - v2[0]) / z
        tau_ref[h:h + 1, :] = vs[k - 1]


def _mix(x, mla, lru, wo, gffn, wqt, k1, k2, tm):
    t, d = x.shape
    row = lambda w: pl.BlockSpec((tm, w), lambda i: (i, 0))
    sc = pl.BlockSpec((PEER_HEADS, PEER_NKEYS, tm), lambda i: (0, 0, i))
    sc_shape = jax.ShapeDtypeStruct((PEER_HEADS, PEER_NKEYS, t), F32)
    return pl.pallas_call(
        _mix_kernel,
        grid=(t // tm,),
        in_specs=[row(d), row(MLA_WIDTH), row(LRU_WIDTH), _full(wo.shape), _full(gffn.shape),
                  _full(wqt.shape), _full(k1.shape), _full(k2.shape)],
        out_specs=(row(d), row(d), sc, sc, sc, sc, pl.BlockSpec((PEER_HEADS, tm), lambda i: (0, i))),
        out_shape=(jax.ShapeDtypeStruct((t, d), F32), jax.ShapeDtypeStruct((t, d), BF16),
                   sc_shape, sc_shape, sc_shape, sc_shape,
                   jax.ShapeDtypeStruct((PEER_HEADS, t), F32)),
        compiler_params=_params("parallel"),
        name="mix",
    )(x, mla, lru, wo, gffn, wqt, k1, k2)


def _peer_kernel(xn2_ref, x1_ref, u_ref, vt_ref, s1_ref, s2_ref, e1_ref, e2_ref, tau_ref, gfin_ref,
                 y_ref, ht_sc, wt_sc, yt_sc, *, eb, tb, final_norm):
    e = pl.program_id(1)
    na = eb // PEER_NKEYS

    @pl.when(e == 0)
    def _():
        yt_sc[...] = jnp.zeros(yt_sc.shape, F32)

    ht_sc[...] = lax.dot_general(u_ref[...], xn2_ref[...], _NT, preferred_element_type=F32)

    def lane_body(tl, carry):
        ls = pl.ds(pl.multiple_of(tl * LANES, LANES), LANES)
        a_rows = pl.ds(pl.multiple_of(e * na, SUBLANES), na)
        s1 = [s1_ref[h, a_rows, ls] for h in range(PEER_HEADS)]
        e1 = [e1_ref[h, a_rows, ls] for h in range(PEER_HEADS)]
        for al in range(na):
            rs = slice(al * PEER_NKEYS, (al + 1) * PEER_NKEYS)
            acc = jnp.zeros((PEER_NKEYS, LANES), F32)
            for h in range(PEER_HEADS):
                s = s1[h][al:al + 1] + s2_ref[h, :, ls]
                w = e1[h][al:al + 1] * e2_ref[h, :, ls]
                acc = acc + jnp.where(s >= tau_ref[h:h + 1, ls], w, 0.0)
            wt_sc[rs, ls] = (acc * _gelu(ht_sc[rs, ls])).astype(BF16)
        return carry

    lax.fori_loop(0, tb // LANES, lane_body, 0)
    yt_sc[...] += jnp.dot(vt_ref[...], wt_sc[...], preferred_element_type=F32)

    @pl.when(e == pl.num_programs(1) - 1)
    def _():
        y = x1_ref[...] + yt_sc[...].T
        if final_norm:
            y = _rms(y, gfin_ref[...])
        y_ref[...] = y


def _peer(xn2, x1, u, vt, s1, s2, e1, e2, tau, gfin, tb, eb, final_norm):
    t, d = x1.shape
    ne = u.shape[0]
    assert eb == SUBLANES * PEER_NKEYS and ne % eb == 0 and tb % LANES == 0
    sc = pl.BlockSpec((PEER_HEADS, PEER_NKEYS, tb), lambda i, e: (0, 0, i))
    return pl.pallas_call(
        functools.partial(_peer_kernel, eb=eb, tb=tb, final_norm=final_norm),
        grid=(t // tb, ne // eb),
        in_specs=[pl.BlockSpec((tb, d), lambda i, e: (i, 0)),
                  pl.BlockSpec((tb, d), lambda i, e: (i, 0)),
                  pl.BlockSpec((eb, d), lambda i, e: (e, 0)),
                  pl.BlockSpec((d, eb), lambda i, e: (0, e)),
                  sc, sc, sc, sc,
                  pl.BlockSpec((PEER_HEADS, tb), lambda i, e: (0, i)),
                  _full(gfin.shape)],
        out_specs=pl.BlockSpec((tb, d), lambda i, e: (i, 0)),
        out_shape=jax.ShapeDtypeStruct((t, d), F32),
        scratch_shapes=[pltpu.VMEM((eb, tb), F32), pltpu.VMEM((eb, tb), BF16), pltpu.VMEM((d, tb), F32)],
        compiler_params=_params("parallel", "arbitrary"),
        name="peer",
    )(xn2, x1, u, vt, s1, s2, e1, e2, tau, gfin)


def _rope_tables(pos):
    half = MLA_ROPE // 2
    inv = ROPE_THETA ** (-jnp.arange(half, dtype=F32) / half)
    ang = pos.astype(F32)[:, None] * inv[None, :]
    cos, sin = jnp.cos(ang), jnp.sin(ang)
    zero = jnp.zeros((pos.shape[0], LANES - MLA_ROPE), F32)
    return jnp.concatenate([cos, cos, zero, -sin, sin, zero], axis=-1)


def _swap_halves(w):
    half = w.shape[-1] // 2
    return jnp.concatenate([w[..., half:], w[..., :half]], axis=-1)


def _pad_cols(w, n):
    return jnp.pad(w, ((0, 0), (0, n - w.shape[-1])))


def _layer_weights(w_in, w_uq, w_uk, w_uv, lru_wa, lru_wi, w_out, peer_wq, peer_keys1, peer_keys2,
                   peer_u, peer_v):
    o1 = MLA_QRANK
    o2 = o1 + MLA_KVRANK
    o3 = o2 + MLA_ROPE
    kr = w_in[:, o2:o3]
    win = jnp.concatenate([w_in[:, :o2], _pad_cols(kr, LANES), _pad_cols(_swap_halves(kr), LANES),
                           w_in[:, o3:]], axis=1).astype(BF16)
    wq3 = w_uq.reshape(MLA_QRANK, MLA_HEADS, MLA_NOPE + MLA_ROPE)
    nope = wq3[:, :, :MLA_NOPE].reshape(MLA_QRANK, MLA_HEADS * MLA_NOPE)
    rp = wq3[:, :, MLA_NOPE:]
    padh = lambda w: jnp.pad(w, ((0, 0), (0, 0), (0, LANES - MLA_ROPE))).reshape(MLA_QRANK, MLA_HEADS * LANES)
    wuq = jnp.concatenate([nope, padh(rp), padh(_swap_halves(rp))], axis=1).astype(BF16)
    eye_h = jnp.eye(MLA_HEADS, dtype=F32)
    wuk = jnp.einsum("rhd,hg->hdgr", w_uk, eye_h).reshape(MLA_HEADS * MLA_NOPE, MLA_HEADS * MLA_KVRANK)
    wuv = jnp.einsum("rhd,hg->hrgd", w_uv, eye_h).reshape(MLA_HEADS, MLA_KVRANK, MLA_WIDTH)
    eye_b = jnp.eye(LRU_BLOCKS, dtype=F32)
    bd = lambda w: jnp.einsum("nde,nm->ndme", w, eye_b).reshape(LRU_WIDTH, LRU_WIDTH)
    return dict(win=win, wuq=wuq, wuk=wuk.astype(BF16), wuv=wuv.astype(BF16),
                wa=bd(lru_wa).astype(BF16), wi=bd(lru_wi).astype(BF16), wo=w_out.astype(BF16),
                wqt=peer_wq.T.astype(BF16), k1=peer_keys1.astype(BF16), k2=peer_keys2.astype(BF16),
                u=peer_u.astype(BF16), vt=peer_v.T.astype(BF16))


def _row(v):
    return v.reshape(1, -1).astype(F32)


def _token_tile(n, cap):
    t = min(n, cap)
    assert n % t == 0, (n, t)
    return t


def kernel(x_prompt, x_sample, cache_mla_ckv, cache_mla_krope, state_lru_h, state_lru_conv, norm_mix, w_in, norm_q, w_uq, norm_kv, w_uk, w_uv, conv_w, conv_b, lru_wa, lru_ba, lru_wi, lru_bi, lru_lambda, norm_mla_out, norm_lru_out, w_out, norm_ffn, peer_wq, peer_keys1, peer_keys2, peer_u, peer_v, norm_final):
    bp, sp, d = x_prompt.shape
    bs, sd, _ = x_sample.shape
    depth = w_in.shape[0]
    past = cache_mla_ckv.shape[2]
    ts_tok = bs * sd
    tab_p = _rope_tables(jnp.arange(sp))
    tab_s = jnp.tile(_rope_tables(past + jnp.arange(sd)), (bs, 1))
    gfin = _row(norm_final)

    xp = x_prompt
    xs = x_sample.reshape(1, ts_tok, d)
    outs = [[] for _ in range(8)]
    for l in range(depth):
        w = _layer_weights(w_in[l], w_uq[l], w_uk[l], w_uv[l], lru_wa[l], lru_wi[l], w_out[l], peer_wq[l],
                           peer_keys1[l], peer_keys2[l], peer_u[l], peer_v[l])
        last = l == depth - 1
        lru_args = (conv_w[l].astype(F32), _row(conv_b[l]), w["wa"], _row(lru_ba[l]), w["wi"], _row(lru_bi[l]),
                    _row(lru_lambda[l]), _row(norm_lru_out[l]))
        proj_args = (_row(norm_mix[l]), w["win"], _row(norm_q[l]), w["wuq"], w["wuk"], _row(norm_kv[l]))
        gmla = _row(norm_mla_out[l])

        def tail(x2d, mla, lru):
            t = x2d.shape[0]
            tm = _token_tile(t, 512)
            x1, xn2, s1, s2, e1, e2, tau = _mix(x2d, mla, lru, w["wo"], _row(norm_ffn[l]), w["wqt"],
                                                 w["k1"], w["k2"], tm)
            return _peer(xn2, x1, w["u"], w["vt"], s1, s2, e1, e2, tau, gfin, tm, 1024, last)

        q, kcat, ckv, kr, xr, xg = _proj(xp, tab_p, *proj_args, _token_tile(sp, 512))
        mla = _attn_prompt(q, kcat, w["wuv"], gmla, _token_tile(sp, 256))
        lru, hl, nb = _lru(xr, xg, jnp.zeros((bp, CONV_WIDTH - 1, LRU_WIDTH), F32),
                           jnp.zeros((bp, 1, LRU_WIDTH), F32), *lru_args, _token_tile(sp, 512))
        xp = tail(xp.reshape(bp * sp, d), mla.reshape(bp * sp, MLA_WIDTH),
                  lru.reshape(bp * sp, LRU_WIDTH)).reshape(bp, sp, d)
        for lst, v in zip(outs[:4], (ckv, kr, hl[:, 0], nb)):
            lst.append(v)

        q, kcat, ckv, kr, xr, xg = _proj(xs, tab_s, *proj_args, _token_tile(ts_tok, 512))
        mla = _attn_sample(q, cache_mla_ckv[l], cache_mla_krope[l], kcat, w["wuv"], gmla, bs, sd)
        lru, hl, nb = _lru(xr.reshape(bs, sd, LRU_WIDTH), xg.reshape(bs, sd, LRU_WIDTH),
                           state_lru_conv[l].astype(F32), state_lru_h[l].reshape(bs, 1, LRU_WIDTH).astype(F32),
                           *lru_args, sd)
        xs = tail(xs[0], mla[0], lru.reshape(ts_tok, LRU_WIDTH)).reshape(1, ts_tok, d)
        for lst, v in zip(outs[4:], (ckv.reshape(bs, sd, MLA_KVRANK), kr.reshape(bs, sd, MLA_ROPE), hl[:, 0], nb)):
            lst.append(v)

    return (xp, xs.reshape(bs, sd, d)) + tuple(jnp.stack(o) for o in outs)
```

```python
import functools

import jax
import jax.numpy as jnp
from jax import lax
from jax.experimental import pallas as pl
from jax.experimental.pallas import tpu as pltpu

F32 = jnp.float32
BF16 = jnp.bfloat16

D_MODEL = 1024
CHUNK = 64
EPS = 1e-6
MLA_HEADS = 8
MLA_NOPE = 64
MLA_ROPE = 32
MLA_VDIM = 64
MLA_QRANK = 256
MLA_KVRANK = 128
ROPE_THETA = 10000.0
MLA_WIDTH = MLA_HEADS * MLA_VDIM
MLA_SCALE = (MLA_NOPE + MLA_ROPE) ** -0.5
QK_PAD = 256
LRU_WIDTH = 512
LRU_BLOCKS = 8
LRU_BLOCK_DIM = LRU_WIDTH // LRU_BLOCKS
CONV_WIDTH = 4
LRU_C = 8.0
PEER_HEADS = 8
PEER_NKEYS = 128
PEER_KEY_DIM = 256
PEER_HALF = PEER_KEY_DIM // 2
PEER_TOPK = 16
LANES = 128
SUBLANES = 8
VMEM_LIMIT = 56 * 1024 * 1024

F_ONES = MLA_KVRANK
F_ROPE = F_ONES + 16
V_ROWS = F_ROPE
ROPE_OFF = F_ROPE - LANES
ZC_CQ, ZC_CKV, ZC_KR, ZC_KRS, ZC_XR, ZC_XG, ZC_END = 0, 256, 384, 512, 640, 1152, 1664
QC_NOPE, QC_ROPE, QC_ROPES, QC_END = 0, 512, 1536, 2560
QR_NOPE, QR_ROPE, QR_ROPES, QR_END = 0, 512, 768, 1024

_NT = (((1,), (1,)), ((), ()))


def _rms(x, g):
    return x * lax.rsqrt(jnp.mean(x * x, axis=-1, keepdims=True) + EPS) * g


def _rms_cols(xt, g):
    return xt * lax.rsqrt(jnp.mean(xt * xt, axis=0, keepdims=True) + EPS) * g


def _expm1(y):
    u = jnp.exp(y)
    um1 = u - 1.0
    return jnp.where(um1 == 0.0, y, jnp.where(um1 == -1.0, -1.0, um1 * y / jnp.log(u)))


def _gelu(x):
    return 0.5 * x * (1.0 + lax.erf(x * (2.0 ** -0.5)))


def _params(*sem):
    return pltpu.CompilerParams(dimension_semantics=sem, vmem_limit_bytes=VMEM_LIMIT)


def _full(shape):
    n = len(shape)
    return pl.BlockSpec(shape, lambda *_: (0,) * n)


def _proj_common(x_ref, tab_ref, gmix_ref, win_ref, gkv_ref, kcat_ref, ckv_ref, kr_ref, xr_ref, xg_ref):
    xn = _rms(x_ref[0], gmix_ref[...]).astype(BF16)
    z = jnp.dot(xn, win_ref[...], preferred_element_type=F32)
    ctab = tab_ref[:, :LANES]
    stab = tab_ref[:, LANES:]
    ckv = _rms(z[:, ZC_CKV:ZC_KR], gkv_ref[...])
    kr = z[:, ZC_KR:ZC_KRS] * ctab + z[:, ZC_KRS:ZC_XR] * stab
    ckv_ref[0] = ckv
    kr_ref[0] = kr[:, ROPE_OFF:ROPE_OFF + MLA_ROPE]
    kcat_ref[0, :, :LANES] = ckv.astype(BF16)
    lane = lax.broadcasted_iota(jnp.int32, kr.shape, 1)
    kcat_ref[0, :, LANES:] = jnp.where(lane < ROPE_OFF, 1.0, kr).astype(BF16)
    xr_ref[0] = z[:, ZC_XR:ZC_XG]
    xg_ref[0] = z[:, ZC_XG:ZC_END]
    return xn, z, ctab, stab


def _proj_rows_kernel(x_ref, tab_ref, gmix_ref, win_ref, gkv_ref, gq_ref, wuq_ref, wuk_ref,
                      kcat_ref, ckv_ref, kr_ref, xr_ref, xg_ref, q_ref):
    _, z, ctab, stab = _proj_common(x_ref, tab_ref, gmix_ref, win_ref, gkv_ref,
                                    kcat_ref, ckv_ref, kr_ref, xr_ref, xg_ref)
    cqn = _rms(z[:, ZC_CQ:ZC_CKV], gq_ref[...]).astype(BF16)
    q = jnp.dot(cqn, wuq_ref[...], preferred_element_type=F32)
    qlat = jnp.dot(q[:, QC_NOPE:QC_ROPE].astype(BF16), wuk_ref[...], preferred_element_type=F32)
    for h in range(MLA_HEADS):
        q_ref[0, h, :, :LANES] = (qlat[:, h * LANES:(h + 1) * LANES] * MLA_SCALE).astype(BF16)
        rp = (q[:, QC_ROPE + h * LANES:QC_ROPE + (h + 1) * LANES] * ctab
              + q[:, QC_ROPES + h * LANES:QC_ROPES + (h + 1) * LANES] * stab)
        q_ref[0, h, :, LANES:] = (rp * MLA_SCALE).astype(BF16)


def _proj_cols_kernel(x_ref, tab_ref, gmix_ref, win_ref, gkv_ref, tabt_ref, wint_ref, gqc_ref, gkvc_ref,
                      wuqt_ref, wukt_ref, kcat_ref, ckv_ref, kr_ref, xr_ref, xg_ref, qt_ref, vt_ref):
    xn, _, _, _ = _proj_common(x_ref, tab_ref, gmix_ref, win_ref, gkv_ref,
                               kcat_ref, ckv_ref, kr_ref, xr_ref, xg_ref)
    tm = xn.shape[0]
    zt = lax.dot_general(wint_ref[...], xn, _NT, preferred_element_type=F32)
    ckvt = _rms_cols(zt[MLA_QRANK:], gkvc_ref[...])
    vt_ref[0, :F_ONES, :] = ckvt.astype(BF16)
    vt_ref[0, F_ONES:, :] = jnp.ones((V_ROWS - F_ONES, tm), BF16)
    cqnt = _rms_cols(zt[:MLA_QRANK], gqc_ref[...]).astype(BF16)
    qt = jnp.dot(wuqt_ref[...], cqnt, preferred_element_type=F32)
    qlatt = jnp.dot(wukt_ref[...], qt[QR_NOPE:QR_ROPE].astype(BF16), preferred_element_type=F32)
    ctabt = tabt_ref[:MLA_ROPE, :]
    stabt = tabt_ref[MLA_ROPE:, :]
    for h in range(MLA_HEADS):
        qt_ref[0, h, :F_ONES, :] = (qlatt[h * MLA_KVRANK:(h + 1) * MLA_KVRANK] * MLA_SCALE).astype(BF16)
        qt_ref[0, h, F_ONES:F_ROPE, :] = jnp.zeros((F_ROPE - F_ONES, tm), BF16)
        rp = (qt[QR_ROPE + h * MLA_ROPE:QR_ROPE + (h + 1) * MLA_ROPE] * ctabt
              + qt[QR_ROPES + h * MLA_ROPE:QR_ROPES + (h + 1) * MLA_ROPE] * stabt)
        qt_ref[0, h, F_ROPE:F_ROPE + MLA_ROPE, :] = (rp * MLA_SCALE).astype(BF16)
        qt_ref[0, h, F_ROPE + MLA_ROPE:, :] = jnp.zeros((QK_PAD - F_ROPE - MLA_ROPE, tm), BF16)


def _proj(x, tab, common, extra, tm, cols):
    b, s, d = x.shape
    row = lambda w: pl.BlockSpec((1, tm, w), lambda bi, i: (bi, i, 0))
    shape = lambda *dims: jax.ShapeDtypeStruct((b,) + dims, F32)
    out_specs = [row(QK_PAD), row(MLA_KVRANK), row(MLA_ROPE), row(LRU_WIDTH), row(LRU_WIDTH)]
    out_shape = [jax.ShapeDtypeStruct((b, s, QK_PAD), BF16), shape(s, MLA_KVRANK), shape(s, MLA_ROPE),
                 shape(s, LRU_WIDTH), shape(s, LRU_WIDTH)]
    in_specs = [row(d), pl.BlockSpec((tm, 2 * LANES), lambda bi, i: (i, 0))] + [_full(a.shape) for a in common]
    if cols:
        tabt, rest = extra[0], extra[1:]
        in_specs += [pl.BlockSpec((2 * MLA_ROPE, tm), lambda bi, i: (0, i))] + [_full(a.shape) for a in rest]
        out_specs += [pl.BlockSpec((1, MLA_HEADS, QK_PAD, tm), lambda bi, i: (bi, 0, 0, i)),
                      pl.BlockSpec((1, V_ROWS, tm), lambda bi, i: (bi, 0, i))]
        out_shape += [jax.ShapeDtypeStruct((b, MLA_HEADS, QK_PAD, s), BF16),
                      jax.ShapeDtypeStruct((b, V_ROWS, s), BF16)]
    else:
        in_specs += [_full(a.shape) for a in extra]
        out_specs += [pl.BlockSpec((1, MLA_HEADS, tm, QK_PAD), lambda bi, i: (bi, 0, i, 0))]
        out_shape += [jax.ShapeDtypeStruct((b, MLA_HEADS, s, QK_PAD), BF16)]
    return pl.pallas_call(
        _proj_cols_kernel if cols else _proj_rows_kernel,
        grid=(b, s // tm),
        in_specs=in_specs,
        out_specs=tuple(out_specs),
        out_shape=tuple(out_shape),
        compiler_params=_params("parallel", "parallel"),
        name="proj_cols" if cols else "proj_rows",
    )(x, tab, *common, *extra)


def _attn_prompt_kernel(qt_ref, k_ref, vt_ref, wuvt_ref, g_ref, o_ref,
                        m_sc, acc_sc, st_sc, p_sc, cmax_sc, alpha_sc, *, qb):
    i = pl.program_id(1)
    m_sc[...] = jnp.full(m_sc.shape, -jnp.inf, F32)
    acc_sc[...] = jnp.zeros(acc_sc.shape, F32)

    def step(j, masked):
        ks = pl.ds(pl.multiple_of(j * qb, qb), qb)
        for h in range(MLA_HEADS):
            st = jnp.dot(k_ref[0, ks, :], qt_ref[0, h], preferred_element_type=F32)
            if masked:
                kc = lax.broadcasted_iota(jnp.int32, st.shape, 0) // CHUNK
                qc = lax.broadcasted_iota(jnp.int32, st.shape, 1) // CHUNK
                st = jnp.where(kc <= qc, st, -jnp.inf)
            st_sc[h] = st
            cmax_sc[h:h + 1, :] = jnp.max(st, axis=0, keepdims=True)
        for h in range(MLA_HEADS):
            m_prev = m_sc[h:h + 1, :]
            m_new = jnp.maximum(m_prev, cmax_sc[h:h + 1, :])
            p_sc[h] = jnp.exp(st_sc[h] - m_new).astype(BF16)
            alpha_sc[h:h + 1, :] = jnp.exp(m_prev - m_new)
            m_sc[h:h + 1, :] = m_new
        for h in range(MLA_HEADS):
            acc_sc[h] = (alpha_sc[h:h + 1, :] * acc_sc[h]
                         + jnp.dot(vt_ref[0, :, ks], p_sc[h], preferred_element_type=F32))

    def body(j, carry):
        step(j, False)
        return carry

    lax.fori_loop(0, i, body, 0)
    step(i, True)
    parts = []
    for h in range(MLA_HEADS):
        acc = acc_sc[h]
        att = (acc[:F_ONES] / acc[F_ONES:F_ONES + 1]).astype(BF16)
        parts.append(jnp.dot(wuvt_ref[h], att, preferred_element_type=F32))
    mla = jnp.concatenate(parts, axis=0).T
    o_ref[0] = _rms(mla, g_ref[...]).astype(BF16)


def _attn_prompt(qt, kcat, vt, wuvt, g, qb):
    b, _, _, s = qt.shape
    return pl.pallas_call(
        functools.partial(_attn_prompt_kernel, qb=qb),
        grid=(b, s // qb),
        in_specs=[pl.BlockSpec((1, MLA_HEADS, QK_PAD, qb), lambda bi, i: (bi, 0, 0, i)),
                  pl.BlockSpec((1, s, QK_PAD), lambda bi, i: (bi, 0, 0)),
                  pl.BlockSpec((1, V_ROWS, s), lambda bi, i: (bi, 0, 0)),
                  _full(wuvt.shape), _full(g.shape)],
        out_specs=pl.BlockSpec((1, qb, MLA_WIDTH), lambda bi, i: (bi, i, 0)),
        out_shape=jax.ShapeDtypeStruct((b, s, MLA_WIDTH), BF16),
        scratch_shapes=[pltpu.VMEM((MLA_HEADS, qb), F32), pltpu.VMEM((MLA_HEADS, V_ROWS, qb), F32),
                        pltpu.VMEM((MLA_HEADS, qb, qb), F32), pltpu.VMEM((MLA_HEADS, qb, qb), BF16),
                        pltpu.VMEM((MLA_HEADS, qb), F32), pltpu.VMEM((MLA_HEADS, qb), F32)],
        compiler_params=_params("parallel", "arbitrary"),
        name="attn_prompt",
    )(qt, kcat, vt, wuvt, g)


def _attn_sample_kernel(q_ref, pckv_ref, pkr_ref, k_ref, wuv_ref, g_ref, o_ref, *, sd, past):
    rows = MLA_HEADS * sd
    q = q_ref[0].reshape(rows, QK_PAD)
    knew = k_ref[0]
    s_past = (lax.dot_general(q[:, :LANES], pckv_ref[0].astype(BF16), _NT, preferred_element_type=F32)
              + lax.dot_general(q[:, F_ROPE:F_ROPE + MLA_ROPE], pkr_ref[0].astype(BF16), _NT,
                                preferred_element_type=F32))
    s_new = lax.dot_general(q, knew, _NT, preferred_element_type=F32)

    def mask(s, k0):
        qc = (past + lax.broadcasted_iota(jnp.int32, s.shape, 0) % sd) // CHUNK
        kc = (k0 + lax.broadcasted_iota(jnp.int32, s.shape, 1)) // CHUNK
        return jnp.where(kc <= qc, s, -jnp.inf)

    s_past = mask(s_past, 0)
    s_new = mask(s_new, past)
    m = jnp.maximum(jnp.max(s_past, axis=-1, keepdims=True), jnp.max(s_new, axis=-1, keepdims=True))
    p_past = jnp.exp(s_past - m).astype(BF16)
    p_new = jnp.exp(s_new - m).astype(BF16)
    acc = jnp.dot(p_new, knew, preferred_element_type=F32)
    num = acc[:, :LANES] + jnp.dot(p_past, pckv_ref[0].astype(BF16), preferred_element_type=F32)
    den = acc[:, F_ONES:F_ONES + 1] + jnp.sum(p_past.astype(F32), axis=-1, keepdims=True)
    att = (num / den).astype(BF16)
    mla = jnp.zeros((sd, MLA_WIDTH), F32)
    for h in range(MLA_HEADS):
        mla = mla + jnp.dot(att[h * sd:(h + 1) * sd], wuv_ref[h], preferred_element_type=F32)
    o_ref[0] = _rms(mla, g_ref[...]).astype(BF16)


def _attn_sample(q, pckv, pkr, kcat, wuv, g, nb, sd):
    past = pckv.shape[1]
    return pl.pallas_call(
        functools.partial(_attn_sample_kernel, sd=sd, past=past),
        grid=(nb,),
        in_specs=[pl.BlockSpec((1, MLA_HEADS, sd, QK_PAD), lambda b: (0, 0, b, 0)),
                  pl.BlockSpec((1, past, MLA_KVRANK), lambda b: (b, 0, 0)),
                  pl.BlockSpec((1, past, MLA_ROPE), lambda b: (b, 0, 0)),
                  pl.BlockSpec((1, sd, QK_PAD), lambda b: (0, b, 0)),
                  _full(wuv.shape), _full(g.shape)],
        out_specs=pl.BlockSpec((1, sd, MLA_WIDTH), lambda b: (0, b, 0)),
        out_shape=jax.ShapeDtypeStruct((1, nb * sd, MLA_WIDTH), BF16),
        compiler_params=_params("parallel"),
        name="attn_sample",
    )(q, pckv, pkr, kcat, wuv, g)


def _lru_kernel(xr_ref, xg_ref, buf0_ref, h0_ref, cw_ref, cb_ref, wa_ref, ba_ref, wi_ref, bi_ref,
                lam_ref, g_ref, o_ref, hlast_ref, newbuf_ref, xp_sc, h_sc, a_sc, b_sc, hh_sc, *, ts):
    t = pl.program_id(1)
    nt = pl.num_programs(1)
    pad = SUBLANES

    @pl.when(t == 0)
    def _():
        xp_sc[pad - (CONV_WIDTH - 1):pad, :] = buf0_ref[0]
        h_sc[...] = h0_ref[0]

    xr = xr_ref[0]
    xp_sc[pad:pad + ts, :] = xr
    xc = cb_ref[...] + cw_ref[CONV_WIDTH - 1:CONV_WIDTH, :] * xr
    for k in range(CONV_WIDTH - 1):
        xc = xc + cw_ref[k:k + 1, :] * xp_sc[pad - (CONV_WIDTH - 1) + k:pad - (CONV_WIDTH - 1) + k + ts, :]
    tail = xp_sc[ts + pad - (CONV_WIDTH - 1):ts + pad, :]
    xp_sc[pad - (CONV_WIDTH - 1):pad, :] = tail

    xcb = xc.astype(BF16)
    r = jax.nn.sigmoid(jnp.dot(xcb, wa_ref[...], preferred_element_type=F32) + ba_ref[...])
    ig = jax.nn.sigmoid(jnp.dot(xcb, wi_ref[...], preferred_element_type=F32) + bi_ref[...])
    log_a = -LRU_C * r * jax.nn.softplus(-lam_ref[...])
    a = jnp.exp(log_a)
    bt = jnp.sqrt(-_expm1(2.0 * log_a)) * (ig * xc)

    ng = ts // SUBLANES
    a3 = a.reshape(ng, SUBLANES, LRU_WIDTH)
    b3 = bt.reshape(ng, SUBLANES, LRU_WIDTH)
    row = lax.broadcasted_iota(jnp.int32, a3.shape, 1)
    for d in (1, 2, 4):
        valid = row >= d
        a_sh = pltpu.roll(a3, d, axis=1)
        b_sh = pltpu.roll(b3, d, axis=1)
        b3 = jnp.where(valid, a3 * b_sh + b3, b3)
        a3 = jnp.where(valid, a3 * a_sh, a3)
    a_sc[...] = a3.reshape(ts, LRU_WIDTH)
    b_sc[...] = b3.reshape(ts, LRU_WIDTH)

    def group(gi, hprev):
        rs = pl.ds(pl.multiple_of(gi * SUBLANES, SUBLANES), SUBLANES)
        hg = a_sc[rs, :] * hprev + b_sc[rs, :]
        hh_sc[rs, :] = hg
        return hg[SUBLANES - 1:SUBLANES, :]

    hfin = lax.fori_loop(0, ng, group, h_sc[...])
    h_sc[...] = hfin

    lru_out = hh_sc[...] * _gelu(xg_ref[0])
    o_ref[0] = _rms(lru_out, g_ref[...]).astype(BF16)

    @pl.when(t == nt - 1)
    def _():
        hlast_ref[0] = hfin
        newbuf_ref[0] = tail


def _lru(xr, xg, buf0, h0, cw, cb, wa, ba, wi, bi, lam, g, ts):
    b, s, w = xr.shape
    row = pl.BlockSpec((1, ts, w), lambda bi_, t: (bi_, t, 0))
    return pl.pallas_call(
        functools.partial(_lru_kernel, ts=ts),
        grid=(b, s // ts),
        in_specs=[row, row,
                  pl.BlockSpec((1, CONV_WIDTH - 1, w), lambda bi_, t: (bi_, 0, 0)),
                  pl.BlockSpec((1, 1, w), lambda bi_, t: (bi_, 0, 0)),
                  _full(cw.shape), _full(cb.shape), _full(wa.shape), _full(ba.shape),
                  _full(wi.shape), _full(bi.shape), _full(lam.shape), _full(g.shape)],
        out_specs=(row,
                   pl.BlockSpec((1, 1, w), lambda bi_, t: (bi_, 0, 0)),
                   pl.BlockSpec((1, CONV_WIDTH - 1, w), lambda bi_, t: (bi_, 0, 0))),
        out_shape=(jax.ShapeDtypeStruct((b, s, w), BF16),
                   jax.ShapeDtypeStruct((b, 1, w), F32),
                   jax.ShapeDtypeStruct((b, CONV_WIDTH - 1, w), F32)),
        scratch_shapes=[pltpu.VMEM((ts + SUBLANES, w), F32), pltpu.VMEM((1, w), F32),
                        pltpu.VMEM((ts, w), F32), pltpu.VMEM((ts, w), F32), pltpu.VMEM((ts, w), F32)],
        compiler_params=_params("parallel", "arbitrary"),
        name="lru",
    )(xr, xg, buf0, h0, cw, cb, wa, ba, wi, bi, lam, g)


def _top_rows(cur, n):
    rows = []
    for _ in range(n):
        m = jnp.max(cur, axis=0, keepdims=True)
        rows.append(m)
        cur = jnp.where(cur == m, -jnp.inf, cur)
    return rows


def _mix_kernel(x_ref, mla_ref, lru_ref, wo_ref, gffn_ref, wqt_ref, k1_ref, k2_ref,
                x1_ref, xn2_ref, s1_ref, s2_ref, e1_ref, e2_ref, tau_ref):
    x1 = (x_ref[...]
          + jnp.dot(mla_ref[...], wo_ref[:MLA_WIDTH, :], preferred_element_type=F32)
          + jnp.dot(lru_ref[...], wo_ref[MLA_WIDTH:, :], preferred_element_type=F32))
    x1_ref[...] = x1
    xn2 = _rms(x1, gffn_ref[...]).astype(BF16)
    xn2_ref[...] = xn2
    qt = lax.dot_general(wqt_ref[...], xn2, _NT, preferred_element_type=F32).astype(BF16)
    k = PEER_TOPK
    for h in range(PEER_HEADS):
        base = h * PEER_KEY_DIM
        s1 = jnp.dot(k1_ref[h], qt[base:base + PEER_HALF], preferred_element_type=F32)
        s2 = jnp.dot(k2_ref[h], qt[base + PEER_HALF:base + PEER_KEY_DIM], preferred_element_type=F32)
        v1 = _top_rows(s1, k)
        v2 = _top_rows(s2, k)
        v1a = jnp.concatenate(v1, axis=0)
        cand = [v1a + v2[0]]
        cand += [v1a[:SUBLANES] + v2[j] for j in range(1, SUBLANES)]
        cand += [v1[0] + jnp.concatenate(v2[SUBLANES:], axis=0)]
        vs = _top_rows(jnp.concatenate(cand, axis=0), k)
        z = jnp.ones_like(vs[0])
        for r in range(1, k):
            z = z + jnp.exp(vs[r] - vs[0])
        s1_ref[h] = s1
        s2_ref[h] = s2
        e1_ref[h] = jnp.exp(s1 - v1[0])
        e2_ref[h] = jnp.exp(s2 - v2[0]) / z
        tau_ref[h:h + 1, :] = vs[k - 1]


def _mix(x, mla, lru, wo, gffn, wqt, k1, k2, tm):
    t, d = x.shape
    row = lambda w: pl.BlockSpec((tm, w), lambda i: (i, 0))
    sc = pl.BlockSpec((PEER_HEADS, PEER_NKEYS, tm), lambda i: (0, 0, i))
    sc_shape = jax.ShapeDtypeStruct((PEER_HEADS, PEER_NKEYS, t), F32)
    return pl.pallas_call(
        _mix_kernel,
        grid=(t // tm,),
        in_specs=[row(d), row(MLA_WIDTH), row(LRU_WIDTH), _full(wo.shape), _full(gffn.shape),
                  _full(wqt.shape), _full(k1.shape), _full(k2.shape)],
        out_specs=(row(d), row(d), sc, sc, sc, sc, pl.BlockSpec((PEER_HEADS, tm), lambda i: (0, i))),
        out_shape=(jax.ShapeDtypeStruct((t, d), F32), jax.ShapeDtypeStruct((t, d), BF16),
                   sc_shape, sc_shape, sc_shape, sc_shape,
                   jax.ShapeDtypeStruct((PEER_HEADS, t), F32)),
        compiler_params=_params("parallel"),
        name="mix",
    )(x, mla, lru, wo, gffn, wqt, k1, k2)


def _peer_kernel(xn2_ref, x1_ref, u_ref, vt_ref, s1_ref, s2_ref, e1_ref, e2_ref, tau_ref, gfin_ref,
                 y_ref, ht_sc, wt_sc, yt_sc, *, eb, tb, final_norm):
    e = pl.program_id(1)
    na = eb // PEER_NKEYS

    @pl.when(e == 0)
    def _():
        yt_sc[...] = jnp.zeros(yt_sc.shape, F32)

    ht_sc[...] = lax.dot_general(u_ref[...], xn2_ref[...], _NT, preferred_element_type=F32)

    def lane_body(tl, carry):
        ls = pl.ds(pl.multiple_of(tl * LANES, LANES), LANES)
        a_rows = pl.ds(pl.multiple_of(e * na, SUBLANES), na)
        s1 = [s1_ref[h, a_rows, ls] for h in range(PEER_HEADS)]
        e1 = [e1_ref[h, a_rows, ls] for h in range(PEER_HEADS)]
        for al in range(na):
            rs = slice(al * PEER_NKEYS, (al + 1) * PEER_NKEYS)
            acc = jnp.zeros((PEER_NKEYS, LANES), F32)
            for h in range(PEER_HEADS):
                s = s1[h][al:al + 1] + s2_ref[h, :, ls]
                w = e1[h][al:al + 1] * e2_ref[h, :, ls]
                acc = acc + jnp.where(s >= tau_ref[h:h + 1, ls], w, 0.0)
            wt_sc[rs, ls] = (acc * _gelu(ht_sc[rs, ls])).astype(BF16)
        return carry

    lax.fori_loop(0, tb // LANES, lane_body, 0)
    yt_sc[...] += jnp.dot(vt_ref[...], wt_sc[...], preferred_element_type=F32)

    @pl.when(e == pl.num_programs(1) - 1)
    def _():
        y = x1_ref[...] + yt_sc[...].T
        if final_norm:
            y = _rms(y, gfin_ref[...])
        y_ref[...] = y


def _peer(xn2, x1, u, vt, s1, s2, e1, e2, tau, gfin, tb, eb, final_norm):
    t, d = x1.shape
    ne = u.shape[0]
    assert eb == SUBLANES * PEER_NKEYS and ne % eb == 0 and tb % LANES == 0
    sc = pl.BlockSpec((PEER_HEADS, PEER_NKEYS, tb), lambda i, e: (0, 0, i))
    return pl.pallas_call(
        functools.partial(_peer_kernel, eb=eb, tb=tb, final_norm=final_norm),
        grid=(t // tb, ne // eb),
        in_specs=[pl.BlockSpec((tb, d), lambda i, e: (i, 0)),
                  pl.BlockSpec((tb, d), lambda i, e: (i, 0)),
                  pl.BlockSpec((eb, d), lambda i, e: (e, 0)),
                  pl.BlockSpec((d, eb), lambda i, e: (0, e)),
                  sc, sc, sc, sc,
                  pl.BlockSpec((PEER_HEADS, tb), lambda i, e: (0, i)),
                  _full(gfin.shape)],
        out_specs=pl.BlockSpec((tb, d), lambda i, e: (i, 0)),
        out_shape=jax.ShapeDtypeStruct((t, d), F32),
        scratch_shapes=[pltpu.VMEM((eb, tb), F32), pltpu.VMEM((eb, tb), BF16), pltpu.VMEM((d, tb), F32)],
        compiler_params=_params("parallel", "arbitrary"),
        name="peer",
    )(xn2, x1, u, vt, s1, s2, e1, e2, tau, gfin)


def _rope_tables(pos):
    half = MLA_ROPE // 2
    inv = ROPE_THETA ** (-jnp.arange(half, dtype=F32) / half)
    ang = pos.astype(F32)[:, None] * inv[None, :]
    cos, sin = jnp.cos(ang), jnp.sin(ang)
    c2 = jnp.concatenate([cos, cos], axis=-1)
    s2 = jnp.concatenate([-sin, sin], axis=-1)
    rows = jnp.concatenate([_place_cols(c2), _place_cols(s2)], axis=-1)
    return rows, jnp.concatenate([c2, s2], axis=-1).T


def _swap_halves(w):
    half = w.shape[-1] // 2
    return jnp.concatenate([w[..., half:], w[..., :half]], axis=-1)


def _place_cols(w):
    pad = [(0, 0)] * (w.ndim - 1) + [(ROPE_OFF, LANES - ROPE_OFF - w.shape[-1])]
    return jnp.pad(w, pad)


def _layer_weights(w_in, w_uq, w_uk, w_uv, lru_wa, lru_wi, w_out, peer_wq, peer_keys1, peer_keys2,
                   peer_u, peer_v):
    o1 = MLA_QRANK
    o2 = o1 + MLA_KVRANK
    o3 = o2 + MLA_ROPE
    kr = w_in[:, o2:o3]
    win = jnp.concatenate([w_in[:, :o2], _place_cols(kr), _place_cols(_swap_halves(kr)),
                           w_in[:, o3:]], axis=1).astype(BF16)
    wint = w_in[:, :o2].T.astype(BF16)
    wq3 = w_uq.reshape(MLA_QRANK, MLA_HEADS, MLA_NOPE + MLA_ROPE)
    nope = wq3[:, :, :MLA_NOPE].reshape(MLA_QRANK, MLA_HEADS * MLA_NOPE)
    rp = wq3[:, :, MLA_NOPE:]
    flat = lambda w: w.reshape(MLA_QRANK, -1)
    wuq = jnp.concatenate([nope, flat(_place_cols(rp)), flat(_place_cols(_swap_halves(rp)))], axis=1).astype(BF16)
    wuqt = jnp.concatenate([nope, flat(rp), flat(_swap_halves(rp))], axis=1).T.astype(BF16)
    eye_h = jnp.eye(MLA_HEADS, dtype=F32)
    wuk = jnp.einsum("rhd,hg->hdgr", w_uk, eye_h).reshape(MLA_HEADS * MLA_NOPE, MLA_HEADS * MLA_KVRANK)
    wuv = jnp.einsum("rhd,hg->hrgd", w_uv, eye_h).reshape(MLA_HEADS, MLA_KVRANK, MLA_WIDTH)
    wuvt = jnp.transpose(w_uv, (1, 2, 0))
    eye_b = jnp.eye(LRU_BLOCKS, dtype=F32)
    bd = lambda w: jnp.einsum("nde,nm->ndme", w, eye_b).reshape(LRU_WIDTH, LRU_WIDTH)
    return dict(win=win, wint=wint, wuq=wuq, wuqt=wuqt, wuk=wuk.astype(BF16), wukt=wuk.T.astype(BF16),
                wuv=wuv.astype(BF16), wuvt=wuvt.astype(BF16),
                wa=bd(lru_wa).astype(BF16), wi=bd(lru_wi).astype(BF16), wo=w_out.astype(BF16),
                wqt=peer_wq.T.astype(BF16), k1=peer_keys1.astype(BF16), k2=peer_keys2.astype(BF16),
                u=peer_u.astype(BF16), vt=peer_v.T.astype(BF16))


def _col(v):
    return v.reshape(-1, 1).astype(F32)


def _row(v):
    return v.reshape(1, -1).astype(F32)


def _token_tile(n, cap):
    t = min(n, cap)
    assert n % t == 0, (n, t)
    return t


def kernel(x_prompt, x_sample, cache_mla_ckv, cache_mla_krope, state_lru_h, state_lru_conv, norm_mix, w_in, norm_q, w_uq, norm_kv, w_uk, w_uv, conv_w, conv_b, lru_wa, lru_ba, lru_wi, lru_bi, lru_lambda, norm_mla_out, norm_lru_out, w_out, norm_ffn, peer_wq, peer_keys1, peer_keys2, peer_u, peer_v, norm_final):
    bp, sp, d = x_prompt.shape
    bs, sd, _ = x_sample.shape
    depth = w_in.shape[0]
    past = cache_mla_ckv.shape[2]
    ts_tok = bs * sd
    tab_p, tabt_p = _rope_tables(jnp.arange(sp))
    tab_s = jnp.tile(_rope_tables(past + jnp.arange(sd))[0], (bs, 1))
    gfin = _row(norm_final)

    xp = x_prompt
    xs = x_sample.reshape(1, ts_tok, d)
    outs = [[] for _ in range(8)]
    for l in range(depth):
        w = _layer_weights(w_in[l], w_uq[l], w_uk[l], w_uv[l], lru_wa[l], lru_wi[l], w_out[l], peer_wq[l],
                           peer_keys1[l], peer_keys2[l], peer_u[l], peer_v[l])
        last = l == depth - 1
        lru_args = (conv_w[l].astype(F32), _row(conv_b[l]), w["wa"], _row(lru_ba[l]), w["wi"], _row(lru_bi[l]),
                    _row(lru_lambda[l]), _row(norm_lru_out[l]))
        proj_common = (_row(norm_mix[l]), w["win"], _row(norm_kv[l]))
        proj_rows = (_row(norm_q[l]), w["wuq"], w["wuk"])
        proj_cols = (tabt_p, w["wint"], _col(norm_q[l]), _col(norm_kv[l]), w["wuqt"], w["wukt"])
        gmla = _row(norm_mla_out[l])

        def tail(x2d, mla, lru):
            t = x2d.shape[0]
            tm = _token_tile(t, 512)
            x1, xn2, s1, s2, e1, e2, tau = _mix(x2d, mla, lru, w["wo"], _row(norm_ffn[l]), w["wqt"],
                                                 w["k1"], w["k2"], tm)
            return _peer(xn2, x1, w["u"], w["vt"], s1, s2, e1, e2, tau, gfin, tm, 1024, last)

        kcat, ckv, kr, xr, xg, qt, vt = _proj(xp, tab_p, proj_common, proj_cols, _token_tile(sp, 512), True)
        mla = _attn_prompt(qt, kcat, vt, w["wuvt"], gmla, _token_tile(sp, 256))
        lru, hl, nb = _lru(xr, xg, jnp.zeros((bp, CONV_WIDTH - 1, LRU_WIDTH), F32),
                           jnp.zeros((bp, 1, LRU_WIDTH), F32), *lru_args, _token_tile(sp, 512))
        xp = tail(xp.reshape(bp * sp, d), mla.reshape(bp * sp, MLA_WIDTH),
                  lru.reshape(bp * sp, LRU_WIDTH)).reshape(bp, sp, d)
        for lst, v in zip(outs[:4], (ckv, kr, hl[:, 0], nb)):
            lst.append(v)

        kcat, ckv, kr, xr, xg, q = _proj(xs, tab_s, proj_common, proj_rows, _token_tile(ts_tok, 512), False)
        mla = _attn_sample(q, cache_mla_ckv[l], cache_mla_krope[l], kcat, w["wuv"], gmla, bs, sd)
        lru, hl, nb = _lru(xr.reshape(bs, sd, LRU_WIDTH), xg.reshape(bs, sd, LRU_WIDTH),
                           state_lru_conv[l].astype(F32), state_lru_h[l].reshape(bs, 1, LRU_WIDTH).astype(F32),
                           *lru_args, sd)
        xs = tail(xs[0], mla[0], lru.reshape(ts_tok, LRU_WIDTH)).reshape(1, ts_tok, d)
        for lst, v in zip(outs[4:], (ckv.reshape(bs, sd, MLA_KVRANK), kr.reshape(bs, sd, MLA_ROPE), hl[:, 0], nb)):
            lst.append(v)

    return (xp, xs.reshape(bs, sd, d)) + tuple(jnp.stack(o) for o in outs)
```

```python
import functools

import jax
import jax.numpy as jnp
from jax import lax
from jax.experimental import pallas as pl
from jax.experimental.pallas import tpu as pltpu

F32 = jnp.float32
BF16 = jnp.bfloat16

D_MODEL = 1024
CHUNK = 64
EPS = 1e-6
MLA_HEADS = 8
MLA_NOPE = 64
MLA_ROPE = 32
MLA_VDIM = 64
MLA_QRANK = 256
MLA_KVRANK = 128
ROPE_THETA = 10000.0
MLA_WIDTH = MLA_HEADS * MLA_VDIM
MLA_SCALE = (MLA_NOPE + MLA_ROPE) ** -0.5
QK_PAD = 256
LRU_WIDTH = 512
LRU_BLOCKS = 8
LRU_BLOCK_DIM = LRU_WIDTH // LRU_BLOCKS
CONV_WIDTH = 4
LRU_C = 8.0
PEER_HEADS = 8
PEER_NKEYS = 128
PEER_KEY_DIM = 256
PEER_HALF = PEER_KEY_DIM // 2
PEER_TOPK = 16
LANES = 128
SUBLANES = 8
VMEM_LIMIT = 56 * 1024 * 1024

F_ONES = MLA_KVRANK
F_ROPE = F_ONES + 16
V_ROWS = F_ROPE
ROPE_OFF = F_ROPE - LANES
ZC_CQ, ZC_CKV, ZC_KR, ZC_KRS, ZC_XR, ZC_XG, ZC_END = 0, 256, 384, 512, 640, 1152, 1664
QC_NOPE, QC_ROPE, QC_ROPES, QC_END = 0, 512, 1536, 2560
QR_NOPE, QR_ROPE, QR_ROPES, QR_END = 0, 512, 768, 1024

_NT = (((1,), (1,)), ((), ()))


def _rms(x, g):
    return x * lax.rsqrt(jnp.mean(x * x, axis=-1, keepdims=True) + EPS) * g


def _rms_cols(xt, g):
    return xt * lax.rsqrt(jnp.mean(xt * xt, axis=0, keepdims=True) + EPS) * g


def _expm1(y):
    u = jnp.exp(y)
    um1 = u - 1.0
    return jnp.where(um1 == 0.0, y, jnp.where(um1 == -1.0, -1.0, um1 * y / jnp.log(u)))


def _gelu(x):
    return 0.5 * x * (1.0 + lax.erf(x * (2.0 ** -0.5)))


def _params(*sem):
    return pltpu.CompilerParams(dimension_semantics=sem, vmem_limit_bytes=VMEM_LIMIT)


def _full(shape):
    n = len(shape)
    return pl.BlockSpec(shape, lambda *_: (0,) * n)


def _proj_common(x_ref, tab_ref, gmix_ref, win_ref, gkv_ref, kcat_ref, ckv_ref, kr_ref, xr_ref, xg_ref):
    xn = _rms(x_ref[0], gmix_ref[...]).astype(BF16)
    z = jnp.dot(xn, win_ref[...], preferred_element_type=F32)
    ctab = tab_ref[:, :LANES]
    stab = tab_ref[:, LANES:]
    ckv = _rms(z[:, ZC_CKV:ZC_KR], gkv_ref[...])
    kr = z[:, ZC_KR:ZC_KRS] * ctab + z[:, ZC_KRS:ZC_XR] * stab
    ckv_ref[0] = ckv
    kr_ref[0] = kr[:, ROPE_OFF:ROPE_OFF + MLA_ROPE]
    kcat_ref[0, :, :LANES] = ckv.astype(BF16)
    lane = lax.broadcasted_iota(jnp.int32, kr.shape, 1)
    kcat_ref[0, :, LANES:] = jnp.where(lane < ROPE_OFF, 1.0, kr).astype(BF16)
    xr_ref[0] = z[:, ZC_XR:ZC_XG]
    xg_ref[0] = z[:, ZC_XG:ZC_END]
    return xn, z, ctab, stab


def _proj_rows_kernel(x_ref, tab_ref, gmix_ref, win_ref, gkv_ref, gq_ref, wuq_ref, wuk_ref,
                      kcat_ref, ckv_ref, kr_ref, xr_ref, xg_ref, q_ref):
    _, z, ctab, stab = _proj_common(x_ref, tab_ref, gmix_ref, win_ref, gkv_ref,
                                    kcat_ref, ckv_ref, kr_ref, xr_ref, xg_ref)
    cqn = _rms(z[:, ZC_CQ:ZC_CKV], gq_ref[...]).astype(BF16)
    q = jnp.dot(cqn, wuq_ref[...], preferred_element_type=F32)
    qlat = jnp.dot(q[:, QC_NOPE:QC_ROPE].astype(BF16), wuk_ref[...], preferred_element_type=F32)
    for h in range(MLA_HEADS):
        q_ref[0, h, :, :LANES] = (qlat[:, h * LANES:(h + 1) * LANES] * MLA_SCALE).astype(BF16)
        rp = (q[:, QC_ROPE + h * LANES:QC_ROPE + (h + 1) * LANES] * ctab
              + q[:, QC_ROPES + h * LANES:QC_ROPES + (h + 1) * LANES] * stab)
        q_ref[0, h, :, LANES:] = (rp * MLA_SCALE).astype(BF16)


def _proj_cols_kernel(x_ref, tab_ref, gmix_ref, win_ref, gkv_ref, tabt_ref, wint_ref, gqc_ref, gkvc_ref,
                      wuqt_ref, wukt_ref, kcat_ref, ckv_ref, kr_ref, xr_ref, xg_ref, qt_ref, vt_ref):
    xn, _, _, _ = _proj_common(x_ref, tab_ref, gmix_ref, win_ref, gkv_ref,
                               kcat_ref, ckv_ref, kr_ref, xr_ref, xg_ref)
    tm = xn.shape[0]
    zt = lax.dot_general(wint_ref[...], xn, _NT, preferred_element_type=F32)
    ckvt = _rms_cols(zt[MLA_QRANK:], gkvc_ref[...])
    vt_ref[0, :F_ONES, :] = ckvt.astype(BF16)
    vt_ref[0, F_ONES:, :] = jnp.ones((V_ROWS - F_ONES, tm), BF16)
    cqnt = _rms_cols(zt[:MLA_QRANK], gqc_ref[...]).astype(BF16)
    qt = jnp.dot(wuqt_ref[...], cqnt, preferred_element_type=F32)
    qlatt = jnp.dot(wukt_ref[...], qt[QR_NOPE:QR_ROPE].astype(BF16), preferred_element_type=F32)
    ctabt = tabt_ref[:MLA_ROPE, :]
    stabt = tabt_ref[MLA_ROPE:, :]
    for h in range(MLA_HEADS):
        qt_ref[0, h, :F_ONES, :] = (qlatt[h * MLA_KVRANK:(h + 1) * MLA_KVRANK] * MLA_SCALE).astype(BF16)
        qt_ref[0, h, F_ONES:F_ROPE, :] = jnp.zeros((F_ROPE - F_ONES, tm), BF16)
        rp = (qt[QR_ROPE + h * MLA_ROPE:QR_ROPE + (h + 1) * MLA_ROPE] * ctabt
              + qt[QR_ROPES + h * MLA_ROPE:QR_ROPES + (h + 1) * MLA_ROPE] * stabt)
        qt_ref[0, h, F_ROPE:F_ROPE + MLA_ROPE, :] = (rp * MLA_SCALE).astype(BF16)
        qt_ref[0, h, F_ROPE + MLA_ROPE:, :] = jnp.zeros((QK_PAD - F_ROPE - MLA_ROPE, tm), BF16)


def _proj(x, tab, common, extra, tm, cols):
    b, s, d = x.shape
    row = lambda w: pl.BlockSpec((1, tm, w), lambda bi, i: (bi, i, 0))
    shape = lambda *dims: jax.ShapeDtypeStruct((b,) + dims, F32)
    out_specs = [row(QK_PAD), row(MLA_KVRANK), row(MLA_ROPE), row(LRU_WIDTH), row(LRU_WIDTH)]
    out_shape = [jax.ShapeDtypeStruct((b, s, QK_PAD), BF16), shape(s, MLA_KVRANK), shape(s, MLA_ROPE),
                 shape(s, LRU_WIDTH), shape(s, LRU_WIDTH)]
    in_specs = [row(d), pl.BlockSpec((tm, 2 * LANES), lambda bi, i: (i, 0))] + [_full(a.shape) for a in common]
    if cols:
        tabt, rest = extra[0], extra[1:]
        in_specs += [pl.BlockSpec((2 * MLA_ROPE, tm), lambda bi, i: (0, i))] + [_full(a.shape) for a in rest]
        out_specs += [pl.BlockSpec((1, MLA_HEADS, QK_PAD, tm), lambda bi, i: (bi, 0, 0, i)),
                      pl.BlockSpec((1, V_ROWS, tm), lambda bi, i: (bi, 0, i))]
        out_shape += [jax.ShapeDtypeStruct((b, MLA_HEADS, QK_PAD, s), BF16),
                      jax.ShapeDtypeStruct((b, V_ROWS, s), BF16)]
    else:
        in_specs += [_full(a.shape) for a in extra]
        out_specs += [pl.BlockSpec((1, MLA_HEADS, tm, QK_PAD), lambda bi, i: (bi, 0, i, 0))]
        out_shape += [jax.ShapeDtypeStruct((b, MLA_HEADS, s, QK_PAD), BF16)]
    return pl.pallas_call(
        _proj_cols_kernel if cols else _proj_rows_kernel,
        grid=(b, s // tm),
        in_specs=in_specs,
        out_specs=tuple(out_specs),
        out_shape=tuple(out_shape),
        compiler_params=_params("parallel", "parallel"),
        name="proj_cols" if cols else "proj_rows",
    )(x, tab, *common, *extra)


def _attn_prompt_kernel(qt_ref, k_ref, vt_ref, wuvt_ref, g_ref, o_ref,
                        m_sc, acc_sc, st_sc, p_sc, cmax_sc, alpha_sc, *, qb):
    i = pl.program_id(1)
    m_sc[...] = jnp.full(m_sc.shape, -jnp.inf, F32)
    acc_sc[...] = jnp.zeros(acc_sc.shape, F32)

    def step(j, masked):
        ks = pl.ds(pl.multiple_of(j * qb, qb), qb)
        for h in range(MLA_HEADS):
            st = jnp.dot(k_ref[0, ks, :], qt_ref[0, h], preferred_element_type=F32)
            if masked:
                kc = lax.broadcasted_iota(jnp.int32, st.shape, 0) // CHUNK
                qc = lax.broadcasted_iota(jnp.int32, st.shape, 1) // CHUNK
                st = jnp.where(kc <= qc, st, -jnp.inf)
            st_sc[h] = st
            cmax_sc[h:h + 1, :] = jnp.max(st, axis=0, keepdims=True)
        for h in range(MLA_HEADS):
            m_prev = m_sc[h:h + 1, :]
            m_new = jnp.maximum(m_prev, cmax_sc[h:h + 1, :])
            p_sc[h] = jnp.exp(st_sc[h] - m_new).astype(BF16)
            alpha_sc[h:h + 1, :] = jnp.exp(m_prev - m_new)
            m_sc[h:h + 1, :] = m_new
        for h in range(MLA_HEADS):
            acc_sc[h] = (alpha_sc[h:h + 1, :] * acc_sc[h]
                         + jnp.dot(vt_ref[0, :, ks], p_sc[h], preferred_element_type=F32))

    def body(j, carry):
        step(j, False)
        return carry

    lax.fori_loop(0, i, body, 0)
    step(i, True)
    parts = []
    for h in range(MLA_HEADS):
        acc = acc_sc[h]
        att = (acc[:F_ONES] / acc[F_ONES:F_ONES + 1]).astype(BF16)
        parts.append(jnp.dot(wuvt_ref[h], att, preferred_element_type=F32))
    mla = jnp.concatenate(parts, axis=0).T
    o_ref[0] = _rms(mla, g_ref[...]).astype(BF16)


def _attn_prompt(qt, kcat, vt, wuvt, g, qb):
    b, _, _, s = qt.shape
    return pl.pallas_call(
        functools.partial(_attn_prompt_kernel, qb=qb),
        grid=(b, s // qb),
        in_specs=[pl.BlockSpec((1, MLA_HEADS, QK_PAD, qb), lambda bi, i: (bi, 0, 0, i)),
                  pl.BlockSpec((1, s, QK_PAD), lambda bi, i: (bi, 0, 0)),
                  pl.BlockSpec((1, V_ROWS, s), lambda bi, i: (bi, 0, 0)),
                  _full(wuvt.shape), _full(g.shape)],
        out_specs=pl.BlockSpec((1, qb, MLA_WIDTH), lambda bi, i: (bi, i, 0)),
        out_shape=jax.ShapeDtypeStruct((b, s, MLA_WIDTH), BF16),
        scratch_shapes=[pltpu.VMEM((MLA_HEADS, qb), F32), pltpu.VMEM((MLA_HEADS, V_ROWS, qb), F32),
                        pltpu.VMEM((MLA_HEADS, qb, qb), F32), pltpu.VMEM((MLA_HEADS, qb, qb), BF16),
                        pltpu.VMEM((MLA_HEADS, qb), F32), pltpu.VMEM((MLA_HEADS, qb), F32)],
        compiler_params=_params("parallel", "arbitrary"),
        name="attn_prompt",
    )(qt, kcat, vt, wuvt, g)


def _attn_sample_kernel(q_ref, pckv_ref, pkr_ref, k_ref, wuv_ref, g_ref, o_ref, *, sd, past):
    rows = MLA_HEADS * sd
    q = q_ref[0].reshape(rows, QK_PAD)
    knew = k_ref[0]
    s_past = (lax.dot_general(q[:, :LANES], pckv_ref[0].astype(BF16), _NT, preferred_element_type=F32)
              + lax.dot_general(q[:, F_ROPE:F_ROPE + MLA_ROPE], pkr_ref[0].astype(BF16), _NT,
                                preferred_element_type=F32))
    s_new = lax.dot_general(q, knew, _NT, preferred_element_type=F32)

    def mask(s, k0):
        qc = (past + lax.broadcasted_iota(jnp.int32, s.shape, 0) % sd) // CHUNK
        kc = (k0 + lax.broadcasted_iota(jnp.int32, s.shape, 1)) // CHUNK
        return jnp.where(kc <= qc, s, -jnp.inf)

    s_past = mask(s_past, 0)
    s_new = mask(s_new, past)
    m = jnp.maximum(jnp.max(s_past, axis=-1, keepdims=True), jnp.max(s_new, axis=-1, keepdims=True))
    p_past = jnp.exp(s_past - m).astype(BF16)
    p_new = jnp.exp(s_new - m).astype(BF16)
    acc = jnp.dot(p_new, knew, preferred_element_type=F32)
    num = acc[:, :LANES] + jnp.dot(p_past, pckv_ref[0].astype(BF16), preferred_element_type=F32)
    den = acc[:, F_ONES:F_ONES + 1] + jnp.sum(p_past.astype(F32), axis=-1, keepdims=True)
    att = (num / den).astype(BF16)
    mla = jnp.zeros((sd, MLA_WIDTH), F32)
    for h in range(MLA_HEADS):
        mla = mla + jnp.dot(att[h * sd:(h + 1) * sd], wuv_ref[h], preferred_element_type=F32)
    o_ref[0] = _rms(mla, g_ref[...]).astype(BF16)


def _attn_sample(q, pckv, pkr, kcat, wuv, g, nb, sd):
    past = pckv.shape[1]
    return pl.pallas_call(
        functools.partial(_attn_sample_kernel, sd=sd, past=past),
        grid=(nb,),
        in_specs=[pl.BlockSpec((1, MLA_HEADS, sd, QK_PAD), lambda b: (0, 0, b, 0)),
                  pl.BlockSpec((1, past, MLA_KVRANK), lambda b: (b, 0, 0)),
                  pl.BlockSpec((1, past, MLA_ROPE), lambda b: (b, 0, 0)),
                  pl.BlockSpec((1, sd, QK_PAD), lambda b: (0, b, 0)),
                  _full(wuv.shape), _full(g.shape)],
        out_specs=pl.BlockSpec((1, sd, MLA_WIDTH), lambda b: (0, b, 0)),
        out_shape=jax.ShapeDtypeStruct((1, nb * sd, MLA_WIDTH), BF16),
        compiler_params=_params("parallel"),
        name="attn_sample",
    )(q, pckv, pkr, kcat, wuv, g)


def _lru_kernel(xr_ref, xg_ref, buf0_ref, h0_ref, cw_ref, cb_ref, wa_ref, ba_ref, wi_ref, bi_ref,
                lam_ref, g_ref, o_ref, hlast_ref, newbuf_ref, xp_sc, h_sc, a_sc, b_sc, hh_sc, *, ts):
    t = pl.program_id(1)
    nt = pl.num_programs(1)
    pad = SUBLANES

    @pl.when(t == 0)
    def _():
        xp_sc[pad - (CONV_WIDTH - 1):pad, :] = buf0_ref[0]
        h_sc[...] = h0_ref[0]

    xr = xr_ref[0]
    xp_sc[pad:pad + ts, :] = xr
    xc = cb_ref[...] + cw_ref[CONV_WIDTH - 1:CONV_WIDTH, :] * xr
    for k in range(CONV_WIDTH - 1):
        xc = xc + cw_ref[k:k + 1, :] * xp_sc[pad - (CONV_WIDTH - 1) + k:pad - (CONV_WIDTH - 1) + k + ts, :]
    tail = xp_sc[ts + pad - (CONV_WIDTH - 1):ts + pad, :]
    xp_sc[pad - (CONV_WIDTH - 1):pad, :] = tail

    xcb = xc.astype(BF16)
    r = jax.nn.sigmoid(jnp.dot(xcb, wa_ref[...], preferred_element_type=F32) + ba_ref[...])
    ig = jax.nn.sigmoid(jnp.dot(xcb, wi_ref[...], preferred_element_type=F32) + bi_ref[...])
    log_a = -LRU_C * r * jax.nn.softplus(-lam_ref[...])
    a = jnp.exp(log_a)
    bt = jnp.sqrt(-_expm1(2.0 * log_a)) * (ig * xc)

    ng = ts // SUBLANES
    a3 = a.reshape(ng, SUBLANES, LRU_WIDTH)
    b3 = bt.reshape(ng, SUBLANES, LRU_WIDTH)
    row = lax.broadcasted_iota(jnp.int32, a3.shape, 1)
    for d in (1, 2, 4):
        valid = row >= d
        a_sh = pltpu.roll(a3, d, axis=1)
        b_sh = pltpu.roll(b3, d, axis=1)
        b3 = jnp.where(valid, a3 * b_sh + b3, b3)
        a3 = jnp.where(valid, a3 * a_sh, a3)
    a_sc[...] = a3.reshape(ts, LRU_WIDTH)
    b_sc[...] = b3.reshape(ts, LRU_WIDTH)

    def group(gi, hprev):
        rs = pl.ds(pl.multiple_of(gi * SUBLANES, SUBLANES), SUBLANES)
        hg = a_sc[rs, :] * hprev + b_sc[rs, :]
        hh_sc[rs, :] = hg
        return hg[SUBLANES - 1:SUBLANES, :]

    hfin = lax.fori_loop(0, ng, group, h_sc[...])
    h_sc[...] = hfin

    lru_out = hh_sc[...] * _gelu(xg_ref[0])
    o_ref[0] = _rms(lru_out, g_ref[...]).astype(BF16)

    @pl.when(t == nt - 1)
    def _():
        hlast_ref[0] = hfin
        newbuf_ref[0] = tail


def _lru(xr, xg, buf0, h0, cw, cb, wa, ba, wi, bi, lam, g, ts):
    b, s, w = xr.shape
    row = pl.BlockSpec((1, ts, w), lambda bi_, t: (bi_, t, 0))
    return pl.pallas_call(
        functools.partial(_lru_kernel, ts=ts),
        grid=(b, s // ts),
        in_specs=[row, row,
                  pl.BlockSpec((1, CONV_WIDTH - 1, w), lambda bi_, t: (bi_, 0, 0)),
                  pl.BlockSpec((1, 1, w), lambda bi_, t: (bi_, 0, 0)),
                  _full(cw.shape), _full(cb.shape), _full(wa.shape), _full(ba.shape),
                  _full(wi.shape), _full(bi.shape), _full(lam.shape), _full(g.shape)],
        out_specs=(row,
                   pl.BlockSpec((1, 1, w), lambda bi_, t: (bi_, 0, 0)),
                   pl.BlockSpec((1, CONV_WIDTH - 1, w), lambda bi_, t: (bi_, 0, 0))),
        out_shape=(jax.ShapeDtypeStruct((b, s, w), BF16),
                   jax.ShapeDtypeStruct((b, 1, w), F32),
                   jax.ShapeDtypeStruct((b, CONV_WIDTH - 1, w), F32)),
        scratch_shapes=[pltpu.VMEM((ts + SUBLANES, w), F32), pltpu.VMEM((1, w), F32),
                        pltpu.VMEM((ts, w), F32), pltpu.VMEM((ts, w), F32), pltpu.VMEM((ts, w), F32)],
        compiler_params=_params("parallel", "arbitrary"),
        name="lru",
    )(xr, xg, buf0, h0, cw, cb, wa, ba, wi, bi, lam, g)


def _top_rows(cur, n):
    rows = []
    for _ in range(n):
        m = jnp.max(cur, axis=0, keepdims=True)
        rows.append(m)
        cur = jnp.where(cur == m, -jnp.inf, cur)
    return rows


def _top_ranked(cur, n):
    rows = []
    rank = jnp.full(cur.shape, float(n), F32)
    for r in range(n):
        m = jnp.max(cur, axis=0, keepdims=True)
        hit = cur == m
        rows.append(m)
        rank = jnp.where(hit, float(r), rank)
        cur = jnp.where(hit, -jnp.inf, cur)
    return rows, rank


def _mix_kernel(x_ref, mla_ref, lru_ref, wo_ref, gffn_ref, wqt_ref, k1_ref, k2_ref,
                x1_ref, xn2_ref, rank2_ref, e2_ref, cnt_ref, e1_ref, s1_sc, s2_sc):
    x1 = (x_ref[...]
          + jnp.dot(mla_ref[...], wo_ref[:MLA_WIDTH, :], preferred_element_type=F32)
          + jnp.dot(lru_ref[...], wo_ref[MLA_WIDTH:, :], preferred_element_type=F32))
    x1_ref[...] = x1
    xn2t = _rms(x1, gffn_ref[...]).T.astype(BF16)
    xn2_ref[...] = xn2t
    qt = jnp.dot(wqt_ref[...], xn2t, preferred_element_type=F32).astype(BF16)
    for h in range(PEER_HEADS):
        base = h * PEER_KEY_DIM
        s1_sc[h] = jnp.dot(k1_ref[h], qt[base:base + PEER_HALF], preferred_element_type=F32)
        s2_sc[h] = jnp.dot(k2_ref[h], qt[base + PEER_HALF:base + PEER_KEY_DIM], preferred_element_type=F32)
    k = PEER_TOPK
    groups = x1.shape[0] // LANES

    def body(it, carry):
        h = it // groups
        ls = pl.ds(pl.multiple_of((it % groups) * LANES, LANES), LANES)
        s1 = s1_sc[h, :, ls]
        s2 = s2_sc[h, :, ls]
        v1, rank1 = _top_ranked(s1, k)
        v2, rank2 = _top_ranked(s2, k)
        v1a = jnp.concatenate(v1, axis=0)
        cand = [v1a + v2[0]]
        cand += [v1a[:SUBLANES] + v2[j] for j in range(1, SUBLANES)]
        cand += [v1[0] + jnp.concatenate(v2[SUBLANES:], axis=0)]
        vs = _top_rows(jnp.concatenate(cand, axis=0), k)
        tau = vs[k - 1]
        sel = [c >= tau for c in cand]
        z = sum(jnp.sum(jnp.where(m, jnp.exp(c - vs[0]), 0.0), axis=0, keepdims=True)
                for m, c in zip(sel, cand))
        ones = [jnp.where(m, 1.0, 0.0) for m in sel]
        low = sum(ones[1:SUBLANES])
        cnt = ones[0] + jnp.concatenate([low, jnp.zeros_like(low)], axis=0)
        tail = jnp.sum(ones[SUBLANES], axis=0, keepdims=True)
        cnt_a = jnp.zeros(s1.shape, F32)
        for i in range(k):
            ci = cnt[i:i + 1] + tail if i == 0 else cnt[i:i + 1]
            cnt_a = jnp.where(rank1 == float(i), ci, cnt_a)
        rank2_ref[h, :, ls] = rank2
        e2_ref[h, :, ls] = jnp.exp(s2 - v2[0]) / z
        cnt_ref[h, :, ls] = cnt_a
        e1_ref[h, :, ls] = jnp.exp(s1 - v1[0])
        return carry

    lax.fori_loop(0, PEER_HEADS * groups, body, 0)


def _mix(x, mla, lru, wo, gffn, wqt, k1, k2, tm):
    t, d = x.shape
    row = lambda w: pl.BlockSpec((tm, w), lambda i: (i, 0))
    sc = pl.BlockSpec((PEER_HEADS, PEER_NKEYS, tm), lambda i: (0, 0, i))
    sc_shape = lambda dt: jax.ShapeDtypeStruct((PEER_HEADS, PEER_NKEYS, t), dt)
    return pl.pallas_call(
        _mix_kernel,
        grid=(t // tm,),
        in_specs=[row(d), row(MLA_WIDTH), row(LRU_WIDTH), _full(wo.shape), _full(gffn.shape),
                  _full(wqt.shape), _full(k1.shape), _full(k2.shape)],
        out_specs=(row(d), pl.BlockSpec((d, tm), lambda i: (0, i)), sc, sc, sc, sc),
        out_shape=(jax.ShapeDtypeStruct((t, d), F32), jax.ShapeDtypeStruct((d, t), BF16),
                   sc_shape(F32), sc_shape(F32), sc_shape(F32), sc_shape(F32)),
        scratch_shapes=[pltpu.VMEM((PEER_HEADS, PEER_NKEYS, tm), F32)] * 2,
        compiler_params=_params("parallel"),
        name="mix",
    )(x, mla, lru, wo, gffn, wqt, k1, k2)


def _peer_kernel(xn2_ref, x1_ref, u_ref, vt_ref, rank2_ref, e2_ref, cnt_ref, e1_ref, gfin_ref,
                 y_ref, ht0_sc, ht1_sc, wt0_sc, wt1_sc, yt_sc, *, eb, tb, final_norm):
    e = pl.program_id(1)

    @pl.when(e == 0)
    def _():
        for ref in (ht0_sc, ht1_sc, wt0_sc, wt1_sc, yt_sc):
            ref[...] = jnp.zeros(ref.shape, ref.dtype)

    na = eb // PEER_NKEYS
    mh_rows = eb // 2
    nt_cols = 2 * LANES
    bq_rows = PEER_NKEYS // 4

    def stages(ht_cur, wt_cur, ht_prev, wt_prev):
        def pre_piece(mh, nt):
            ms = slice(mh * mh_rows, (mh + 1) * mh_rows)
            ns = slice(nt * nt_cols, (nt + 1) * nt_cols)
            ht_cur[ms, ns] = jnp.dot(u_ref[ms, :], xn2_ref[:, ns], preferred_element_type=F32)

        def out_piece(mh, nt):
            ms = slice(mh * (yt_sc.shape[0] // 2), (mh + 1) * (yt_sc.shape[0] // 2))
            ns = slice(nt * nt_cols, (nt + 1) * nt_cols)
            yt_sc[ms, ns] += jnp.dot(vt_ref[ms, :], wt_cur[:, ns], preferred_element_type=F32)

        def mix_block(tl, bq):
            ls = slice(tl * LANES, (tl + 1) * LANES)
            bs = slice(bq * bq_rows, (bq + 1) * bq_rows)
            acc = [jnp.zeros((bq_rows, LANES), F32)] * na
            for h in range(PEER_HEADS):
                r2 = rank2_ref[h, bs, ls]
                e2 = e2_ref[h, bs, ls]
                for al in range(na):
                    hit = r2 < cnt_ref[h, al:al + 1, ls]
                    acc[al] = acc[al] + jnp.where(hit, e1_ref[h, al:al + 1, ls] * e2, 0.0)
            for al in range(na):
                rs = slice(al * PEER_NKEYS + bq * bq_rows, al * PEER_NKEYS + (bq + 1) * bq_rows)
                wt_prev[rs, ls] = (acc[al] * _gelu(ht_prev[rs, ls])).astype(BF16)

        pieces = [(f, mh, nt) for nt in range(tb // nt_cols) for mh in range(2) for f in (pre_piece, out_piece)]
        blocks = [(tl, bq) for tl in range(tb // LANES) for bq in range(PEER_NKEYS // bq_rows)]
        per = -(-len(blocks) // len(pieces))
        for i, (f, mh, nt) in enumerate(pieces):
            f(mh, nt)
            for tl, bq in blocks[i * per:(i + 1) * per]:
                mix_block(tl, bq)

    @pl.when(e % 2 == 0)
    def _():
        stages(ht0_sc, wt0_sc, ht1_sc, wt1_sc)

    @pl.when(e % 2 == 1)
    def _():
        stages(ht1_sc, wt1_sc, ht0_sc, wt0_sc)

    @pl.when(e == pl.num_programs(1) - 1)
    def _():
        y = x1_ref[...] + yt_sc[...].T
        if final_norm:
            y = _rms(y, gfin_ref[...])
        y_ref[...] = y


def _peer(xn2, x1, u, vt, rank2, e2, cnt, e1, gfin, tb, eb, final_norm):
    t, d = x1.shape
    ne = u.shape[0] // eb
    na = eb // PEER_NKEYS
    assert na == SUBLANES and u.shape[0] % eb == 0 and tb % (2 * LANES) == 0
    tile = lambda e, lag: jnp.clip(e - lag, 0, ne - 1)
    sc = pl.BlockSpec((PEER_HEADS, PEER_NKEYS, tb), lambda i, e: (0, 0, i))
    rows = pl.BlockSpec((PEER_HEADS, na, tb), lambda i, e: (0, tile(e, 1), i))
    return pl.pallas_call(
        functools.partial(_peer_kernel, eb=eb, tb=tb, final_norm=final_norm),
        grid=(t // tb, ne + 2),
        in_specs=[pl.BlockSpec((d, tb), lambda i, e: (0, i)),
                  pl.BlockSpec((tb, d), lambda i, e: (i, 0)),
                  pl.BlockSpec((eb, d), lambda i, e: (tile(e, 0), 0)),
                  pl.BlockSpec((d, eb), lambda i, e: (0, tile(e, 2))),
                  sc, sc, rows, rows,
                  _full(gfin.shape)],
        out_specs=pl.BlockSpec((tb, d), lambda i, e: (i, 0)),
        out_shape=jax.ShapeDtypeStruct((t, d), F32),
        scratch_shapes=[pltpu.VMEM((eb, tb), F32), pltpu.VMEM((eb, tb), F32),
                        pltpu.VMEM((eb, tb), BF16), pltpu.VMEM((eb, tb), BF16), pltpu.VMEM((d, tb), F32)],
        compiler_params=_params("parallel", "arbitrary"),
        name="peer",
    )(xn2, x1, u, vt, rank2, e2, cnt, e1, gfin)


def _rope_tables(pos):
    half = MLA_ROPE // 2
    inv = ROPE_THETA ** (-jnp.arange(half, dtype=F32) / half)
    ang = pos.astype(F32)[:, None] * inv[None, :]
    cos, sin = jnp.cos(ang), jnp.sin(ang)
    c2 = jnp.concatenate([cos, cos], axis=-1)
    s2 = jnp.concatenate([-sin, sin], axis=-1)
    rows = jnp.concatenate([_place_cols(c2), _place_cols(s2)], axis=-1)
    return rows, jnp.concatenate([c2, s2], axis=-1).T


def _swap_halves(w):
    half = w.shape[-1] // 2
    return jnp.concatenate([w[..., half:], w[..., :half]], axis=-1)


def _place_cols(w):
    pad = [(0, 0)] * (w.ndim - 1) + [(ROPE_OFF, LANES - ROPE_OFF - w.shape[-1])]
    return jnp.pad(w, pad)


def _layer_weights(w_in, w_uq, w_uk, w_uv, lru_wa, lru_wi, w_out, peer_wq, peer_keys1, peer_keys2,
                   peer_u, peer_v):
    o1 = MLA_QRANK
    o2 = o1 + MLA_KVRANK
    o3 = o2 + MLA_ROPE
    kr = w_in[:, o2:o3]
    win = jnp.concatenate([w_in[:, :o2], _place_cols(kr), _place_cols(_swap_halves(kr)),
                           w_in[:, o3:]], axis=1).astype(BF16)
    wint = w_in[:, :o2].T.astype(BF16)
    wq3 = w_uq.reshape(MLA_QRANK, MLA_HEADS, MLA_NOPE + MLA_ROPE)
    nope = wq3[:, :, :MLA_NOPE].reshape(MLA_QRANK, MLA_HEADS * MLA_NOPE)
    rp = wq3[:, :, MLA_NOPE:]
    flat = lambda w: w.reshape(MLA_QRANK, -1)
    wuq = jnp.concatenate([nope, flat(_place_cols(rp)), flat(_place_cols(_swap_halves(rp)))], axis=1).astype(BF16)
    wuqt = jnp.concatenate([nope, flat(rp), flat(_swap_halves(rp))], axis=1).T.astype(BF16)
    eye_h = jnp.eye(MLA_HEADS, dtype=F32)
    wuk = jnp.einsum("rhd,hg->hdgr", w_uk, eye_h).reshape(MLA_HEADS * MLA_NOPE, MLA_HEADS * MLA_KVRANK)
    wuv = jnp.einsum("rhd,hg->hrgd", w_uv, eye_h).reshape(MLA_HEADS, MLA_KVRANK, MLA_WIDTH)
    wuvt = jnp.transpose(w_uv, (1, 2, 0))
    eye_b = jnp.eye(LRU_BLOCKS, dtype=F32)
    bd = lambda w: jnp.einsum("nde,nm->ndme", w, eye_b).reshape(LRU_WIDTH, LRU_WIDTH)
    return dict(win=win, wint=wint, wuq=wuq, wuqt=wuqt, wuk=wuk.astype(BF16), wukt=wuk.T.astype(BF16),
                wuv=wuv.astype(BF16), wuvt=wuvt.astype(BF16),
                wa=bd(lru_wa).astype(BF16), wi=bd(lru_wi).astype(BF16), wo=w_out.astype(BF16),
                wqt=peer_wq.T.astype(BF16), k1=peer_keys1.astype(BF16), k2=peer_keys2.astype(BF16),
                u=peer_u.astype(BF16), vt=peer_v.T.astype(BF16))


def _col(v):
    return v.reshape(-1, 1).astype(F32)


def _row(v):
    return v.reshape(1, -1).astype(F32)


def _token_tile(n, cap):
    t = min(n, cap)
    assert n % t == 0, (n, t)
    return t


def kernel(x_prompt, x_sample, cache_mla_ckv, cache_mla_krope, state_lru_h, state_lru_conv, norm_mix, w_in, norm_q, w_uq, norm_kv, w_uk, w_uv, conv_w, conv_b, lru_wa, lru_ba, lru_wi, lru_bi, lru_lambda, norm_mla_out, norm_lru_out, w_out, norm_ffn, peer_wq, peer_keys1, peer_keys2, peer_u, peer_v, norm_final):
    bp, sp, d = x_prompt.shape
    bs, sd, _ = x_sample.shape
    depth = w_in.shape[0]
    past = cache_mla_ckv.shape[2]
    ts_tok = bs * sd
    tab_p, tabt_p = _rope_tables(jnp.arange(sp))
    tab_s = jnp.tile(_rope_tables(past + jnp.arange(sd))[0], (bs, 1))
    gfin = _row(norm_final)

    xp = x_prompt
    xs = x_sample.reshape(1, ts_tok, d)
    outs = [[] for _ in range(8)]
    for l in range(depth):
        w = _layer_weights(w_in[l], w_uq[l], w_uk[l], w_uv[l], lru_wa[l], lru_wi[l], w_out[l], peer_wq[l],
                           peer_keys1[l], peer_keys2[l], peer_u[l], peer_v[l])
        last = l == depth - 1
        lru_args = (conv_w[l].astype(F32), _row(conv_b[l]), w["wa"], _row(lru_ba[l]), w["wi"], _row(lru_bi[l]),
                    _row(lru_lambda[l]), _row(norm_lru_out[l]))
        proj_common = (_row(norm_mix[l]), w["win"], _row(norm_kv[l]))
        proj_rows = (_row(norm_q[l]), w["wuq"], w["wuk"])
        proj_cols = (tabt_p, w["wint"], _col(norm_q[l]), _col(norm_kv[l]), w["wuqt"], w["wukt"])
        gmla = _row(norm_mla_out[l])

        def tail(x2d, mla, lru):
            t = x2d.shape[0]
            tm = _token_tile(t, 512)
            x1, xn2, rank2, e2, cnt, e1 = _mix(x2d, mla, lru, w["wo"], _row(norm_ffn[l]), w["wqt"],
                                               w["k1"], w["k2"], tm)
            return _peer(xn2, x1, w["u"], w["vt"], rank2, e2, cnt, e1, gfin, tm, 1024, last)

        kcat, ckv, kr, xr, xg, qt, vt = _proj(xp, tab_p, proj_common, proj_cols, _token_tile(sp, 512), True)
        mla = _attn_prompt(qt, kcat, vt, w["wuvt"], gmla, _token_tile(sp, 256))
        lru, hl, nb = _lru(xr, xg, jnp.zeros((bp, CONV_WIDTH - 1, LRU_WIDTH), F32),
                           jnp.zeros((bp, 1, LRU_WIDTH), F32), *lru_args, _token_tile(sp, 512))
        xp = tail(xp.reshape(bp * sp, d), mla.reshape(bp * sp, MLA_WIDTH),
                  lru.reshape(bp * sp, LRU_WIDTH)).reshape(bp, sp, d)
        for lst, v in zip(outs[:4], (ckv, kr, hl[:, 0], nb)):
            lst.append(v)

        kcat, ckv, kr, xr, xg, q = _proj(xs, tab_s, proj_common, proj_rows, _token_tile(ts_tok, 512), False)
        mla = _attn_sample(q, cache_mla_ckv[l], cache_mla_krope[l], kcat, w["wuv"], gmla, bs, sd)
        lru, hl, nb = _lru(xr.reshape(bs, sd, LRU_WIDTH), xg.reshape(bs, sd, LRU_WIDTH),
                           state_lru_conv[l].astype(F32), state_lru_h[l].reshape(bs, 1, LRU_WIDTH).astype(F32),
                           *lru_args, sd)
        xs = tail(xs[0], mla[0], lru.reshape(ts_tok, LRU_WIDTH)).reshape(1, ts_tok, d)
        for lst, v in zip(outs[4:], (ckv.reshape(bs, sd, MLA_KVRANK), kr.reshape(bs, sd, MLA_ROPE), hl[:, 0], nb)):
            lst.append(v)

    return (xp, xs.reshape(bs, sd, d)) + tuple(jnp.stack(o) for o in outs)
```

```python
import functools

import jax
import jax.numpy as jnp
from jax import lax
from jax.experimental import pallas as pl
from jax.experimental.pallas import tpu as pltpu

F32 = jnp.float32
BF16 = jnp.bfloat16

D_MODEL = 1024
CHUNK = 64
EPS = 1e-6
MLA_HEADS = 8
MLA_NOPE = 64
MLA_ROPE = 32
MLA_VDIM = 64
MLA_QRANK = 256
MLA_KVRANK = 128
ROPE_THETA = 10000.0
MLA_WIDTH = MLA_HEADS * MLA_VDIM
MLA_SCALE = (MLA_NOPE + MLA_ROPE) ** -0.5
QK_PAD = 256
LRU_WIDTH = 512
LRU_BLOCKS = 8
LRU_BLOCK_DIM = LRU_WIDTH // LRU_BLOCKS
CONV_WIDTH = 4
LRU_C = 8.0
PEER_HEADS = 8
PEER_NKEYS = 128
PEER_KEY_DIM = 256
PEER_HALF = PEER_KEY_DIM // 2
PEER_TOPK = 16
LANES = 128
SUBLANES = 8
VMEM_LIMIT = 56 * 1024 * 1024

F_ONES = MLA_KVRANK
F_ROPE = F_ONES + 16
V_ROWS = F_ROPE
ROPE_OFF = F_ROPE - LANES
ZC_CQ, ZC_CKV, ZC_KR, ZC_KRS, ZC_XR, ZC_XG, ZC_END = 0, 256, 384, 512, 640, 1152, 1664
QC_NOPE, QC_ROPE, QC_ROPES, QC_END = 0, 512, 1536, 2560
QR_NOPE, QR_ROPE, QR_ROPES, QR_END = 0, 512, 768, 1024

_NT = (((1,), (1,)), ((), ()))


def _rms(x, g):
    return x * lax.rsqrt(jnp.mean(x * x, axis=-1, keepdims=True) + EPS) * g


def _rms_cols(xt, g):
    return xt * lax.rsqrt(jnp.mean(xt * xt, axis=0, keepdims=True) + EPS) * g


def _expm1(y):
    u = jnp.exp(y)
    um1 = u - 1.0
    return jnp.where(um1 == 0.0, y, jnp.where(um1 == -1.0, -1.0, um1 * y / jnp.log(u)))


def _gelu(x):
    return 0.5 * x * (1.0 + lax.erf(x * (2.0 ** -0.5)))


def _params(*sem):
    return pltpu.CompilerParams(dimension_semantics=sem, vmem_limit_bytes=VMEM_LIMIT)


def _full(shape):
    n = len(shape)
    return pl.BlockSpec(shape, lambda *_: (0,) * n)


def _proj_common(x_ref, tab_ref, gmix_ref, win_ref, gkv_ref, kcat_ref, ckv_ref, kr_ref, xr_ref, xg_ref):
    xn = _rms(x_ref[0], gmix_ref[...]).astype(BF16)
    z = jnp.dot(xn, win_ref[...], preferred_element_type=F32)
    ctab = tab_ref[:, :LANES]
    stab = tab_ref[:, LANES:]
    ckv = _rms(z[:, ZC_CKV:ZC_KR], gkv_ref[...])
    kr = z[:, ZC_KR:ZC_KRS] * ctab + z[:, ZC_KRS:ZC_XR] * stab
    ckv_ref[0] = ckv
    kr_ref[0] = kr[:, ROPE_OFF:ROPE_OFF + MLA_ROPE]
    kcat_ref[0, :, :LANES] = ckv.astype(BF16)
    lane = lax.broadcasted_iota(jnp.int32, kr.shape, 1)
    kcat_ref[0, :, LANES:] = jnp.where(lane < ROPE_OFF, 1.0, kr).astype(BF16)
    xr_ref[0] = z[:, ZC_XR:ZC_XG]
    xg_ref[0] = z[:, ZC_XG:ZC_END]
    return xn, z, ctab, stab


def _proj_rows_kernel(x_ref, tab_ref, gmix_ref, win_ref, gkv_ref, gq_ref, wuq_ref, wuk_ref,
                      kcat_ref, ckv_ref, kr_ref, xr_ref, xg_ref, q_ref):
    _, z, ctab, stab = _proj_common(x_ref, tab_ref, gmix_ref, win_ref, gkv_ref,
                                    kcat_ref, ckv_ref, kr_ref, xr_ref, xg_ref)
    cqn = _rms(z[:, ZC_CQ:ZC_CKV], gq_ref[...]).astype(BF16)
    q = jnp.dot(cqn, wuq_ref[...], preferred_element_type=F32)
    qlat = jnp.dot(q[:, QC_NOPE:QC_ROPE].astype(BF16), wuk_ref[...], preferred_element_type=F32)
    for h in range(MLA_HEADS):
        q_ref[0, h, :, :LANES] = (qlat[:, h * LANES:(h + 1) * LANES] * MLA_SCALE).astype(BF16)
        rp = (q[:, QC_ROPE + h * LANES:QC_ROPE + (h + 1) * LANES] * ctab
              + q[:, QC_ROPES + h * LANES:QC_ROPES + (h + 1) * LANES] * stab)
        q_ref[0, h, :, LANES:] = (rp * MLA_SCALE).astype(BF16)


def _proj_cols_kernel(x_ref, tab_ref, gmix_ref, win_ref, gkv_ref, tabt_ref, wint_ref, gqc_ref, gkvc_ref,
                      wuqt_ref, wukt_ref, kcat_ref, ckv_ref, kr_ref, xr_ref, xg_ref, qt_ref, vt_ref):
    xn, _, _, _ = _proj_common(x_ref, tab_ref, gmix_ref, win_ref, gkv_ref,
                               kcat_ref, ckv_ref, kr_ref, xr_ref, xg_ref)
    tm = xn.shape[0]
    zt = lax.dot_general(wint_ref[...], xn, _NT, preferred_element_type=F32)
    ckvt = _rms_cols(zt[MLA_QRANK:], gkvc_ref[...])
    vt_ref[0, :F_ONES, :] = ckvt.astype(BF16)
    vt_ref[0, F_ONES:, :] = jnp.ones((V_ROWS - F_ONES, tm), BF16)
    cqnt = _rms_cols(zt[:MLA_QRANK], gqc_ref[...]).astype(BF16)
    qt = jnp.dot(wuqt_ref[...], cqnt, preferred_element_type=F32)
    qlatt = jnp.dot(wukt_ref[...], qt[QR_NOPE:QR_ROPE].astype(BF16), preferred_element_type=F32)
    ctabt = tabt_ref[:MLA_ROPE, :]
    stabt = tabt_ref[MLA_ROPE:, :]
    for h in range(MLA_HEADS):
        qt_ref[0, h, :F_ONES, :] = (qlatt[h * MLA_KVRANK:(h + 1) * MLA_KVRANK] * MLA_SCALE).astype(BF16)
        qt_ref[0, h, F_ONES:F_ROPE, :] = jnp.zeros((F_ROPE - F_ONES, tm), BF16)
        rp = (qt[QR_ROPE + h * MLA_ROPE:QR_ROPE + (h + 1) * MLA_ROPE] * ctabt
              + qt[QR_ROPES + h * MLA_ROPE:QR_ROPES + (h + 1) * MLA_ROPE] * stabt)
        qt_ref[0, h, F_ROPE:F_ROPE + MLA_ROPE, :] = (rp * MLA_SCALE).astype(BF16)
        qt_ref[0, h, F_ROPE + MLA_ROPE:, :] = jnp.zeros((QK_PAD - F_ROPE - MLA_ROPE, tm), BF16)


def _proj(x, tab, common, extra, tm, cols):
    b, s, d = x.shape
    row = lambda w: pl.BlockSpec((1, tm, w), lambda bi, i: (bi, i, 0))
    shape = lambda *dims: jax.ShapeDtypeStruct((b,) + dims, F32)
    out_specs = [row(QK_PAD), row(MLA_KVRANK), row(MLA_ROPE), row(LRU_WIDTH), row(LRU_WIDTH)]
    out_shape = [jax.ShapeDtypeStruct((b, s, QK_PAD), BF16), shape(s, MLA_KVRANK), shape(s, MLA_ROPE),
                 shape(s, LRU_WIDTH), shape(s, LRU_WIDTH)]
    in_specs = [row(d), pl.BlockSpec((tm, 2 * LANES), lambda bi, i: (i, 0))] + [_full(a.shape) for a in common]
    if cols:
        tabt, rest = extra[0], extra[1:]
        in_specs += [pl.BlockSpec((2 * MLA_ROPE, tm), lambda bi, i: (0, i))] + [_full(a.shape) for a in rest]
        out_specs += [pl.BlockSpec((1, MLA_HEADS, QK_PAD, tm), lambda bi, i: (bi, 0, 0, i)),
                      pl.BlockSpec((1, V_ROWS, tm), lambda bi, i: (bi, 0, i))]
        out_shape += [jax.ShapeDtypeStruct((b, MLA_HEADS, QK_PAD, s), BF16),
                      jax.ShapeDtypeStruct((b, V_ROWS, s), BF16)]
    else:
        in_specs += [_full(a.shape) for a in extra]
        out_specs += [pl.BlockSpec((1, MLA_HEADS, tm, QK_PAD), lambda bi, i: (bi, 0, i, 0))]
        out_shape += [jax.ShapeDtypeStruct((b, MLA_HEADS, s, QK_PAD), BF16)]
    return pl.pallas_call(
        _proj_cols_kernel if cols else _proj_rows_kernel,
        grid=(b, s // tm),
        in_specs=in_specs,
        out_specs=tuple(out_specs),
        out_shape=tuple(out_shape),
        compiler_params=_params("parallel", "parallel"),
        name="proj_cols" if cols else "proj_rows",
    )(x, tab, *common, *extra)


def _attn_prompt_kernel(qt_ref, k_ref, vt_ref, wuvt_ref, g_ref, o_ref,
                        m_sc, acc_sc, st_sc, p_sc, cmax_sc, alpha_sc, *, qb):
    i = pl.program_id(1)
    m_sc[...] = jnp.full(m_sc.shape, -jnp.inf, F32)
    acc_sc[...] = jnp.zeros(acc_sc.shape, F32)

    def step(j, masked):
        ks = pl.ds(pl.multiple_of(j * qb, qb), qb)
        for h in range(MLA_HEADS):
            st = jnp.dot(k_ref[0, ks, :], qt_ref[0, h], preferred_element_type=F32)
            if masked:
                kc = lax.broadcasted_iota(jnp.int32, st.shape, 0) // CHUNK
                qc = lax.broadcasted_iota(jnp.int32, st.shape, 1) // CHUNK
                st = jnp.where(kc <= qc, st, -jnp.inf)
            st_sc[h] = st
            cmax_sc[h:h + 1, :] = jnp.max(st, axis=0, keepdims=True)
        for h in range(MLA_HEADS):
            m_prev = m_sc[h:h + 1, :]
            m_new = jnp.maximum(m_prev, cmax_sc[h:h + 1, :])
            p_sc[h] = jnp.exp(st_sc[h] - m_new).astype(BF16)
            alpha_sc[h:h + 1, :] = jnp.exp(m_prev - m_new)
            m_sc[h:h + 1, :] = m_new
        for h in range(MLA_HEADS):
            acc_sc[h] = (alpha_sc[h:h + 1, :] * acc_sc[h]
                         + jnp.dot(vt_ref[0, :, ks], p_sc[h], preferred_element_type=F32))

    def body(j, carry):
        step(j, False)
        return carry

    lax.fori_loop(0, i, body, 0)
    step(i, True)
    parts = []
    for h in range(MLA_HEADS):
        acc = acc_sc[h]
        att = (acc[:F_ONES] / acc[F_ONES:F_ONES + 1]).astype(BF16)
        parts.append(jnp.dot(wuvt_ref[h], att, preferred_element_type=F32))
    mla = jnp.concatenate(parts, axis=0).T
    o_ref[0] = _rms(mla, g_ref[...]).astype(BF16)


def _attn_prompt(qt, kcat, vt, wuvt, g, qb):
    b, _, _, s = qt.shape
    return pl.pallas_call(
        functools.partial(_attn_prompt_kernel, qb=qb),
        grid=(b, s // qb),
        in_specs=[pl.BlockSpec((1, MLA_HEADS, QK_PAD, qb), lambda bi, i: (bi, 0, 0, i)),
                  pl.BlockSpec((1, s, QK_PAD), lambda bi, i: (bi, 0, 0)),
                  pl.BlockSpec((1, V_ROWS, s), lambda bi, i: (bi, 0, 0)),
                  _full(wuvt.shape), _full(g.shape)],
        out_specs=pl.BlockSpec((1, qb, MLA_WIDTH), lambda bi, i: (bi, i, 0)),
        out_shape=jax.ShapeDtypeStruct((b, s, MLA_WIDTH), BF16),
        scratch_shapes=[pltpu.VMEM((MLA_HEADS, qb), F32), pltpu.VMEM((MLA_HEADS, V_ROWS, qb), F32),
                        pltpu.VMEM((MLA_HEADS, qb, qb), F32), pltpu.VMEM((MLA_HEADS, qb, qb), BF16),
                        pltpu.VMEM((MLA_HEADS, qb), F32), pltpu.VMEM((MLA_HEADS, qb), F32)],
        compiler_params=_params("parallel", "arbitrary"),
        name="attn_prompt",
    )(qt, kcat, vt, wuvt, g)


def _attn_sample_kernel(q_ref, pckv_ref, pkr_ref, k_ref, wuv_ref, g_ref, o_ref, *, sd, past):
    rows = MLA_HEADS * sd
    q = q_ref[0].reshape(rows, QK_PAD)
    knew = k_ref[0]
    s_past = (lax.dot_general(q[:, :LANES], pckv_ref[0].astype(BF16), _NT, preferred_element_type=F32)
              + lax.dot_general(q[:, F_ROPE:F_ROPE + MLA_ROPE], pkr_ref[0].astype(BF16), _NT,
                                preferred_element_type=F32))
    s_new = lax.dot_general(q, knew, _NT, preferred_element_type=F32)

    def mask(s, k0):
        qc = (past + lax.broadcasted_iota(jnp.int32, s.shape, 0) % sd) // CHUNK
        kc = (k0 + lax.broadcasted_iota(jnp.int32, s.shape, 1)) // CHUNK
        return jnp.where(kc <= qc, s, -jnp.inf)

    s_past = mask(s_past, 0)
    s_new = mask(s_new, past)
    m = jnp.maximum(jnp.max(s_past, axis=-1, keepdims=True), jnp.max(s_new, axis=-1, keepdims=True))
    p_past = jnp.exp(s_past - m).astype(BF16)
    p_new = jnp.exp(s_new - m).astype(BF16)
    acc = jnp.dot(p_new, knew, preferred_element_type=F32)
    num = acc[:, :LANES] + jnp.dot(p_past, pckv_ref[0].astype(BF16), preferred_element_type=F32)
    den = acc[:, F_ONES:F_ONES + 1] + jnp.sum(p_past.astype(F32), axis=-1, keepdims=True)
    att = (num / den).astype(BF16)
    mla = jnp.zeros((sd, MLA_WIDTH), F32)
    for h in range(MLA_HEADS):
        mla = mla + jnp.dot(att[h * sd:(h + 1) * sd], wuv_ref[h], preferred_element_type=F32)
    o_ref[0] = _rms(mla, g_ref[...]).astype(BF16)


def _attn_sample(q, pckv, pkr, kcat, wuv, g, nb, sd):
    past = pckv.shape[1]
    return pl.pallas_call(
        functools.partial(_attn_sample_kernel, sd=sd, past=past),
        grid=(nb,),
        in_specs=[pl.BlockSpec((1, MLA_HEADS, sd, QK_PAD), lambda b: (0, 0, b, 0)),
                  pl.BlockSpec((1, past, MLA_KVRANK), lambda b: (b, 0, 0)),
                  pl.BlockSpec((1, past, MLA_ROPE), lambda b: (b, 0, 0)),
                  pl.BlockSpec((1, sd, QK_PAD), lambda b: (0, b, 0)),
                  _full(wuv.shape), _full(g.shape)],
        out_specs=pl.BlockSpec((1, sd, MLA_WIDTH), lambda b: (0, b, 0)),
        out_shape=jax.ShapeDtypeStruct((1, nb * sd, MLA_WIDTH), BF16),
        compiler_params=_params("parallel"),
        name="attn_sample",
    )(q, pckv, pkr, kcat, wuv, g)


def _lru_kernel(xr_ref, xg_ref, buf0_ref, h0_ref, cw_ref, cb_ref, wa_ref, ba_ref, wi_ref, bi_ref,
                lam_ref, g_ref, o_ref, hlast_ref, newbuf_ref, xp_sc, h_sc, a_sc, b_sc, hh_sc, *, ts):
    t = pl.program_id(1)
    nt = pl.num_programs(1)
    pad = SUBLANES

    @pl.when(t == 0)
    def _():
        xp_sc[pad - (CONV_WIDTH - 1):pad, :] = buf0_ref[0]
        h_sc[...] = h0_ref[0]

    xr = xr_ref[0]
    xp_sc[pad:pad + ts, :] = xr
    xc = cb_ref[...] + cw_ref[CONV_WIDTH - 1:CONV_WIDTH, :] * xr
    for k in range(CONV_WIDTH - 1):
        xc = xc + cw_ref[k:k + 1, :] * xp_sc[pad - (CONV_WIDTH - 1) + k:pad - (CONV_WIDTH - 1) + k + ts, :]
    tail = xp_sc[ts + pad - (CONV_WIDTH - 1):ts + pad, :]
    xp_sc[pad - (CONV_WIDTH - 1):pad, :] = tail

    xcb = xc.astype(BF16)
    r = jax.nn.sigmoid(jnp.dot(xcb, wa_ref[...], preferred_element_type=F32) + ba_ref[...])
    ig = jax.nn.sigmoid(jnp.dot(xcb, wi_ref[...], preferred_element_type=F32) + bi_ref[...])
    log_a = -LRU_C * r * jax.nn.softplus(-lam_ref[...])
    a = jnp.exp(log_a)
    bt = jnp.sqrt(-_expm1(2.0 * log_a)) * (ig * xc)

    ng = ts // SUBLANES
    a3 = a.reshape(ng, SUBLANES, LRU_WIDTH)
    b3 = bt.reshape(ng, SUBLANES, LRU_WIDTH)
    row = lax.broadcasted_iota(jnp.int32, a3.shape, 1)
    for d in (1, 2, 4):
        valid = row >= d
        a_sh = pltpu.roll(a3, d, axis=1)
        b_sh = pltpu.roll(b3, d, axis=1)
        b3 = jnp.where(valid, a3 * b_sh + b3, b3)
        a3 = jnp.where(valid, a3 * a_sh, a3)
    a_sc[...] = a3.reshape(ts, LRU_WIDTH)
    b_sc[...] = b3.reshape(ts, LRU_WIDTH)

    def group(gi, hprev):
        rs = pl.ds(pl.multiple_of(gi * SUBLANES, SUBLANES), SUBLANES)
        hg = a_sc[rs, :] * hprev + b_sc[rs, :]
        hh_sc[rs, :] = hg
        return hg[SUBLANES - 1:SUBLANES, :]

    hfin = lax.fori_loop(0, ng, group, h_sc[...])
    h_sc[...] = hfin

    lru_out = hh_sc[...] * _gelu(xg_ref[0])
    o_ref[0] = _rms(lru_out, g_ref[...]).astype(BF16)

    @pl.when(t == nt - 1)
    def _():
        hlast_ref[0] = hfin
        newbuf_ref[0] = tail


def _lru(xr, xg, buf0, h0, cw, cb, wa, ba, wi, bi, lam, g, ts):
    b, s, w = xr.shape
    row = pl.BlockSpec((1, ts, w), lambda bi_, t: (bi_, t, 0))
    return pl.pallas_call(
        functools.partial(_lru_kernel, ts=ts),
        grid=(b, s // ts),
        in_specs=[row, row,
                  pl.BlockSpec((1, CONV_WIDTH - 1, w), lambda bi_, t: (bi_, 0, 0)),
                  pl.BlockSpec((1, 1, w), lambda bi_, t: (bi_, 0, 0)),
                  _full(cw.shape), _full(cb.shape), _full(wa.shape), _full(ba.shape),
                  _full(wi.shape), _full(bi.shape), _full(lam.shape), _full(g.shape)],
        out_specs=(row,
                   pl.BlockSpec((1, 1, w), lambda bi_, t: (bi_, 0, 0)),
                   pl.BlockSpec((1, CONV_WIDTH - 1, w), lambda bi_, t: (bi_, 0, 0))),
        out_shape=(jax.ShapeDtypeStruct((b, s, w), BF16),
                   jax.ShapeDtypeStruct((b, 1, w), F32),
                   jax.ShapeDtypeStruct((b, CONV_WIDTH - 1, w), F32)),
        scratch_shapes=[pltpu.VMEM((ts + SUBLANES, w), F32), pltpu.VMEM((1, w), F32),
                        pltpu.VMEM((ts, w), F32), pltpu.VMEM((ts, w), F32), pltpu.VMEM((ts, w), F32)],
        compiler_params=_params("parallel", "arbitrary"),
        name="lru",
    )(xr, xg, buf0, h0, cw, cb, wa, ba, wi, bi, lam, g)


def _top_rows(cur, n):
    rows = []
    for _ in range(n):
        m = jnp.max(cur, axis=0, keepdims=True)
        rows.append(m)
        cur = jnp.where(cur == m, -jnp.inf, cur)
    return rows


def _top_ranked(cur, n):
    rows = []
    rank = jnp.full(cur.shape, float(n), F32)
    for r in range(n):
        m = jnp.max(cur, axis=0, keepdims=True)
        hit = cur == m
        rows.append(m)
        rank = jnp.where(hit, float(r), rank)
        cur = jnp.where(hit, -jnp.inf, cur)
    return rows, rank


def _mix_kernel(x_ref, mla_ref, lru_ref, wo_ref, gffn_ref, wqt_ref, k1_ref, k2_ref,
                x1_ref, xn2_ref, rank2_ref, e2_ref, cnt_ref, e1_ref, s1_sc, s2_sc):
    x1 = (x_ref[...]
          + jnp.dot(mla_ref[...], wo_ref[:MLA_WIDTH, :], preferred_element_type=F32)
          + jnp.dot(lru_ref[...], wo_ref[MLA_WIDTH:, :], preferred_element_type=F32))
    x1_ref[...] = x1
    xn2t = _rms(x1, gffn_ref[...]).T.astype(BF16)
    xn2_ref[...] = xn2t
    qt = jnp.dot(wqt_ref[...], xn2t, preferred_element_type=F32).astype(BF16)
    for h in range(PEER_HEADS):
        base = h * PEER_KEY_DIM
        s1_sc[h] = jnp.dot(k1_ref[h], qt[base:base + PEER_HALF], preferred_element_type=F32)
        s2_sc[h] = jnp.dot(k2_ref[h], qt[base + PEER_HALF:base + PEER_KEY_DIM], preferred_element_type=F32)
    k = PEER_TOPK
    groups = x1.shape[0] // LANES

    def body(it, carry):
        h = it // groups
        ls = pl.ds(pl.multiple_of((it % groups) * LANES, LANES), LANES)
        s1 = s1_sc[h, :, ls]
        s2 = s2_sc[h, :, ls]
        v1, rank1 = _top_ranked(s1, k)
        v2, rank2 = _top_ranked(s2, k)
        v1a = jnp.concatenate(v1, axis=0)
        cand = [v1a + v2[0]]
        cand += [v1a[:SUBLANES] + v2[j] for j in range(1, SUBLANES)]
        cand += [v1[0] + jnp.concatenate(v2[SUBLANES:], axis=0)]
        vs = _top_rows(jnp.concatenate(cand, axis=0), k)
        tau = vs[k - 1]
        sel = [c >= tau for c in cand]
        z = sum(jnp.sum(jnp.where(m, jnp.exp(c - vs[0]), 0.0), axis=0, keepdims=True)
                for m, c in zip(sel, cand))
        ones = [jnp.where(m, 1.0, 0.0) for m in sel]
        low = sum(ones[1:SUBLANES])
        cnt = ones[0] + jnp.concatenate([low, jnp.zeros_like(low)], axis=0)
        tail = jnp.sum(ones[SUBLANES], axis=0, keepdims=True)
        cnt_a = jnp.zeros(s1.shape, F32)
        for i in range(k):
            ci = cnt[i:i + 1] + tail if i == 0 else cnt[i:i + 1]
            cnt_a = jnp.where(rank1 == float(i), ci, cnt_a)
        rank2_ref[h, :, ls] = rank2
        e2_ref[h, :, ls] = jnp.exp(s2 - v2[0]) / z
        cnt_ref[h, :, ls] = cnt_a
        e1_ref[h, :, ls] = jnp.exp(s1 - v1[0])
        return carry

    lax.fori_loop(0, PEER_HEADS * groups, body, 0)


def _mix(x, mla, lru, wo, gffn, wqt, k1, k2, tm):
    t, d = x.shape
    row = lambda w: pl.BlockSpec((tm, w), lambda i: (i, 0))
    sc = pl.BlockSpec((PEER_HEADS, PEER_NKEYS, tm), lambda i: (0, 0, i))
    sc_shape = lambda dt: jax.ShapeDtypeStruct((PEER_HEADS, PEER_NKEYS, t), dt)
    return pl.pallas_call(
        _mix_kernel,
        grid=(t // tm,),
        in_specs=[row(d), row(MLA_WIDTH), row(LRU_WIDTH), _full(wo.shape), _full(gffn.shape),
                  _full(wqt.shape), _full(k1.shape), _full(k2.shape)],
        out_specs=(row(d), pl.BlockSpec((d, tm), lambda i: (0, i)), sc, sc, sc, sc),
        out_shape=(jax.ShapeDtypeStruct((t, d), F32), jax.ShapeDtypeStruct((d, t), BF16),
                   sc_shape(F32), sc_shape(F32), sc_shape(F32), sc_shape(F32)),
        scratch_shapes=[pltpu.VMEM((PEER_HEADS, PEER_NKEYS, tm), F32)] * 2,
        compiler_params=_params("parallel"),
        name="mix",
    )(x, mla, lru, wo, gffn, wqt, k1, k2)


def _peer_kernel(xn2_ref, x1_ref, u_ref, vt_ref, rank2_ref, e2_ref, cnt_ref, e1_ref, gfin_ref,
                 y_ref, ht0_sc, ht1_sc, wt0_sc, wt1_sc, yt_sc, *, eb, tb, ne, final_norm):
    g = pl.program_id(0)
    out_tile = jnp.maximum(g - 2, 0) % ne

    @pl.when(g == 0)
    def _():
        for ref in (ht0_sc, ht1_sc, wt0_sc, wt1_sc):
            ref[...] = jnp.zeros(ref.shape, ref.dtype)

    @pl.when(out_tile == 0)
    def _():
        yt_sc[...] = jnp.zeros(yt_sc.shape, F32)

    na = eb // PEER_NKEYS
    mh_rows = eb // 4
    nt_cols = 2 * LANES
    bq_rows = PEER_NKEYS // 8

    def stages(ht_cur, wt_cur, ht_prev, wt_prev):
        def pre_piece(mh, nt):
            ms = slice(mh * mh_rows, (mh + 1) * mh_rows)
            ns = slice(nt * nt_cols, (nt + 1) * nt_cols)
            res = jnp.dot(u_ref[ms, :], xn2_ref[:, ns], preferred_element_type=F32)

            def store():
                ht_cur[ms, ns] = res
            return store

        def out_piece(mh, nt):
            out_rows = yt_sc.shape[0] // (eb // mh_rows)
            ms = slice(mh * out_rows, (mh + 1) * out_rows)
            ns = slice(nt * nt_cols, (nt + 1) * nt_cols)
            res = yt_sc[ms, ns] + jnp.dot(vt_ref[ms, :], wt_cur[:, ns], preferred_element_type=F32)

            def store():
                yt_sc[ms, ns] = res
            return store

        def mix_block(tl, bq):
            ls = slice(tl * LANES, (tl + 1) * LANES)
            bs = slice(bq * bq_rows, (bq + 1) * bq_rows)
            acc = [jnp.zeros((bq_rows, LANES), F32)] * na
            for h in range(PEER_HEADS):
                r2 = rank2_ref[h, bs, ls]
                e2 = e2_ref[h, bs, ls]
                for al in range(na):
                    hit = r2 < cnt_ref[h, al:al + 1, ls]
                    acc[al] = acc[al] + jnp.where(hit, e1_ref[h, al:al + 1, ls] * e2, 0.0)
            for al in range(na):
                rs = slice(al * PEER_NKEYS + bq * bq_rows, al * PEER_NKEYS + (bq + 1) * bq_rows)
                wt_prev[rs, ls] = (acc[al] * _gelu(ht_prev[rs, ls])).astype(BF16)

        pieces = [(f, mh, nt) for nt in range(tb // nt_cols) for mh in range(eb // mh_rows)
                  for f in (pre_piece, out_piece)]
        blocks = [(tl, bq) for tl in range(tb // LANES) for bq in range(PEER_NKEYS // bq_rows)]
        per = -(-len(blocks) // len(pieces))
        for i, (f, mh, nt) in enumerate(pieces):
            store = f(mh, nt)
            for tl, bq in blocks[i * per:(i + 1) * per]:
                mix_block(tl, bq)
            store()

    @pl.when(g % 2 == 0)
    def _():
        stages(ht0_sc, wt0_sc, ht1_sc, wt1_sc)

    @pl.when(g % 2 == 1)
    def _():
        stages(ht1_sc, wt1_sc, ht0_sc, wt0_sc)

    @pl.when((g >= 2) & (out_tile == ne - 1))
    def _():
        y = x1_ref[...] + yt_sc[...].T
        if final_norm:
            y = _rms(y, gfin_ref[...])
        y_ref[...] = y


def _peer(xn2, x1, u, vt, rank2, e2, cnt, e1, gfin, tb, eb, final_norm):
    t, d = x1.shape
    ne = u.shape[0] // eb
    na = eb // PEER_NKEYS
    steps = (t // tb) * ne
    assert na == SUBLANES and u.shape[0] % eb == 0 and tb % (2 * LANES) == 0 and ne > 1

    def at(lag):
        def pos(g):
            s = jnp.clip(g - lag, 0, steps - 1)
            return s // ne, s % ne
        return pos

    pre, mid, out = at(0), at(1), at(2)
    sc = pl.BlockSpec((PEER_HEADS, PEER_NKEYS, tb), lambda g: (0, 0, mid(g)[0]))
    rows = pl.BlockSpec((PEER_HEADS, na, tb), lambda g: (0, mid(g)[1], mid(g)[0]))
    return pl.pallas_call(
        functools.partial(_peer_kernel, eb=eb, tb=tb, ne=ne, final_norm=final_norm),
        grid=(steps + 2,),
        in_specs=[pl.BlockSpec((d, tb), lambda g: (0, pre(g)[0])),
                  pl.BlockSpec((tb, d), lambda g: (out(g)[0], 0)),
                  pl.BlockSpec((eb, d), lambda g: (pre(g)[1], 0)),
                  pl.BlockSpec((d, eb), lambda g: (0, out(g)[1])),
                  sc, sc, rows, rows,
                  _full(gfin.shape)],
        out_specs=pl.BlockSpec((tb, d), lambda g: (out(g)[0], 0)),
        out_shape=jax.ShapeDtypeStruct((t, d), F32),
        scratch_shapes=[pltpu.VMEM((eb, tb), F32), pltpu.VMEM((eb, tb), F32),
                        pltpu.VMEM((eb, tb), BF16), pltpu.VMEM((eb, tb), BF16), pltpu.VMEM((d, tb), F32)],
        compiler_params=_params("arbitrary"),
        name="peer",
    )(xn2, x1, u, vt, rank2, e2, cnt, e1, gfin)


def _rope_tables(pos):
    half = MLA_ROPE // 2
    inv = ROPE_THETA ** (-jnp.arange(half, dtype=F32) / half)
    ang = pos.astype(F32)[:, None] * inv[None, :]
    cos, sin = jnp.cos(ang), jnp.sin(ang)
    c2 = jnp.concatenate([cos, cos], axis=-1)
    s2 = jnp.concatenate([-sin, sin], axis=-1)
    rows = jnp.concatenate([_place_cols(c2), _place_cols(s2)], axis=-1)
    return rows, jnp.concatenate([c2, s2], axis=-1).T


def _swap_halves(w):
    half = w.shape[-1] // 2
    return jnp.concatenate([w[..., half:], w[..., :half]], axis=-1)


def _place_cols(w):
    pad = [(0, 0)] * (w.ndim - 1) + [(ROPE_OFF, LANES - ROPE_OFF - w.shape[-1])]
    return jnp.pad(w, pad)


def _layer_weights(w_in, w_uq, w_uk, w_uv, lru_wa, lru_wi, w_out, peer_wq, peer_keys1, peer_keys2,
                   peer_u, peer_v):
    o1 = MLA_QRANK
    o2 = o1 + MLA_KVRANK
    o3 = o2 + MLA_ROPE
    kr = w_in[:, o2:o3]
    win = jnp.concatenate([w_in[:, :o2], _place_cols(kr), _place_cols(_swap_halves(kr)),
                           w_in[:, o3:]], axis=1).astype(BF16)
    wint = w_in[:, :o2].T.astype(BF16)
    wq3 = w_uq.reshape(MLA_QRANK, MLA_HEADS, MLA_NOPE + MLA_ROPE)
    nope = wq3[:, :, :MLA_NOPE].reshape(MLA_QRANK, MLA_HEADS * MLA_NOPE)
    rp = wq3[:, :, MLA_NOPE:]
    flat = lambda w: w.reshape(MLA_QRANK, -1)
    wuq = jnp.concatenate([nope, flat(_place_cols(rp)), flat(_place_cols(_swap_halves(rp)))], axis=1).astype(BF16)
    wuqt = jnp.concatenate([nope, flat(rp), flat(_swap_halves(rp))], axis=1).T.astype(BF16)
    eye_h = jnp.eye(MLA_HEADS, dtype=F32)
    wuk = jnp.einsum("rhd,hg->hdgr", w_uk, eye_h).reshape(MLA_HEADS * MLA_NOPE, MLA_HEADS * MLA_KVRANK)
    wuv = jnp.einsum("rhd,hg->hrgd", w_uv, eye_h).reshape(MLA_HEADS, MLA_KVRANK, MLA_WIDTH)
    wuvt = jnp.transpose(w_uv, (1, 2, 0))
    eye_b = jnp.eye(LRU_BLOCKS, dtype=F32)
    bd = lambda w: jnp.einsum("nde,nm->ndme", w, eye_b).reshape(LRU_WIDTH, LRU_WIDTH)
    return dict(win=win, wint=wint, wuq=wuq, wuqt=wuqt, wuk=wuk.astype(BF16), wukt=wuk.T.astype(BF16),
                wuv=wuv.astype(BF16), wuvt=wuvt.astype(BF16),
                wa=bd(lru_wa).astype(BF16), wi=bd(lru_wi).astype(BF16), wo=w_out.astype(BF16),
                wqt=peer_wq.T.astype(BF16), k1=peer_keys1.astype(BF16), k2=peer_keys2.astype(BF16),
                u=peer_u.astype(BF16), vt=peer_v.T.astype(BF16))


def _col(v):
    return v.reshape(-1, 1).astype(F32)


def _row(v):
    return v.reshape(1, -1).astype(F32)


def _token_tile(n, cap):
    t = min(n, cap)
    assert n % t == 0, (n, t)
    return t


def kernel(x_prompt, x_sample, cache_mla_ckv, cache_mla_krope, state_lru_h, state_lru_conv, norm_mix, w_in, norm_q, w_uq, norm_kv, w_uk, w_uv, conv_w, conv_b, lru_wa, lru_ba, lru_wi, lru_bi, lru_lambda, norm_mla_out, norm_lru_out, w_out, norm_ffn, peer_wq, peer_keys1, peer_keys2, peer_u, peer_v, norm_final):
    bp, sp, d = x_prompt.shape
    bs, sd, _ = x_sample.shape
    depth = w_in.shape[0]
    past = cache_mla_ckv.shape[2]
    ts_tok = bs * sd
    tab_p, tabt_p = _rope_tables(jnp.arange(sp))
    tab_s = jnp.tile(_rope_tables(past + jnp.arange(sd))[0], (bs, 1))
    gfin = _row(norm_final)

    xp = x_prompt
    xs = x_sample.reshape(1, ts_tok, d)
    outs = [[] for _ in range(8)]
    for l in range(depth):
        w = _layer_weights(w_in[l], w_uq[l], w_uk[l], w_uv[l], lru_wa[l], lru_wi[l], w_out[l], peer_wq[l],
                           peer_keys1[l], peer_keys2[l], peer_u[l], peer_v[l])
        last = l == depth - 1
        lru_args = (conv_w[l].astype(F32), _row(conv_b[l]), w["wa"], _row(lru_ba[l]), w["wi"], _row(lru_bi[l]),
                    _row(lru_lambda[l]), _row(norm_lru_out[l]))
        proj_common = (_row(norm_mix[l]), w["win"], _row(norm_kv[l]))
        proj_rows = (_row(norm_q[l]), w["wuq"], w["wuk"])
        proj_cols = (tabt_p, w["wint"], _col(norm_q[l]), _col(norm_kv[l]), w["wuqt"], w["wukt"])
        gmla = _row(norm_mla_out[l])

        def tail(x2d, mla, lru):
            t = x2d.shape[0]
            tm = _token_tile(t, 512)
            x1, xn2, rank2, e2, cnt, e1 = _mix(x2d, mla, lru, w["wo"], _row(norm_ffn[l]), w["wqt"],
                                               w["k1"], w["k2"], tm)
            return _peer(xn2, x1, w["u"], w["vt"], rank2, e2, cnt, e1, gfin, tm, 1024, last)

        kcat, ckv, kr, xr, xg, qt, vt = _proj(xp, tab_p, proj_common, proj_cols, _token_tile(sp, 512), True)
        mla = _attn_prompt(qt, kcat, vt, w["wuvt"], gmla, _token_tile(sp, 512))
        lru, hl, nb = _lru(xr, xg, jnp.zeros((bp, CONV_WIDTH - 1, LRU_WIDTH), F32),
                           jnp.zeros((bp, 1, LRU_WIDTH), F32), *lru_args, _token_tile(sp, 512))
        xp = tail(xp.reshape(bp * sp, d), mla.reshape(bp * sp, MLA_WIDTH),
                  lru.reshape(bp * sp, LRU_WIDTH)).reshape(bp, sp, d)
        for lst, v in zip(outs[:4], (ckv, kr, hl[:, 0], nb)):
            lst.append(v)

        kcat, ckv, kr, xr, xg, q = _proj(xs, tab_s, proj_common, proj_rows, _token_tile(ts_tok, 512), False)
        mla = _attn_sample(q, cache_mla_ckv[l], cache_mla_krope[l], kcat, w["wuv"], gmla, bs, sd)
        lru, hl, nb = _lru(xr.reshape(bs, sd, LRU_WIDTH), xg.reshape(bs, sd, LRU_WIDTH),
                           state_lru_conv[l].astype(F32), state_lru_h[l].reshape(bs, 1, LRU_WIDTH).astype(F32),
                           *lru_args, sd)
        xs = tail(xs[0], mla[0], lru.reshape(ts_tok, LRU_WIDTH)).reshape(1, ts_tok, d)
        for lst, v in zip(outs[4:], (ckv.reshape(bs, sd, MLA_KVRANK), kr.reshape(bs, sd, MLA_ROPE), hl[:, 0], nb)):
            lst.append(v)

    return (xp, xs.reshape(bs, sd, d)) + tuple(jnp.stack(o) for o in outs)
```

```python
import functools

import jax
import jax.numpy as jnp
from jax import lax
from jax.experimental import pallas as pl
from jax.experimental.pallas import tpu as pltpu

F32 = jnp.float32
BF16 = jnp.bfloat16

D_MODEL = 1024
CHUNK = 64
EPS = 1e-6
MLA_HEADS = 8
MLA_NOPE = 64
MLA_ROPE = 32
MLA_VDIM = 64
MLA_QRANK = 256
MLA_KVRANK = 128
ROPE_THETA = 10000.0
MLA_WIDTH = MLA_HEADS * MLA_VDIM
MLA_SCALE = (MLA_NOPE + MLA_ROPE) ** -0.5
QK_PAD = 256
LRU_WIDTH = 512
LRU_BLOCKS = 8
LRU_BLOCK_DIM = LRU_WIDTH // LRU_BLOCKS
CONV_WIDTH = 4
LRU_C = 8.0
PEER_HEADS = 8
PEER_NKEYS = 128
PEER_KEY_DIM = 256
PEER_HALF = PEER_KEY_DIM // 2
PEER_TOPK = 16
LANES = 128
SUBLANES = 8
PEER_TILE = 8 * PEER_NKEYS
VMEM_LIMIT = 56 * 1024 * 1024

F_ONES = MLA_KVRANK
F_ROPE = F_ONES + 16
V_ROWS = F_ROPE
ROPE_OFF = F_ROPE - LANES
ZC_CQ, ZC_CKV, ZC_KR, ZC_KRS, ZC_XR, ZC_XG, ZC_END = 0, 256, 384, 512, 640, 1152, 1664
QC_NOPE, QC_ROPE, QC_ROPES, QC_END = 0, 512, 1536, 2560
QR_NOPE, QR_ROPE, QR_ROPES, QR_END = 0, 512, 768, 1024

_NT = (((1,), (1,)), ((), ()))


def _rms(x, g):
    return x * lax.rsqrt(jnp.mean(x * x, axis=-1, keepdims=True) + EPS) * g


def _rms_cols(xt, g):
    return xt * lax.rsqrt(jnp.mean(xt * xt, axis=0, keepdims=True) + EPS) * g


def _expm1(y):
    u = jnp.exp(y)
    um1 = u - 1.0
    return jnp.where(um1 == 0.0, y, jnp.where(um1 == -1.0, -1.0, um1 * y / jnp.log(u)))


def _gelu(x):
    return 0.5 * x * (1.0 + lax.erf(x * (2.0 ** -0.5)))


def _params(*sem):
    return pltpu.CompilerParams(dimension_semantics=sem, vmem_limit_bytes=VMEM_LIMIT)


def _full(shape):
    n = len(shape)
    return pl.BlockSpec(shape, lambda *_: (0,) * n)


def _proj_common(x_ref, tab_ref, gmix_ref, win_ref, gkv_ref, kcat_ref, ckv_ref, kr_ref, xr_ref, xg_ref):
    xn = _rms(x_ref[0], gmix_ref[...]).astype(BF16)
    z = jnp.dot(xn, win_ref[...], preferred_element_type=F32)
    ctab = tab_ref[:, :LANES]
    stab = tab_ref[:, LANES:]
    ckv = _rms(z[:, ZC_CKV:ZC_KR], gkv_ref[...])
    kr = z[:, ZC_KR:ZC_KRS] * ctab + z[:, ZC_KRS:ZC_XR] * stab
    ckv_ref[0] = ckv
    kr_ref[0] = kr[:, ROPE_OFF:ROPE_OFF + MLA_ROPE]
    kcat_ref[0, :, :LANES] = ckv.astype(BF16)
    lane = lax.broadcasted_iota(jnp.int32, kr.shape, 1)
    kcat_ref[0, :, LANES:] = jnp.where(lane < ROPE_OFF, 1.0, kr).astype(BF16)
    xr_ref[0] = z[:, ZC_XR:ZC_XG]
    xg_ref[0] = z[:, ZC_XG:ZC_END]
    return xn, z, ctab, stab


def _proj_rows_kernel(x_ref, tab_ref, gmix_ref, win_ref, gkv_ref, gq_ref, wuq_ref, wuk_ref,
                      kcat_ref, ckv_ref, kr_ref, xr_ref, xg_ref, q_ref):
    _, z, ctab, stab = _proj_common(x_ref, tab_ref, gmix_ref, win_ref, gkv_ref,
                                    kcat_ref, ckv_ref, kr_ref, xr_ref, xg_ref)
    cqn = _rms(z[:, ZC_CQ:ZC_CKV], gq_ref[...]).astype(BF16)
    q = jnp.dot(cqn, wuq_ref[...], preferred_element_type=F32)
    qlat = jnp.dot(q[:, QC_NOPE:QC_ROPE].astype(BF16), wuk_ref[...], preferred_element_type=F32)
    for h in range(MLA_HEADS):
        q_ref[0, h, :, :LANES] = (qlat[:, h * LANES:(h + 1) * LANES] * MLA_SCALE).astype(BF16)
        rp = (q[:, QC_ROPE + h * LANES:QC_ROPE + (h + 1) * LANES] * ctab
              + q[:, QC_ROPES + h * LANES:QC_ROPES + (h + 1) * LANES] * stab)
        q_ref[0, h, :, LANES:] = (rp * MLA_SCALE).astype(BF16)


def _proj_cols_kernel(x_ref, tab_ref, gmix_ref, win_ref, gkv_ref, tabt_ref, wint_ref, gqc_ref, gkvc_ref,
                      wuqt_ref, wukt_ref, kcat_ref, ckv_ref, kr_ref, xr_ref, xg_ref, qt_ref, vt_ref):
    xn, _, _, _ = _proj_common(x_ref, tab_ref, gmix_ref, win_ref, gkv_ref,
                               kcat_ref, ckv_ref, kr_ref, xr_ref, xg_ref)
    tm = xn.shape[0]
    zt = lax.dot_general(wint_ref[...], xn, _NT, preferred_element_type=F32)
    ckvt = _rms_cols(zt[MLA_QRANK:], gkvc_ref[...])
    vt_ref[0, :F_ONES, :] = ckvt.astype(BF16)
    vt_ref[0, F_ONES:, :] = jnp.ones((V_ROWS - F_ONES, tm), BF16)
    cqnt = _rms_cols(zt[:MLA_QRANK], gqc_ref[...]).astype(BF16)
    qt = jnp.dot(wuqt_ref[...], cqnt, preferred_element_type=F32)
    qlatt = jnp.dot(wukt_ref[...], qt[QR_NOPE:QR_ROPE].astype(BF16), preferred_element_type=F32)
    ctabt = tabt_ref[:MLA_ROPE, :]
    stabt = tabt_ref[MLA_ROPE:, :]
    for h in range(MLA_HEADS):
        qt_ref[0, h, :F_ONES, :] = (qlatt[h * MLA_KVRANK:(h + 1) * MLA_KVRANK] * MLA_SCALE).astype(BF16)
        qt_ref[0, h, F_ONES:F_ROPE, :] = jnp.zeros((F_ROPE - F_ONES, tm), BF16)
        rp = (qt[QR_ROPE + h * MLA_ROPE:QR_ROPE + (h + 1) * MLA_ROPE] * ctabt
              + qt[QR_ROPES + h * MLA_ROPE:QR_ROPES + (h + 1) * MLA_ROPE] * stabt)
        qt_ref[0, h, F_ROPE:F_ROPE + MLA_ROPE, :] = (rp * MLA_SCALE).astype(BF16)
        qt_ref[0, h, F_ROPE + MLA_ROPE:, :] = jnp.zeros((QK_PAD - F_ROPE - MLA_ROPE, tm), BF16)


def _proj(x, tab, common, extra, tm, cols):
    b, s, d = x.shape
    row = lambda w: pl.BlockSpec((1, tm, w), lambda bi, i: (bi, i, 0))
    shape = lambda *dims: jax.ShapeDtypeStruct((b,) + dims, F32)
    out_specs = [row(QK_PAD), row(MLA_KVRANK), row(MLA_ROPE), row(LRU_WIDTH), row(LRU_WIDTH)]
    out_shape = [jax.ShapeDtypeStruct((b, s, QK_PAD), BF16), shape(s, MLA_KVRANK), shape(s, MLA_ROPE),
                 shape(s, LRU_WIDTH), shape(s, LRU_WIDTH)]
    in_specs = [row(d), pl.BlockSpec((tm, 2 * LANES), lambda bi, i: (i, 0))] + [_full(a.shape) for a in common]
    if cols:
        tabt, rest = extra[0], extra[1:]
        in_specs += [pl.BlockSpec((2 * MLA_ROPE, tm), lambda bi, i: (0, i))] + [_full(a.shape) for a in rest]
        out_specs += [pl.BlockSpec((1, MLA_HEADS, QK_PAD, tm), lambda bi, i: (bi, 0, 0, i)),
                      pl.BlockSpec((1, V_ROWS, tm), lambda bi, i: (bi, 0, i))]
        out_shape += [jax.ShapeDtypeStruct((b, MLA_HEADS, QK_PAD, s), BF16),
                      jax.ShapeDtypeStruct((b, V_ROWS, s), BF16)]
    else:
        in_specs += [_full(a.shape) for a in extra]
        out_specs += [pl.BlockSpec((1, MLA_HEADS, tm, QK_PAD), lambda bi, i: (bi, 0, i, 0))]
        out_shape += [jax.ShapeDtypeStruct((b, MLA_HEADS, s, QK_PAD), BF16)]
    return pl.pallas_call(
        _proj_cols_kernel if cols else _proj_rows_kernel,
        grid=(b, s // tm),
        in_specs=in_specs,
        out_specs=tuple(out_specs),
        out_shape=tuple(out_shape),
        compiler_params=_params("parallel", "parallel"),
        name="proj_cols" if cols else "proj_rows",
    )(x, tab, *common, *extra)


def _attn_prompt_kernel(qt_ref, k_ref, vt_ref, wuvt_ref, g_ref, o_ref,
                        m_sc, acc_sc, st_sc, cmax_sc, *, qb):
    i = pl.program_id(1)
    m_sc[...] = jnp.full(m_sc.shape, -jnp.inf, F32)
    acc_sc[...] = jnp.zeros(acc_sc.shape, F32)

    def step(j, masked):
        ks = pl.ds(pl.multiple_of(j * qb, qb), qb)
        for h in range(MLA_HEADS):
            st = jnp.dot(k_ref[0, ks, :], qt_ref[0, h], preferred_element_type=F32)
            if masked:
                kc = lax.broadcasted_iota(jnp.int32, st.shape, 0) // CHUNK
                qc = lax.broadcasted_iota(jnp.int32, st.shape, 1) // CHUNK
                st = jnp.where(kc <= qc, st, -jnp.inf)
            st_sc[h] = st
            cmax_sc[h:h + 1, :] = jnp.max(st, axis=0, keepdims=True)
        for h in range(MLA_HEADS):
            m_prev = m_sc[h:h + 1, :]
            m_new = jnp.maximum(m_prev, cmax_sc[h:h + 1, :])
            p = jnp.exp(st_sc[h] - m_new).astype(BF16)
            m_sc[h:h + 1, :] = m_new
            acc_sc[h] = (jnp.exp(m_prev - m_new) * acc_sc[h]
                         + jnp.dot(vt_ref[0, :, ks], p, preferred_element_type=F32))

    def body(j, carry):
        step(j, False)
        return carry

    lax.fori_loop(0, i, body, 0)
    step(i, True)
    parts = []
    for h in range(MLA_HEADS):
        acc = acc_sc[h]
        att = (acc[:F_ONES] / acc[F_ONES:F_ONES + 1]).astype(BF16)
        parts.append(jnp.dot(wuvt_ref[h], att, preferred_element_type=F32))
    mla = jnp.concatenate(parts, axis=0).T
    o_ref[0] = _rms(mla, g_ref[...]).astype(BF16)


def _attn_prompt(qt, kcat, vt, wuvt, g, qb):
    b, _, _, s = qt.shape
    return pl.pallas_call(
        functools.partial(_attn_prompt_kernel, qb=qb),
        grid=(b, s // qb),
        in_specs=[pl.BlockSpec((1, MLA_HEADS, QK_PAD, qb), lambda bi, i: (bi, 0, 0, i)),
                  pl.BlockSpec((1, s, QK_PAD), lambda bi, i: (bi, 0, 0)),
                  pl.BlockSpec((1, V_ROWS, s), lambda bi, i: (bi, 0, 0)),
                  _full(wuvt.shape), _full(g.shape)],
        out_specs=pl.BlockSpec((1, qb, MLA_WIDTH), lambda bi, i: (bi, i, 0)),
        out_shape=jax.ShapeDtypeStruct((b, s, MLA_WIDTH), BF16),
        scratch_shapes=[pltpu.VMEM((MLA_HEADS, qb), F32), pltpu.VMEM((MLA_HEADS, V_ROWS, qb), F32),
                        pltpu.VMEM((MLA_HEADS, qb, qb), F32), pltpu.VMEM((MLA_HEADS, qb), F32)],
        compiler_params=_params("parallel", "arbitrary"),
        name="attn_prompt",
    )(qt, kcat, vt, wuvt, g)


def _attn_sample_kernel(q_ref, pckv_ref, pkr_ref, k_ref, wuv_ref, g_ref, o_ref, *, sd, past):
    rows = MLA_HEADS * sd
    q = q_ref[0].reshape(rows, QK_PAD)
    knew = k_ref[0]
    s_past = (lax.dot_general(q[:, :LANES], pckv_ref[0].astype(BF16), _NT, preferred_element_type=F32)
              + lax.dot_general(q[:, F_ROPE:F_ROPE + MLA_ROPE], pkr_ref[0].astype(BF16), _NT,
                                preferred_element_type=F32))
    s_new = lax.dot_general(q, knew, _NT, preferred_element_type=F32)

    def mask(s, k0):
        qc = (past + lax.broadcasted_iota(jnp.int32, s.shape, 0) % sd) // CHUNK
        kc = (k0 + lax.broadcasted_iota(jnp.int32, s.shape, 1)) // CHUNK
        return jnp.where(kc <= qc, s, -jnp.inf)

    s_past = mask(s_past, 0)
    s_new = mask(s_new, past)
    m = jnp.maximum(jnp.max(s_past, axis=-1, keepdims=True), jnp.max(s_new, axis=-1, keepdims=True))
    p_past = jnp.exp(s_past - m).astype(BF16)
    p_new = jnp.exp(s_new - m).astype(BF16)
    acc = jnp.dot(p_new, knew, preferred_element_type=F32)
    num = acc[:, :LANES] + jnp.dot(p_past, pckv_ref[0].astype(BF16), preferred_element_type=F32)
    den = acc[:, F_ONES:F_ONES + 1] + jnp.sum(p_past.astype(F32), axis=-1, keepdims=True)
    att = (num / den).astype(BF16)
    mla = jnp.zeros((sd, MLA_WIDTH), F32)
    for h in range(MLA_HEADS):
        mla = mla + jnp.dot(att[h * sd:(h + 1) * sd], wuv_ref[h], preferred_element_type=F32)
    o_ref[0] = _rms(mla, g_ref[...]).astype(BF16)


def _attn_sample(q, pckv, pkr, kcat, wuv, g, nb, sd):
    past = pckv.shape[1]
    return pl.pallas_call(
        functools.partial(_attn_sample_kernel, sd=sd, past=past),
        grid=(nb,),
        in_specs=[pl.BlockSpec((1, MLA_HEADS, sd, QK_PAD), lambda b: (0, 0, b, 0)),
                  pl.BlockSpec((1, past, MLA_KVRANK), lambda b: (b, 0, 0)),
                  pl.BlockSpec((1, past, MLA_ROPE), lambda b: (b, 0, 0)),
                  pl.BlockSpec((1, sd, QK_PAD), lambda b: (0, b, 0)),
                  _full(wuv.shape), _full(g.shape)],
        out_specs=pl.BlockSpec((1, sd, MLA_WIDTH), lambda b: (0, b, 0)),
        out_shape=jax.ShapeDtypeStruct((1, nb * sd, MLA_WIDTH), BF16),
        compiler_params=_params("parallel"),
        name="attn_sample",
    )(q, pckv, pkr, kcat, wuv, g)


def _lru_kernel(xr_ref, xg_ref, buf0_ref, h0_ref, cw_ref, cb_ref, wa_ref, ba_ref, wi_ref, bi_ref,
                lam_ref, g_ref, o_ref, hlast_ref, newbuf_ref, xp_sc, h_sc, a_sc, b_sc, hh_sc, *, ts):
    t = pl.program_id(1)
    nt = pl.num_programs(1)
    pad = SUBLANES

    @pl.when(t == 0)
    def _():
        xp_sc[pad - (CONV_WIDTH - 1):pad, :] = buf0_ref[0]
        h_sc[...] = h0_ref[0]

    xr = xr_ref[0]
    xp_sc[pad:pad + ts, :] = xr
    xc = cb_ref[...] + cw_ref[CONV_WIDTH - 1:CONV_WIDTH, :] * xr
    for k in range(CONV_WIDTH - 1):
        xc = xc + cw_ref[k:k + 1, :] * xp_sc[pad - (CONV_WIDTH - 1) + k:pad - (CONV_WIDTH - 1) + k + ts, :]
    tail = xp_sc[ts + pad - (CONV_WIDTH - 1):ts + pad, :]
    xp_sc[pad - (CONV_WIDTH - 1):pad, :] = tail

    xcb = xc.astype(BF16)
    r = jax.nn.sigmoid(jnp.dot(xcb, wa_ref[...], preferred_element_type=F32) + ba_ref[...])
    ig = jax.nn.sigmoid(jnp.dot(xcb, wi_ref[...], preferred_element_type=F32) + bi_ref[...])
    log_a = -LRU_C * r * jax.nn.softplus(-lam_ref[...])
    a = jnp.exp(log_a)
    bt = jnp.sqrt(-_expm1(2.0 * log_a)) * (ig * xc)

    ng = ts // SUBLANES
    a3 = a.reshape(ng, SUBLANES, LRU_WIDTH)
    b3 = bt.reshape(ng, SUBLANES, LRU_WIDTH)
    row = lax.broadcasted_iota(jnp.int32, a3.shape, 1)
    for d in (1, 2, 4):
        valid = row >= d
        a_sh = pltpu.roll(a3, d, axis=1)
        b_sh = pltpu.roll(b3, d, axis=1)
        b3 = jnp.where(valid, a3 * b_sh + b3, b3)
        a3 = jnp.where(valid, a3 * a_sh, a3)
    a_sc[...] = a3.reshape(ts, LRU_WIDTH)
    b_sc[...] = b3.reshape(ts, LRU_WIDTH)

    def group(gi, hprev):
        rs = pl.ds(pl.multiple_of(gi * SUBLANES, SUBLANES), SUBLANES)
        hg = a_sc[rs, :] * hprev + b_sc[rs, :]
        hh_sc[rs, :] = hg
        return hg[SUBLANES - 1:SUBLANES, :]

    hfin = lax.fori_loop(0, ng, group, h_sc[...])
    h_sc[...] = hfin

    lru_out = hh_sc[...] * _gelu(xg_ref[0])
    o_ref[0] = _rms(lru_out, g_ref[...]).astype(BF16)

    @pl.when(t == nt - 1)
    def _():
        hlast_ref[0] = hfin
        newbuf_ref[0] = tail


def _lru(xr, xg, buf0, h0, cw, cb, wa, ba, wi, bi, lam, g, ts):
    b, s, w = xr.shape
    row = pl.BlockSpec((1, ts, w), lambda bi_, t: (bi_, t, 0))
    return pl.pallas_call(
        functools.partial(_lru_kernel, ts=ts),
        grid=(b, s // ts),
        in_specs=[row, row,
                  pl.BlockSpec((1, CONV_WIDTH - 1, w), lambda bi_, t: (bi_, 0, 0)),
                  pl.BlockSpec((1, 1, w), lambda bi_, t: (bi_, 0, 0)),
                  _full(cw.shape), _full(cb.shape), _full(wa.shape), _full(ba.shape),
                  _full(wi.shape), _full(bi.shape), _full(lam.shape), _full(g.shape)],
        out_specs=(row,
                   pl.BlockSpec((1, 1, w), lambda bi_, t: (bi_, 0, 0)),
                   pl.BlockSpec((1, CONV_WIDTH - 1, w), lambda bi_, t: (bi_, 0, 0))),
        out_shape=(jax.ShapeDtypeStruct((b, s, w), BF16),
                   jax.ShapeDtypeStruct((b, 1, w), F32),
                   jax.ShapeDtypeStruct((b, CONV_WIDTH - 1, w), F32)),
        scratch_shapes=[pltpu.VMEM((ts + SUBLANES, w), F32), pltpu.VMEM((1, w), F32),
                        pltpu.VMEM((ts, w), F32), pltpu.VMEM((ts, w), F32), pltpu.VMEM((ts, w), F32)],
        compiler_params=_params("parallel", "arbitrary"),
        name="lru",
    )(xr, xg, buf0, h0, cw, cb, wa, ba, wi, bi, lam, g)


def _top_rows(cur, n):
    rows = []
    for _ in range(n):
        m = jnp.max(cur, axis=0, keepdims=True)
        rows.append(m)
        cur = jnp.where(cur == m, -jnp.inf, cur)
    return rows


def _top_ranked(cur, n):
    rows = []
    rank = jnp.full(cur.shape, float(n), F32)
    for r in range(n):
        m = jnp.max(cur, axis=0, keepdims=True)
        hit = cur == m
        rows.append(m)
        rank = jnp.where(hit, float(r), rank)
        cur = jnp.where(hit, -jnp.inf, cur)
    return rows, rank


def _mix_kernel(x_ref, mla_ref, lru_ref, wo_ref, gffn_ref, wqt_ref, k1_ref, k2_ref,
                x1_ref, xn2_ref, rank2_ref, e2_ref, cnt_ref, e1_ref, s1_sc, s2_sc):
    x1 = (x_ref[...]
          + jnp.dot(mla_ref[...], wo_ref[:MLA_WIDTH, :], preferred_element_type=F32)
          + jnp.dot(lru_ref[...], wo_ref[MLA_WIDTH:, :], preferred_element_type=F32))
    x1_ref[...] = x1
    xn2t = _rms(x1, gffn_ref[...]).T.astype(BF16)
    xn2_ref[...] = xn2t
    qt = jnp.dot(wqt_ref[...], xn2t, preferred_element_type=F32).astype(BF16)
    for h in range(PEER_HEADS):
        base = h * PEER_KEY_DIM
        s1_sc[h] = jnp.dot(k1_ref[h], qt[base:base + PEER_HALF], preferred_element_type=F32)
        s2_sc[h] = jnp.dot(k2_ref[h], qt[base + PEER_HALF:base + PEER_KEY_DIM], preferred_element_type=F32)
    k = PEER_TOPK
    groups = x1.shape[0] // LANES

    def body(it, carry):
        h = it // groups
        ls = pl.ds(pl.multiple_of((it % groups) * LANES, LANES), LANES)
        s1 = s1_sc[h, :, ls]
        s2 = s2_sc[h, :, ls]
        v1, rank1 = _top_ranked(s1, k)
        v2, rank2 = _top_ranked(s2, k)
        v1a = jnp.concatenate(v1, axis=0)
        cand = [v1a + v2[0]]
        cand += [v1a[:SUBLANES] + v2[j] for j in range(1, SUBLANES)]
        cand += [v1[0] + jnp.concatenate(v2[SUBLANES:], axis=0)]
        vs = _top_rows(jnp.concatenate(cand, axis=0), k)
        tau = vs[k - 1]
        sel = [c >= tau for c in cand]
        z = sum(jnp.sum(jnp.where(m, jnp.exp(c - vs[0]), 0.0), axis=0, keepdims=True)
                for m, c in zip(sel, cand))
        ones = [jnp.where(m, 1.0, 0.0) for m in sel]
        low = sum(ones[1:SUBLANES])
        cnt = ones[0] + jnp.concatenate([low, jnp.zeros_like(low)], axis=0)
        tail = jnp.sum(ones[SUBLANES], axis=0, keepdims=True)
        cnt_a = jnp.zeros(s1.shape, F32)
        for i in range(k):
            ci = cnt[i:i + 1] + tail if i == 0 else cnt[i:i + 1]
            cnt_a = jnp.where(rank1 == float(i), ci, cnt_a)
        rank2_ref[h, :, ls] = rank2
        e2_ref[h, :, ls] = jnp.exp(s2 - v2[0]) / z
        cnt_ref[h, :, ls] = cnt_a
        e1_ref[h, :, ls] = jnp.exp(s1 - v1[0])
        return carry

    lax.fori_loop(0, PEER_HEADS * groups, body, 0)


def _mix(x, mla, lru, wo, gffn, wqt, k1, k2, tm):
    t, d = x.shape
    row = lambda w: pl.BlockSpec((tm, w), lambda i: (i, 0))
    sc = pl.BlockSpec((PEER_HEADS, PEER_NKEYS, tm), lambda i: (0, 0, i))
    sc_shape = lambda dt: jax.ShapeDtypeStruct((PEER_HEADS, PEER_NKEYS, t), dt)
    return pl.pallas_call(
        _mix_kernel,
        grid=(t // tm,),
        in_specs=[row(d), row(MLA_WIDTH), row(LRU_WIDTH), _full(wo.shape), _full(gffn.shape),
                  _full(wqt.shape), _full(k1.shape), _full(k2.shape)],
        out_specs=(row(d), pl.BlockSpec((d, tm), lambda i: (0, i)), sc, sc, sc, sc),
        out_shape=(jax.ShapeDtypeStruct((t, d), F32), jax.ShapeDtypeStruct((d, t), BF16),
                   sc_shape(F32), sc_shape(F32), sc_shape(F32), sc_shape(F32)),
        scratch_shapes=[pltpu.VMEM((PEER_HEADS, PEER_NKEYS, tm), F32)] * 2,
        compiler_params=_params("parallel"),
        name="mix",
    )(x, mla, lru, wo, gffn, wqt, k1, k2)


def _peer_kernel(xn2_ref, x1_ref, u_ref, vt_ref, rank2_ref, e2_ref, cnt_ref, e1_ref, gfin_ref,
                 y_ref, ht0_sc, ht1_sc, wt0_sc, wt1_sc, yt_sc, *, eb, tb, ne, final_norm):
    g = pl.program_id(0)
    out_tile = jnp.maximum(g - 2, 0) % ne

    @pl.when(g == 0)
    def _():
        for ref in (ht0_sc, ht1_sc, wt0_sc, wt1_sc):
            ref[...] = jnp.zeros(ref.shape, ref.dtype)

    @pl.when(out_tile == 0)
    def _():
        yt_sc[...] = jnp.zeros(yt_sc.shape, F32)

    na = eb // PEER_NKEYS
    mh_rows = eb // 4
    nt_cols = 2 * LANES
    bq_rows = PEER_NKEYS // 8

    def stages(ht_cur, wt_cur, ht_prev, wt_prev):
        def pre_piece(mh, nt):
            ms = slice(mh * mh_rows, (mh + 1) * mh_rows)
            ns = slice(nt * nt_cols, (nt + 1) * nt_cols)
            res = jnp.dot(u_ref[ms, :], xn2_ref[:, ns], preferred_element_type=F32)

            def store():
                ht_cur[ms, ns] = res
            return store

        def out_piece(mh, nt):
            out_rows = yt_sc.shape[0] // (eb // mh_rows)
            ms = slice(mh * out_rows, (mh + 1) * out_rows)
            ns = slice(nt * nt_cols, (nt + 1) * nt_cols)
            res = yt_sc[ms, ns] + jnp.dot(vt_ref[ms, :], wt_cur[:, ns], preferred_element_type=F32)

            def store():
                yt_sc[ms, ns] = res
            return store

        def mix_block(tl, bq):
            ls = slice(tl * LANES, (tl + 1) * LANES)
            bs = slice(bq * bq_rows, (bq + 1) * bq_rows)
            acc = [jnp.zeros((bq_rows, LANES), F32)] * na
            for h in range(PEER_HEADS):
                r2 = rank2_ref[h, bs, ls]
                e2 = e2_ref[h, bs, ls]
                for al in range(na):
                    hit = r2 < cnt_ref[h, al:al + 1, ls]
                    acc[al] = acc[al] + jnp.where(hit, e1_ref[h, al:al + 1, ls] * e2, 0.0)
            for al in range(na):
                rs = slice(al * PEER_NKEYS + bq * bq_rows, al * PEER_NKEYS + (bq + 1) * bq_rows)
                wt_prev[rs, ls] = (acc[al] * _gelu(ht_prev[rs, ls])).astype(BF16)

        pieces = [(f, mh, nt) for nt in range(tb // nt_cols) for mh in range(eb // mh_rows)
                  for f in (pre_piece, out_piece)]
        blocks = [(tl, bq) for tl in range(tb // LANES) for bq in range(PEER_NKEYS // bq_rows)]
        per = -(-len(blocks) // len(pieces))
        for i, (f, mh, nt) in enumerate(pieces):
            store = f(mh, nt)
            for tl, bq in blocks[i * per:(i + 1) * per]:
                mix_block(tl, bq)
            store()

    @pl.when(g % 2 == 0)
    def _():
        stages(ht0_sc, wt0_sc, ht1_sc, wt1_sc)

    @pl.when(g % 2 == 1)
    def _():
        stages(ht1_sc, wt1_sc, ht0_sc, wt0_sc)

    @pl.when((g >= 2) & (out_tile == ne - 1))
    def _():
        y = x1_ref[...] + yt_sc[...].T
        if final_norm:
            y = _rms(y, gfin_ref[...])
        y_ref[...] = y


def _peer(xn2, x1, u, vt, rank2, e2, cnt, e1, gfin, tb, eb, final_norm):
    t, d = x1.shape
    ne = u.shape[0] // eb
    na = eb // PEER_NKEYS
    steps = (t // tb) * ne
    assert na == SUBLANES and u.shape[0] % eb == 0 and tb % (2 * LANES) == 0 and ne > 1

    def at(lag):
        def pos(g):
            s = jnp.clip(g - lag, 0, steps - 1)
            return s // ne, s % ne
        return pos

    pre, mid, out = at(0), at(1), at(2)
    sc = pl.BlockSpec((PEER_HEADS, PEER_NKEYS, tb), lambda g: (0, 0, mid(g)[0]))
    rows = pl.BlockSpec((PEER_HEADS, na, tb), lambda g: (0, mid(g)[1], mid(g)[0]))
    return pl.pallas_call(
        functools.partial(_peer_kernel, eb=eb, tb=tb, ne=ne, final_norm=final_norm),
        grid=(steps + 2,),
        in_specs=[pl.BlockSpec((d, tb), lambda g: (0, pre(g)[0])),
                  pl.BlockSpec((tb, d), lambda g: (out(g)[0], 0)),
                  pl.BlockSpec((eb, d), lambda g: (pre(g)[1], 0)),
                  pl.BlockSpec((d, eb), lambda g: (0, out(g)[1])),
                  sc, sc, rows, rows,
                  _full(gfin.shape)],
        out_specs=pl.BlockSpec((tb, d), lambda g: (out(g)[0], 0)),
        out_shape=jax.ShapeDtypeStruct((t, d), F32),
        scratch_shapes=[pltpu.VMEM((eb, tb), F32), pltpu.VMEM((eb, tb), F32),
                        pltpu.VMEM((eb, tb), BF16), pltpu.VMEM((eb, tb), BF16), pltpu.VMEM((d, tb), F32)],
        compiler_params=_params("arbitrary"),
        name="peer",
    )(xn2, x1, u, vt, rank2, e2, cnt, e1, gfin)


def _rope_tables(pos):
    half = MLA_ROPE // 2
    inv = ROPE_THETA ** (-jnp.arange(half, dtype=F32) / half)
    ang = pos.astype(F32)[:, None] * inv[None, :]
    cos, sin = jnp.cos(ang), jnp.sin(ang)
    c2 = jnp.concatenate([cos, cos], axis=-1)
    s2 = jnp.concatenate([-sin, sin], axis=-1)
    rows = jnp.concatenate([_place_cols(c2), _place_cols(s2)], axis=-1)
    return rows, jnp.concatenate([c2, s2], axis=-1).T


def _swap_halves(w):
    half = w.shape[-1] // 2
    return jnp.concatenate([w[..., half:], w[..., :half]], axis=-1)


def _place_cols(w):
    pad = [(0, 0)] * (w.ndim - 1) + [(ROPE_OFF, LANES - ROPE_OFF - w.shape[-1])]
    return jnp.pad(w, pad)


def _layer_weights(w_in, w_uq, w_uk, w_uv, lru_wa, lru_wi, w_out, peer_wq, peer_keys1, peer_keys2,
                   peer_u, peer_v):
    o1 = MLA_QRANK
    o2 = o1 + MLA_KVRANK
    o3 = o2 + MLA_ROPE
    kr = w_in[:, o2:o3]
    win = jnp.concatenate([w_in[:, :o2], _place_cols(kr), _place_cols(_swap_halves(kr)),
                           w_in[:, o3:]], axis=1).astype(BF16)
    wint = w_in[:, :o2].T.astype(BF16)
    wq3 = w_uq.reshape(MLA_QRANK, MLA_HEADS, MLA_NOPE + MLA_ROPE)
    nope = wq3[:, :, :MLA_NOPE].reshape(MLA_QRANK, MLA_HEADS * MLA_NOPE)
    rp = wq3[:, :, MLA_NOPE:]
    flat = lambda w: w.reshape(MLA_QRANK, -1)
    wuq = jnp.concatenate([nope, flat(_place_cols(rp)), flat(_place_cols(_swap_halves(rp)))], axis=1).astype(BF16)
    wuqt = jnp.concatenate([nope, flat(rp), flat(_swap_halves(rp))], axis=1).T.astype(BF16)
    eye_h = jnp.eye(MLA_HEADS, dtype=F32)
    wuk = jnp.einsum("rhd,hg->hdgr", w_uk, eye_h).reshape(MLA_HEADS * MLA_NOPE, MLA_HEADS * MLA_KVRANK)
    wuv = jnp.einsum("rhd,hg->hrgd", w_uv, eye_h).reshape(MLA_HEADS, MLA_KVRANK, MLA_WIDTH)
    wuvt = jnp.transpose(w_uv, (1, 2, 0))
    eye_b = jnp.eye(LRU_BLOCKS, dtype=F32)
    bd = lambda w: jnp.einsum("nde,nm->ndme", w, eye_b).reshape(LRU_WIDTH, LRU_WIDTH)
    return dict(win=win, wint=wint, wuq=wuq, wuqt=wuqt, wuk=wuk.astype(BF16), wukt=wuk.T.astype(BF16),
                wuv=wuv.astype(BF16), wuvt=wuvt.astype(BF16),
                wa=bd(lru_wa).astype(BF16), wi=bd(lru_wi).astype(BF16), wo=w_out.astype(BF16),
                wqt=peer_wq.T.astype(BF16), k1=peer_keys1.astype(BF16), k2=peer_keys2.astype(BF16),
                u=peer_u.astype(BF16), vt=peer_v.T.astype(BF16))


def _col(v):
    return v.reshape(-1, 1).astype(F32)


def _row(v):
    return v.reshape(1, -1).astype(F32)


def _token_tile(n, cap):
    t = min(n, cap)
    assert n % t == 0, (n, t)
    return t


def kernel(x_prompt, x_sample, cache_mla_ckv, cache_mla_krope, state_lru_h, state_lru_conv, norm_mix, w_in, norm_q, w_uq, norm_kv, w_uk, w_uv, conv_w, conv_b, lru_wa, lru_ba, lru_wi, lru_bi, lru_lambda, norm_mla_out, norm_lru_out, w_out, norm_ffn, peer_wq, peer_keys1, peer_keys2, peer_u, peer_v, norm_final):
    bp, sp, d = x_prompt.shape
    bs, sd, _ = x_sample.shape
    depth = w_in.shape[0]
    past = cache_mla_ckv.shape[2]
    ts_tok = bs * sd
    tab_p, tabt_p = _rope_tables(jnp.arange(sp))
    tab_s = jnp.tile(_rope_tables(past + jnp.arange(sd))[0], (bs, 1))
    gfin = _row(norm_final)

    xp = x_prompt
    xs = x_sample.reshape(1, ts_tok, d)
    outs = [[] for _ in range(8)]
    for l in range(depth):
        w = _layer_weights(w_in[l], w_uq[l], w_uk[l], w_uv[l], lru_wa[l], lru_wi[l], w_out[l], peer_wq[l],
                           peer_keys1[l], peer_keys2[l], peer_u[l], peer_v[l])
        last = l == depth - 1
        lru_args = (conv_w[l].astype(F32), _row(conv_b[l]), w["wa"], _row(lru_ba[l]), w["wi"], _row(lru_bi[l]),
                    _row(lru_lambda[l]), _row(norm_lru_out[l]))
        proj_common = (_row(norm_mix[l]), w["win"], _row(norm_kv[l]))
        proj_rows = (_row(norm_q[l]), w["wuq"], w["wuk"])
        proj_cols = (tabt_p, w["wint"], _col(norm_q[l]), _col(norm_kv[l]), w["wuqt"], w["wukt"])
        gmla = _row(norm_mla_out[l])

        def tail(x2d, mla, lru):
            t = x2d.shape[0]
            tm = _token_tile(t, 512)
            x1, xn2, rank2, e2, cnt, e1 = _mix(x2d, mla, lru, w["wo"], _row(norm_ffn[l]), w["wqt"],
                                               w["k1"], w["k2"], tm)
            return _peer(xn2, x1, w["u"], w["vt"], rank2, e2, cnt, e1, gfin, tm, PEER_TILE, last)

        kcat, ckv, kr, xr, xg, qt, vt = _proj(xp, tab_p, proj_common, proj_cols, _token_tile(sp, 512), True)
        mla = _attn_prompt(qt, kcat, vt, w["wuvt"], gmla, _token_tile(sp, 512))
        lru, hl, nb = _lru(xr, xg, jnp.zeros((bp, CONV_WIDTH - 1, LRU_WIDTH), F32),
                           jnp.zeros((bp, 1, LRU_WIDTH), F32), *lru_args, _token_tile(sp, 512))
        xp = tail(xp.reshape(bp * sp, d), mla.reshape(bp * sp, MLA_WIDTH),
                  lru.reshape(bp * sp, LRU_WIDTH)).reshape(bp, sp, d)
        for lst, v in zip(outs[:4], (ckv, kr, hl[:, 0], nb)):
            lst.append(v)

        kcat, ckv, kr, xr, xg, q = _proj(xs, tab_s, proj_common, proj_rows, _token_tile(ts_tok, 512), False)
        mla = _attn_sample(q, cache_mla_ckv[l], cache_mla_krope[l], kcat, w["wuv"], gmla, bs, sd)
        lru, hl, nb = _lru(xr.reshape(bs, sd, LRU_WIDTH), xg.reshape(bs, sd, LRU_WIDTH),
                           state_lru_conv[l].astype(F32), state_lru_h[l].reshape(bs, 1, LRU_WIDTH).astype(F32),
                           *lru_args, sd)
        xs = tail(xs[0], mla[0], lru.reshape(ts_tok, LRU_WIDTH)).reshape(1, ts_tok, d)
        for lst, v in zip(outs[4:], (ckv.reshape(bs, sd, MLA_KVRANK), kr.reshape(bs, sd, MLA_ROPE), hl[:, 0], nb)):
            lst.append(v)

    return (xp, xs.reshape(bs, sd, d)) + tuple(jnp.stack(o) for o in outs)
```

```python
import functools

import jax
import jax.numpy as jnp
from jax import lax
from jax.experimental import pallas as pl
from jax.experimental.pallas import tpu as pltpu

F32 = jnp.float32
BF16 = jnp.bfloat16

D_MODEL = 1024
CHUNK = 64
EPS = 1e-6
MLA_HEADS = 8
MLA_NOPE = 64
MLA_ROPE = 32
MLA_VDIM = 64
MLA_QRANK = 256
MLA_KVRANK = 128
ROPE_THETA = 10000.0
MLA_WIDTH = MLA_HEADS * MLA_VDIM
MLA_SCALE = (MLA_NOPE + MLA_ROPE) ** -0.5
QK_PAD = 256
LRU_WIDTH = 512
LRU_BLOCKS = 8
LRU_BLOCK_DIM = LRU_WIDTH // LRU_BLOCKS
CONV_WIDTH = 4
LRU_C = 8.0
PEER_HEADS = 8
PEER_NKEYS = 128
PEER_KEY_DIM = 256
PEER_HALF = PEER_KEY_DIM // 2
PEER_TOPK = 16
LANES = 128
SUBLANES = 8
PEER_TILE = 8 * PEER_NKEYS
VMEM_LIMIT = 56 * 1024 * 1024

F_ONES = MLA_KVRANK
F_ROPE = F_ONES + 16
V_ROWS = F_ROPE
ROPE_OFF = F_ROPE - LANES
ZC_CQ, ZC_CKV, ZC_KR, ZC_KRS, ZC_XR, ZC_XG, ZC_END = 0, 256, 384, 512, 640, 1152, 1664
QC_NOPE, QC_ROPE, QC_ROPES, QC_END = 0, 512, 1536, 2560
QR_NOPE, QR_ROPE, QR_ROPES, QR_END = 0, 512, 768, 1024

_NT = (((1,), (1,)), ((), ()))


def _rms(x, g):
    return x * lax.rsqrt(jnp.mean(x * x, axis=-1, keepdims=True) + EPS) * g


def _rms_cols(xt, g):
    return xt * lax.rsqrt(jnp.mean(xt * xt, axis=0, keepdims=True) + EPS) * g


def _expm1(y):
    u = jnp.exp(y)
    um1 = u - 1.0
    return jnp.where(um1 == 0.0, y, jnp.where(um1 == -1.0, -1.0, um1 * y / jnp.log(u)))


def _gelu(x):
    return 0.5 * x * (1.0 + lax.erf(x * (2.0 ** -0.5)))


def _params(*sem):
    return pltpu.CompilerParams(dimension_semantics=sem, vmem_limit_bytes=VMEM_LIMIT)


def _full(shape):
    n = len(shape)
    return pl.BlockSpec(shape, lambda *_: (0,) * n)


def _proj_common(x_ref, tab_ref, gmix_ref, win_ref, gkv_ref, kcat_ref, ckv_ref, kr_ref, xr_ref, xg_ref):
    xn = _rms(x_ref[0], gmix_ref[...]).astype(BF16)
    z = jnp.dot(xn, win_ref[...], preferred_element_type=F32)
    ctab = tab_ref[:, :LANES]
    stab = tab_ref[:, LANES:]
    ckv = _rms(z[:, ZC_CKV:ZC_KR], gkv_ref[...])
    kr = z[:, ZC_KR:ZC_KRS] * ctab + z[:, ZC_KRS:ZC_XR] * stab
    ckv_ref[0] = ckv
    kr_ref[0] = kr[:, ROPE_OFF:ROPE_OFF + MLA_ROPE]
    kcat_ref[0, :, :LANES] = ckv.astype(BF16)
    lane = lax.broadcasted_iota(jnp.int32, kr.shape, 1)
    kcat_ref[0, :, LANES:] = jnp.where(lane < ROPE_OFF, 1.0, kr).astype(BF16)
    xr_ref[0] = z[:, ZC_XR:ZC_XG]
    xg_ref[0] = z[:, ZC_XG:ZC_END]
    return xn, z, ctab, stab


def _proj_rows_kernel(x_ref, tab_ref, gmix_ref, win_ref, gkv_ref, gq_ref, wuq_ref, wuk_ref,
                      kcat_ref, ckv_ref, kr_ref, xr_ref, xg_ref, q_ref):
    _, z, ctab, stab = _proj_common(x_ref, tab_ref, gmix_ref, win_ref, gkv_ref,
                                    kcat_ref, ckv_ref, kr_ref, xr_ref, xg_ref)
    cqn = _rms(z[:, ZC_CQ:ZC_CKV], gq_ref[...]).astype(BF16)
    q = jnp.dot(cqn, wuq_ref[...], preferred_element_type=F32)
    qlat = jnp.dot(q[:, QC_NOPE:QC_ROPE].astype(BF16), wuk_ref[...], preferred_element_type=F32)
    for h in range(MLA_HEADS):
        q_ref[0, h, :, :LANES] = (qlat[:, h * LANES:(h + 1) * LANES] * MLA_SCALE).astype(BF16)
        rp = (q[:, QC_ROPE + h * LANES:QC_ROPE + (h + 1) * LANES] * ctab
              + q[:, QC_ROPES + h * LANES:QC_ROPES + (h + 1) * LANES] * stab)
        q_ref[0, h, :, LANES:] = (rp * MLA_SCALE).astype(BF16)


def _proj_cols_kernel(x_ref, tab_ref, gmix_ref, win_ref, gkv_ref, tabt_ref, wint_ref, gqc_ref, gkvc_ref,
                      wuqt_ref, wukt_ref, kcat_ref, ckv_ref, kr_ref, xr_ref, xg_ref, qt_ref, vt_ref):
    xn, _, _, _ = _proj_common(x_ref, tab_ref, gmix_ref, win_ref, gkv_ref,
                               kcat_ref, ckv_ref, kr_ref, xr_ref, xg_ref)
    tm = xn.shape[0]
    zt = lax.dot_general(wint_ref[...], xn, _NT, preferred_element_type=F32)
    ckvt = _rms_cols(zt[MLA_QRANK:], gkvc_ref[...])
    vt_ref[0, :F_ONES, :] = ckvt.astype(BF16)
    vt_ref[0, F_ONES:, :] = jnp.ones((V_ROWS - F_ONES, tm), BF16)
    cqnt = _rms_cols(zt[:MLA_QRANK], gqc_ref[...]).astype(BF16)
    qt = jnp.dot(wuqt_ref[...], cqnt, preferred_element_type=F32)
    qlatt = jnp.dot(wukt_ref[...], qt[QR_NOPE:QR_ROPE].astype(BF16), preferred_element_type=F32)
    ctabt = tabt_ref[:MLA_ROPE, :]
    stabt = tabt_ref[MLA_ROPE:, :]
    for h in range(MLA_HEADS):
        qt_ref[0, h, :F_ONES, :] = (qlatt[h * MLA_KVRANK:(h + 1) * MLA_KVRANK] * MLA_SCALE).astype(BF16)
        qt_ref[0, h, F_ONES:F_ROPE, :] = jnp.zeros((F_ROPE - F_ONES, tm), BF16)
        rp = (qt[QR_ROPE + h * MLA_ROPE:QR_ROPE + (h + 1) * MLA_ROPE] * ctabt
              + qt[QR_ROPES + h * MLA_ROPE:QR_ROPES + (h + 1) * MLA_ROPE] * stabt)
        qt_ref[0, h, F_ROPE:F_ROPE + MLA_ROPE, :] = (rp * MLA_SCALE).astype(BF16)
        qt_ref[0, h, F_ROPE + MLA_ROPE:, :] = jnp.zeros((QK_PAD - F_ROPE - MLA_ROPE, tm), BF16)


def _proj(x, tab, common, extra, tm, cols):
    b, s, d = x.shape
    row = lambda w: pl.BlockSpec((1, tm, w), lambda bi, i: (bi, i, 0))
    shape = lambda *dims: jax.ShapeDtypeStruct((b,) + dims, F32)
    out_specs = [row(QK_PAD), row(MLA_KVRANK), row(MLA_ROPE), row(LRU_WIDTH), row(LRU_WIDTH)]
    out_shape = [jax.ShapeDtypeStruct((b, s, QK_PAD), BF16), shape(s, MLA_KVRANK), shape(s, MLA_ROPE),
                 shape(s, LRU_WIDTH), shape(s, LRU_WIDTH)]
    in_specs = [row(d), pl.BlockSpec((tm, 2 * LANES), lambda bi, i: (i, 0))] + [_full(a.shape) for a in common]
    if cols:
        tabt, rest = extra[0], extra[1:]
        in_specs += [pl.BlockSpec((2 * MLA_ROPE, tm), lambda bi, i: (0, i))] + [_full(a.shape) for a in rest]
        out_specs += [pl.BlockSpec((1, MLA_HEADS, QK_PAD, tm), lambda bi, i: (bi, 0, 0, i)),
                      pl.BlockSpec((1, V_ROWS, tm), lambda bi, i: (bi, 0, i))]
        out_shape += [jax.ShapeDtypeStruct((b, MLA_HEADS, QK_PAD, s), BF16),
                      jax.ShapeDtypeStruct((b, V_ROWS, s), BF16)]
    else:
        in_specs += [_full(a.shape) for a in extra]
        out_specs += [pl.BlockSpec((1, MLA_HEADS, tm, QK_PAD), lambda bi, i: (bi, 0, i, 0))]
        out_shape += [jax.ShapeDtypeStruct((b, MLA_HEADS, s, QK_PAD), BF16)]
    return pl.pallas_call(
        _proj_cols_kernel if cols else _proj_rows_kernel,
        grid=(b, s // tm),
        in_specs=in_specs,
        out_specs=tuple(out_specs),
        out_shape=tuple(out_shape),
        compiler_params=_params("parallel", "parallel"),
        name="proj_cols" if cols else "proj_rows",
    )(x, tab, *common, *extra)


def _attn_prompt_kernel(qt_ref, k_ref, vt_ref, wuvt_ref, g_ref, o_ref,
                        m_sc, acc_sc, st_sc, cmax_sc, *, qb):
    i = pl.program_id(1)
    m_sc[...] = jnp.full(m_sc.shape, -jnp.inf, F32)
    acc_sc[...] = jnp.zeros(acc_sc.shape, F32)

    def step(j, masked):
        ks = pl.ds(pl.multiple_of(j * qb, qb), qb)
        for h in range(MLA_HEADS):
            st = jnp.dot(k_ref[0, ks, :], qt_ref[0, h], preferred_element_type=F32)
            if masked:
                kc = lax.broadcasted_iota(jnp.int32, st.shape, 0) // CHUNK
                qc = lax.broadcasted_iota(jnp.int32, st.shape, 1) // CHUNK
                st = jnp.where(kc <= qc, st, -jnp.inf)
            st_sc[h] = st
            cmax_sc[h:h + 1, :] = jnp.max(st, axis=0, keepdims=True)
        for h in range(MLA_HEADS):
            m_prev = m_sc[h:h + 1, :]
            m_new = jnp.maximum(m_prev, cmax_sc[h:h + 1, :])
            p = jnp.exp(st_sc[h] - m_new).astype(BF16)
            m_sc[h:h + 1, :] = m_new
            acc_sc[h] = (jnp.exp(m_prev - m_new) * acc_sc[h]
                         + jnp.dot(vt_ref[0, :, ks], p, preferred_element_type=F32))

    def body(j, carry):
        step(j, False)
        return carry

    lax.fori_loop(0, i, body, 0)
    step(i, True)
    parts = []
    for h in range(MLA_HEADS):
        acc = acc_sc[h]
        att = (acc[:F_ONES] / acc[F_ONES:F_ONES + 1]).astype(BF16)
        parts.append(jnp.dot(wuvt_ref[h], att, preferred_element_type=F32))
    mla = jnp.concatenate(parts, axis=0).T
    o_ref[0] = _rms(mla, g_ref[...]).astype(BF16)


def _attn_prompt(qt, kcat, vt, wuvt, g, qb):
    b, _, _, s = qt.shape
    return pl.pallas_call(
        functools.partial(_attn_prompt_kernel, qb=qb),
        grid=(b, s // qb),
        in_specs=[pl.BlockSpec((1, MLA_HEADS, QK_PAD, qb), lambda bi, i: (bi, 0, 0, i)),
                  pl.BlockSpec((1, s, QK_PAD), lambda bi, i: (bi, 0, 0)),
                  pl.BlockSpec((1, V_ROWS, s), lambda bi, i: (bi, 0, 0)),
                  _full(wuvt.shape), _full(g.shape)],
        out_specs=pl.BlockSpec((1, qb, MLA_WIDTH), lambda bi, i: (bi, i, 0)),
        out_shape=jax.ShapeDtypeStruct((b, s, MLA_WIDTH), BF16),
        scratch_shapes=[pltpu.VMEM((MLA_HEADS, qb), F32), pltpu.VMEM((MLA_HEADS, V_ROWS, qb), F32),
                        pltpu.VMEM((MLA_HEADS, qb, qb), F32), pltpu.VMEM((MLA_HEADS, qb), F32)],
        compiler_params=_params("parallel", "arbitrary"),
        name="attn_prompt",
    )(qt, kcat, vt, wuvt, g)


def _attn_sample_kernel(q_ref, pckv_ref, pkr_ref, k_ref, wuv_ref, g_ref, o_ref, *, sd, past):
    rows = MLA_HEADS * sd
    q = q_ref[0].reshape(rows, QK_PAD)
    knew = k_ref[0]
    s_past = (lax.dot_general(q[:, :LANES], pckv_ref[0].astype(BF16), _NT, preferred_element_type=F32)
              + lax.dot_general(q[:, F_ROPE:F_ROPE + MLA_ROPE], pkr_ref[0].astype(BF16), _NT,
                                preferred_element_type=F32))
    s_new = lax.dot_general(q, knew, _NT, preferred_element_type=F32)

    def mask(s, k0):
        qc = (past + lax.broadcasted_iota(jnp.int32, s.shape, 0) % sd) // CHUNK
        kc = (k0 + lax.broadcasted_iota(jnp.int32, s.shape, 1)) // CHUNK
        return jnp.where(kc <= qc, s, -jnp.inf)

    s_past = mask(s_past, 0)
    s_new = mask(s_new, past)
    m = jnp.maximum(jnp.max(s_past, axis=-1, keepdims=True), jnp.max(s_new, axis=-1, keepdims=True))
    p_past = jnp.exp(s_past - m).astype(BF16)
    p_new = jnp.exp(s_new - m).astype(BF16)
    acc = jnp.dot(p_new, knew, preferred_element_type=F32)
    num = acc[:, :LANES] + jnp.dot(p_past, pckv_ref[0].astype(BF16), preferred_element_type=F32)
    den = acc[:, F_ONES:F_ONES + 1] + jnp.sum(p_past.astype(F32), axis=-1, keepdims=True)
    att = (num / den).astype(BF16)
    mla = jnp.zeros((sd, MLA_WIDTH), F32)
    for h in range(MLA_HEADS):
        mla = mla + jnp.dot(att[h * sd:(h + 1) * sd], wuv_ref[h], preferred_element_type=F32)
    o_ref[0] = _rms(mla, g_ref[...]).astype(BF16)


def _attn_sample(q, pckv, pkr, kcat, wuv, g, nb, sd):
    past = pckv.shape[1]
    return pl.pallas_call(
        functools.partial(_attn_sample_kernel, sd=sd, past=past),
        grid=(nb,),
        in_specs=[pl.BlockSpec((1, MLA_HEADS, sd, QK_PAD), lambda b: (0, 0, b, 0)),
                  pl.BlockSpec((1, past, MLA_KVRANK), lambda b: (b, 0, 0)),
                  pl.BlockSpec((1, past, MLA_ROPE), lambda b: (b, 0, 0)),
                  pl.BlockSpec((1, sd, QK_PAD), lambda b: (0, b, 0)),
                  _full(wuv.shape), _full(g.shape)],
        out_specs=pl.BlockSpec((1, sd, MLA_WIDTH), lambda b: (0, b, 0)),
        out_shape=jax.ShapeDtypeStruct((1, nb * sd, MLA_WIDTH), BF16),
        compiler_params=_params("parallel"),
        name="attn_sample",
    )(q, pckv, pkr, kcat, wuv, g)


def _lru_kernel(xr_ref, xg_ref, buf0_ref, h0_ref, cw_ref, cb_ref, wa_ref, ba_ref, wi_ref, bi_ref,
                lam_ref, g_ref, o_ref, hlast_ref, newbuf_ref, xp_sc, h_sc, a_sc, b_sc, hh_sc, *, ts):
    t = pl.program_id(1)
    nt = pl.num_programs(1)
    pad = SUBLANES

    @pl.when(t == 0)
    def _():
        xp_sc[pad - (CONV_WIDTH - 1):pad, :] = buf0_ref[0]
        h_sc[...] = h0_ref[0]

    xr = xr_ref[0]
    xp_sc[pad:pad + ts, :] = xr
    xc = cb_ref[...] + cw_ref[CONV_WIDTH - 1:CONV_WIDTH, :] * xr
    for k in range(CONV_WIDTH - 1):
        xc = xc + cw_ref[k:k + 1, :] * xp_sc[pad - (CONV_WIDTH - 1) + k:pad - (CONV_WIDTH - 1) + k + ts, :]
    tail = xp_sc[ts + pad - (CONV_WIDTH - 1):ts + pad, :]
    xp_sc[pad - (CONV_WIDTH - 1):pad, :] = tail

    xcb = xc.astype(BF16)
    r = jax.nn.sigmoid(jnp.dot(xcb, wa_ref[...], preferred_element_type=F32) + ba_ref[...])
    ig = jax.nn.sigmoid(jnp.dot(xcb, wi_ref[...], preferred_element_type=F32) + bi_ref[...])
    log_a = -LRU_C * r * jax.nn.softplus(-lam_ref[...])
    a = jnp.exp(log_a)
    bt = jnp.sqrt(-_expm1(2.0 * log_a)) * (ig * xc)

    ng = ts // SUBLANES
    a3 = a.reshape(ng, SUBLANES, LRU_WIDTH)
    b3 = bt.reshape(ng, SUBLANES, LRU_WIDTH)
    row = lax.broadcasted_iota(jnp.int32, a3.shape, 1)
    for d in (1, 2, 4):
        valid = row >= d
        a_sh = pltpu.roll(a3, d, axis=1)
        b_sh = pltpu.roll(b3, d, axis=1)
        b3 = jnp.where(valid, a3 * b_sh + b3, b3)
        a3 = jnp.where(valid, a3 * a_sh, a3)
    a_sc[...] = a3.reshape(ts, LRU_WIDTH)
    b_sc[...] = b3.reshape(ts, LRU_WIDTH)

    def group(gi, hprev):
        rs = pl.ds(pl.multiple_of(gi * SUBLANES, SUBLANES), SUBLANES)
        hg = a_sc[rs, :] * hprev + b_sc[rs, :]
        hh_sc[rs, :] = hg
        return hg[SUBLANES - 1:SUBLANES, :]

    hfin = lax.fori_loop(0, ng, group, h_sc[...])
    h_sc[...] = hfin

    lru_out = hh_sc[...] * _gelu(xg_ref[0])
    o_ref[0] = _rms(lru_out, g_ref[...]).astype(BF16)

    @pl.when(t == nt - 1)
    def _():
        hlast_ref[0] = hfin
        newbuf_ref[0] = tail


def _lru(xr, xg, buf0, h0, cw, cb, wa, ba, wi, bi, lam, g, ts):
    b, s, w = xr.shape
    row = pl.BlockSpec((1, ts, w), lambda bi_, t: (bi_, t, 0))
    return pl.pallas_call(
        functools.partial(_lru_kernel, ts=ts),
        grid=(b, s // ts),
        in_specs=[row, row,
                  pl.BlockSpec((1, CONV_WIDTH - 1, w), lambda bi_, t: (bi_, 0, 0)),
                  pl.BlockSpec((1, 1, w), lambda bi_, t: (bi_, 0, 0)),
                  _full(cw.shape), _full(cb.shape), _full(wa.shape), _full(ba.shape),
                  _full(wi.shape), _full(bi.shape), _full(lam.shape), _full(g.shape)],
        out_specs=(row,
                   pl.BlockSpec((1, 1, w), lambda bi_, t: (bi_, 0, 0)),
                   pl.BlockSpec((1, CONV_WIDTH - 1, w), lambda bi_, t: (bi_, 0, 0))),
        out_shape=(jax.ShapeDtypeStruct((b, s, w), BF16),
                   jax.ShapeDtypeStruct((b, 1, w), F32),
                   jax.ShapeDtypeStruct((b, CONV_WIDTH - 1, w), F32)),
        scratch_shapes=[pltpu.VMEM((ts + SUBLANES, w), F32), pltpu.VMEM((1, w), F32),
                        pltpu.VMEM((ts, w), F32), pltpu.VMEM((ts, w), F32), pltpu.VMEM((ts, w), F32)],
        compiler_params=_params("parallel", "arbitrary"),
        name="lru",
    )(xr, xg, buf0, h0, cw, cb, wa, ba, wi, bi, lam, g)


def _top_rows(cur, n):
    rows = []
    for _ in range(n):
        m = jnp.max(cur, axis=0, keepdims=True)
        rows.append(m)
        cur = jnp.where(cur == m, -jnp.inf, cur)
    return rows


def _top_ranked(cur, n):
    rows = []
    rank = jnp.full(cur.shape, float(n), F32)
    for r in range(n):
        m = jnp.max(cur, axis=0, keepdims=True)
        hit = cur == m
        rows.append(m)
        rank = jnp.where(hit, float(r), rank)
        cur = jnp.where(hit, -jnp.inf, cur)
    return rows, rank


def _mix_kernel(x_ref, mla_ref, lru_ref, wo_ref, gffn_ref, wqt_ref, k1_ref, k2_ref,
                x1_ref, xn2_ref, rank2_ref, e2_ref, cnt_ref, e1_ref, s1_sc, s2_sc):
    x1 = (x_ref[...]
          + jnp.dot(mla_ref[...], wo_ref[:MLA_WIDTH, :], preferred_element_type=F32)
          + jnp.dot(lru_ref[...], wo_ref[MLA_WIDTH:, :], preferred_element_type=F32))
    x1_ref[...] = x1
    xn2t = _rms(x1, gffn_ref[...]).T.astype(BF16)
    xn2_ref[...] = xn2t
    qt = jnp.dot(wqt_ref[...], xn2t, preferred_element_type=F32).astype(BF16)
    for h in range(PEER_HEADS):
        base = h * PEER_KEY_DIM
        s1_sc[h] = jnp.dot(k1_ref[h], qt[base:base + PEER_HALF], preferred_element_type=F32)
        s2_sc[h] = jnp.dot(k2_ref[h], qt[base + PEER_HALF:base + PEER_KEY_DIM], preferred_element_type=F32)
    k = PEER_TOPK
    groups = x1.shape[0] // LANES

    def body(it, carry):
        h = it // groups
        ls = pl.ds(pl.multiple_of((it % groups) * LANES, LANES), LANES)
        s1 = s1_sc[h, :, ls]
        s2 = s2_sc[h, :, ls]
        v1, rank1 = _top_ranked(s1, k)
        v2, rank2 = _top_ranked(s2, k)
        v1a = jnp.concatenate(v1, axis=0)
        cand = [v1a + v2[0]]
        cand += [v1a[:SUBLANES] + v2[j] for j in range(1, SUBLANES)]
        cand += [v1[0] + jnp.concatenate(v2[SUBLANES:], axis=0)]
        vs = _top_rows(jnp.concatenate(cand, axis=0), k)
        tau = vs[k - 1]
        sel = [c >= tau for c in cand]
        z = sum(jnp.sum(jnp.where(m, jnp.exp(c - vs[0]), 0.0), axis=0, keepdims=True)
                for m, c in zip(sel, cand))
        ones = [jnp.where(m, 1.0, 0.0) for m in sel]
        low = sum(ones[1:SUBLANES])
        cnt = ones[0] + jnp.concatenate([low, jnp.zeros_like(low)], axis=0)
        tail = jnp.sum(ones[SUBLANES], axis=0, keepdims=True)
        cnt_a = jnp.zeros(s1.shape, F32)
        for i in range(k):
            ci = cnt[i:i + 1] + tail if i == 0 else cnt[i:i + 1]
            cnt_a = jnp.where(rank1 == float(i), ci, cnt_a)
        rank2_ref[h, :, ls] = rank2
        e2_ref[h, :, ls] = jnp.exp(s2 - v2[0]) / z
        cnt_ref[h, :, ls] = cnt_a
        e1_ref[h, :, ls] = jnp.exp(s1 - v1[0])
        return carry

    lax.fori_loop(0, PEER_HEADS * groups, body, 0)


def _mix(x, mla, lru, wo, gffn, wqt, k1, k2, tm):
    t, d = x.shape
    row = lambda w: pl.BlockSpec((tm, w), lambda i: (i, 0))
    sc = pl.BlockSpec((PEER_HEADS, PEER_NKEYS, tm), lambda i: (0, 0, i))
    sc_shape = lambda dt: jax.ShapeDtypeStruct((PEER_HEADS, PEER_NKEYS, t), dt)
    return pl.pallas_call(
        _mix_kernel,
        grid=(t // tm,),
        in_specs=[row(d), row(MLA_WIDTH), row(LRU_WIDTH), _full(wo.shape), _full(gffn.shape),
                  _full(wqt.shape), _full(k1.shape), _full(k2.shape)],
        out_specs=(row(d), pl.BlockSpec((d, tm), lambda i: (0, i)), sc, sc, sc, sc),
        out_shape=(jax.ShapeDtypeStruct((t, d), F32), jax.ShapeDtypeStruct((d, t), BF16),
                   sc_shape(F32), sc_shape(F32), sc_shape(F32), sc_shape(F32)),
        scratch_shapes=[pltpu.VMEM((PEER_HEADS, PEER_NKEYS, tm), F32)] * 2,
        compiler_params=_params("parallel"),
        name="mix",
    )(x, mla, lru, wo, gffn, wqt, k1, k2)


def _peer_kernel(xn2_ref, x1_ref, u_ref, vt_ref, rank2_ref, e2_ref, cnt_ref, e1_ref, gfin_ref,
                 y_ref, ht0_sc, ht1_sc, wt0_sc, wt1_sc, yt_sc, *, eb, tb, ne, final_norm):
    g = pl.program_id(0)
    out_tile = jnp.maximum(g - 2, 0) % ne

    @pl.when(g == 0)
    def _():
        for ref in (ht0_sc, ht1_sc, wt0_sc, wt1_sc):
            ref[...] = jnp.zeros(ref.shape, ref.dtype)

    @pl.when(out_tile == 0)
    def _():
        yt_sc[...] = jnp.zeros(yt_sc.shape, F32)

    na = eb // PEER_NKEYS
    mh_rows = eb // 2
    nt_cols = 2 * LANES
    bq_rows = PEER_NKEYS // 4

    def stages(ht_cur, wt_cur, ht_prev, wt_prev):
        def pre_piece(mh, nt):
            ms = slice(mh * mh_rows, (mh + 1) * mh_rows)
            ns = slice(nt * nt_cols, (nt + 1) * nt_cols)
            ht_cur[ms, ns] = jnp.dot(u_ref[ms, :], xn2_ref[:, ns], preferred_element_type=F32)

        def out_piece(mh, nt):
            out_rows = yt_sc.shape[0] // (eb // mh_rows)
            ms = slice(mh * out_rows, (mh + 1) * out_rows)
            ns = slice(nt * nt_cols, (nt + 1) * nt_cols)
            yt_sc[ms, ns] += jnp.dot(vt_ref[ms, :], wt_cur[:, ns], preferred_element_type=F32)

        def mix_block(tl, bq):
            ls = slice(tl * LANES, (tl + 1) * LANES)
            bs = slice(bq * bq_rows, (bq + 1) * bq_rows)
            acc = [jnp.zeros((bq_rows, LANES), F32)] * na
            for h in range(PEER_HEADS):
                r2 = rank2_ref[h, bs, ls]
                e2 = e2_ref[h, bs, ls]
                for al in range(na):
                    hit = r2 < cnt_ref[h, al:al + 1, ls]
                    acc[al] = acc[al] + jnp.where(hit, e1_ref[h, al:al + 1, ls] * e2, 0.0)
            for al in range(na):
                rs = slice(al * PEER_NKEYS + bq * bq_rows, al * PEER_NKEYS + (bq + 1) * bq_rows)
                wt_prev[rs, ls] = (acc[al] * _gelu(ht_prev[rs, ls])).astype(BF16)

        pieces = [(f, mh, nt) for nt in range(tb // nt_cols) for mh in range(eb // mh_rows)
                  for f in (pre_piece, out_piece)]
        blocks = [(tl, bq) for tl in range(tb // LANES) for bq in range(PEER_NKEYS // bq_rows)]
        per = -(-len(blocks) // len(pieces))
        for i, (f, mh, nt) in enumerate(pieces):
            f(mh, nt)
            for tl, bq in blocks[i * per:(i + 1) * per]:
                mix_block(tl, bq)

    @pl.when(g % 2 == 0)
    def _():
        stages(ht0_sc, wt0_sc, ht1_sc, wt1_sc)

    @pl.when(g % 2 == 1)
    def _():
        stages(ht1_sc, wt1_sc, ht0_sc, wt0_sc)

    @pl.when((g >= 2) & (out_tile == ne - 1))
    def _():
        y = x1_ref[...] + yt_sc[...].T
        if final_norm:
            y = _rms(y, gfin_ref[...])
        y_ref[...] = y


def _peer(xn2, x1, u, vt, rank2, e2, cnt, e1, gfin, tb, eb, final_norm):
    t, d = x1.shape
    ne = u.shape[0] // eb
    na = eb // PEER_NKEYS
    steps = (t // tb) * ne
    assert na == SUBLANES and u.shape[0] % eb == 0 and tb % (2 * LANES) == 0 and ne > 1

    def at(lag):
        def pos(g):
            s = jnp.clip(g - lag, 0, steps - 1)
            return s // ne, s % ne
        return pos

    pre, mid, out = at(0), at(1), at(2)
    sc = pl.BlockSpec((PEER_HEADS, PEER_NKEYS, tb), lambda g: (0, 0, mid(g)[0]))
    rows = pl.BlockSpec((PEER_HEADS, na, tb), lambda g: (0, mid(g)[1], mid(g)[0]))
    return pl.pallas_call(
        functools.partial(_peer_kernel, eb=eb, tb=tb, ne=ne, final_norm=final_norm),
        grid=(steps + 2,),
        in_specs=[pl.BlockSpec((d, tb), lambda g: (0, pre(g)[0])),
                  pl.BlockSpec((tb, d), lambda g: (out(g)[0], 0)),
                  pl.BlockSpec((eb, d), lambda g: (pre(g)[1], 0)),
                  pl.BlockSpec((d, eb), lambda g: (0, out(g)[1])),
                  sc, sc, rows, rows,
                  _full(gfin.shape)],
        out_specs=pl.BlockSpec((tb, d), lambda g: (out(g)[0], 0)),
        out_shape=jax.ShapeDtypeStruct((t, d), F32),
        scratch_shapes=[pltpu.VMEM((eb, tb), F32), pltpu.VMEM((eb, tb), F32),
                        pltpu.VMEM((eb, tb), BF16), pltpu.VMEM((eb, tb), BF16), pltpu.VMEM((d, tb), F32)],
        compiler_params=_params("arbitrary"),
        name="peer",
    )(xn2, x1, u, vt, rank2, e2, cnt, e1, gfin)


def _rope_tables(pos):
    half = MLA_ROPE // 2
    inv = ROPE_THETA ** (-jnp.arange(half, dtype=F32) / half)
    ang = pos.astype(F32)[:, None] * inv[None, :]
    cos, sin = jnp.cos(ang), jnp.sin(ang)
    c2 = jnp.concatenate([cos, cos], axis=-1)
    s2 = jnp.concatenate([-sin, sin], axis=-1)
    rows = jnp.concatenate([_place_cols(c2), _place_cols(s2)], axis=-1)
    return rows, jnp.concatenate([c2, s2], axis=-1).T


def _swap_halves(w):
    half = w.shape[-1] // 2
    return jnp.concatenate([w[..., half:], w[..., :half]], axis=-1)


def _place_cols(w):
    pad = [(0, 0)] * (w.ndim - 1) + [(ROPE_OFF, LANES - ROPE_OFF - w.shape[-1])]
    return jnp.pad(w, pad)


def _layer_weights(w_in, w_uq, w_uk, w_uv, lru_wa, lru_wi, w_out, peer_wq, peer_keys1, peer_keys2,
                   peer_u, peer_v):
    o1 = MLA_QRANK
    o2 = o1 + MLA_KVRANK
    o3 = o2 + MLA_ROPE
    kr = w_in[:, o2:o3]
    win = jnp.concatenate([w_in[:, :o2], _place_cols(kr), _place_cols(_swap_halves(kr)),
                           w_in[:, o3:]], axis=1).astype(BF16)
    wint = w_in[:, :o2].T.astype(BF16)
    wq3 = w_uq.reshape(MLA_QRANK, MLA_HEADS, MLA_NOPE + MLA_ROPE)
    nope = wq3[:, :, :MLA_NOPE].reshape(MLA_QRANK, MLA_HEADS * MLA_NOPE)
    rp = wq3[:, :, MLA_NOPE:]
    flat = lambda w: w.reshape(MLA_QRANK, -1)
    wuq = jnp.concatenate([nope, flat(_place_cols(rp)), flat(_place_cols(_swap_halves(rp)))], axis=1).astype(BF16)
    wuqt = jnp.concatenate([nope, flat(rp), flat(_swap_halves(rp))], axis=1).T.astype(BF16)
    eye_h = jnp.eye(MLA_HEADS, dtype=F32)
    wuk = jnp.einsum("rhd,hg->hdgr", w_uk, eye_h).reshape(MLA_HEADS * MLA_NOPE, MLA_HEADS * MLA_KVRANK)
    wuv = jnp.einsum("rhd,hg->hrgd", w_uv, eye_h).reshape(MLA_HEADS, MLA_KVRANK, MLA_WIDTH)
    wuvt = jnp.transpose(w_uv, (1, 2, 0))
    eye_b = jnp.eye(LRU_BLOCKS, dtype=F32)
    bd = lambda w: jnp.einsum("nde,nm->ndme", w, eye_b).reshape(LRU_WIDTH, LRU_WIDTH)
    return dict(win=win, wint=wint, wuq=wuq, wuqt=wuqt, wuk=wuk.astype(BF16), wukt=wuk.T.astype(BF16),
                wuv=wuv.astype(BF16), wuvt=wuvt.astype(BF16),
                wa=bd(lru_wa).astype(BF16), wi=bd(lru_wi).astype(BF16), wo=w_out.astype(BF16),
                wqt=peer_wq.T.astype(BF16), k1=peer_keys1.astype(BF16), k2=peer_keys2.astype(BF16),
                u=peer_u.astype(BF16), vt=peer_v.T.astype(BF16))


def _col(v):
    return v.reshape(-1, 1).astype(F32)


def _row(v):
    return v.reshape(1, -1).astype(F32)


def _token_tile(n, cap):
    t = min(n, cap)
    assert n % t == 0, (n, t)
    return t


def kernel(x_prompt, x_sample, cache_mla_ckv, cache_mla_krope, state_lru_h, state_lru_conv, norm_mix, w_in, norm_q, w_uq, norm_kv, w_uk, w_uv, conv_w, conv_b, lru_wa, lru_ba, lru_wi, lru_bi, lru_lambda, norm_mla_out, norm_lru_out, w_out, norm_ffn, peer_wq, peer_keys1, peer_keys2, peer_u, peer_v, norm_final):
    bp, sp, d = x_prompt.shape
    bs, sd, _ = x_sample.shape
    depth = w_in.shape[0]
    past = cache_mla_ckv.shape[2]
    ts_tok = bs * sd
    tab_p, tabt_p = _rope_tables(jnp.arange(sp))
    tab_s = jnp.tile(_rope_tables(past + jnp.arange(sd))[0], (bs, 1))
    gfin = _row(norm_final)

    xp = x_prompt
    xs = x_sample.reshape(1, ts_tok, d)
    outs = [[] for _ in range(8)]
    for l in range(depth):
        w = _layer_weights(w_in[l], w_uq[l], w_uk[l], w_uv[l], lru_wa[l], lru_wi[l], w_out[l], peer_wq[l],
                           peer_keys1[l], peer_keys2[l], peer_u[l], peer_v[l])
        last = l == depth - 1
        lru_args = (conv_w[l].astype(F32), _row(conv_b[l]), w["wa"], _row(lru_ba[l]), w["wi"], _row(lru_bi[l]),
                    _row(lru_lambda[l]), _row(norm_lru_out[l]))
        proj_common = (_row(norm_mix[l]), w["win"], _row(norm_kv[l]))
        proj_rows = (_row(norm_q[l]), w["wuq"], w["wuk"])
        proj_cols = (tabt_p, w["wint"], _col(norm_q[l]), _col(norm_kv[l]), w["wuqt"], w["wukt"])
        gmla = _row(norm_mla_out[l])

        def tail(x2d, mla, lru):
            t = x2d.shape[0]
            tm = _token_tile(t, 512)
            x1, xn2, rank2, e2, cnt, e1 = _mix(x2d, mla, lru, w["wo"], _row(norm_ffn[l]), w["wqt"],
                                               w["k1"], w["k2"], tm)
            return _peer(xn2, x1, w["u"], w["vt"], rank2, e2, cnt, e1, gfin, tm, PEER_TILE, last)

        kcat, ckv, kr, xr, xg, qt, vt = _proj(xp, tab_p, proj_common, proj_cols, _token_tile(sp, 512), True)
        mla = _attn_prompt(qt, kcat, vt, w["wuvt"], gmla, _token_tile(sp, 512))
        lru, hl, nb = _lru(xr, xg, jnp.zeros((bp, CONV_WIDTH - 1, LRU_WIDTH), F32),
                           jnp.zeros((bp, 1, LRU_WIDTH), F32), *lru_args, _token_tile(sp, 512))
        xp = tail(xp.reshape(bp * sp, d), mla.reshape(bp * sp, MLA_WIDTH),
                  lru.reshape(bp * sp, LRU_WIDTH)).reshape(bp, sp, d)
        for lst, v in zip(outs[:4], (ckv, kr, hl[:, 0], nb)):
            lst.append(v)

        kcat, ckv, kr, xr, xg, q = _proj(xs, tab_s, proj_common, proj_rows, _token_tile(ts_tok, 512), False)
        mla = _attn_sample(q, cache_mla_ckv[l], cache_mla_krope[l], kcat, w["wuv"], gmla, bs, sd)
        lru, hl, nb = _lru(xr.reshape(bs, sd, LRU_WIDTH), xg.reshape(bs, sd, LRU_WIDTH),
                           state_lru_conv[l].astype(F32), state_lru_h[l].reshape(bs, 1, LRU_WIDTH).astype(F32),
                           *lru_args, sd)
        xs = tail(xs[0], mla[0], lru.reshape(ts_tok, LRU_WIDTH)).reshape(1, ts_tok, d)
        for lst, v in zip(outs[4:], (ckv.reshape(bs, sd, MLA_KVRANK), kr.reshape(bs, sd, MLA_ROPE), hl[:, 0], nb)):
            lst.append(v)

    return (xp, xs.reshape(bs, sd, d)) + tuple(jnp.stack(o) for o in outs)
```

```python
import functools

import jax
import jax.numpy as jnp
from jax import lax
from jax.experimental import pallas as pl
from jax.experimental.pallas import tpu as pltpu

F32 = jnp.float32
BF16 = jnp.bfloat16

D_MODEL = 1024
CHUNK = 64
EPS = 1e-6
MLA_HEADS = 8
MLA_NOPE = 64
MLA_ROPE = 32
MLA_VDIM = 64
MLA_QRANK = 256
MLA_KVRANK = 128
ROPE_THETA = 10000.0
MLA_WIDTH = MLA_HEADS * MLA_VDIM
MLA_SCALE = (MLA_NOPE + MLA_ROPE) ** -0.5
QK_PAD = 256
LRU_WIDTH = 512
LRU_BLOCKS = 8
LRU_BLOCK_DIM = LRU_WIDTH // LRU_BLOCKS
CONV_WIDTH = 4
LRU_C = 8.0
PEER_HEADS = 8
PEER_NKEYS = 128
PEER_KEY_DIM = 256
PEER_HALF = PEER_KEY_DIM // 2
PEER_TOPK = 16
LANES = 128
SUBLANES = 8
PEER_TILE = 8 * PEER_NKEYS
VMEM_LIMIT = 56 * 1024 * 1024

F_ONES = MLA_KVRANK
F_ROPE = F_ONES + 16
V_ROWS = F_ROPE
ROPE_OFF = F_ROPE - LANES
ZC_CQ, ZC_CKV, ZC_KR, ZC_KRS, ZC_XR, ZC_XG, ZC_END = 0, 256, 384, 512, 640, 1152, 1664
QC_NOPE, QC_ROPE, QC_ROPES, QC_END = 0, 512, 1536, 2560
QR_NOPE, QR_ROPE, QR_ROPES, QR_END = 0, 512, 768, 1024

_NT = (((1,), (1,)), ((), ()))


def _rms(x, g):
    return x * lax.rsqrt(jnp.mean(x * x, axis=-1, keepdims=True) + EPS) * g


def _rms_cols(xt, g):
    return xt * lax.rsqrt(jnp.mean(xt * xt, axis=0, keepdims=True) + EPS) * g


def _expm1(y):
    u = jnp.exp(y)
    um1 = u - 1.0
    return jnp.where(um1 == 0.0, y, jnp.where(um1 == -1.0, -1.0, um1 * y / jnp.log(u)))


def _gelu(x):
    return 0.5 * x * (1.0 + lax.erf(x * (2.0 ** -0.5)))


def _params(*sem):
    return pltpu.CompilerParams(dimension_semantics=sem, vmem_limit_bytes=VMEM_LIMIT)


def _full(shape):
    n = len(shape)
    return pl.BlockSpec(shape, lambda *_: (0,) * n)


def _proj_common(x_ref, tab_ref, gmix_ref, win_ref, gkv_ref, kcat_ref, ckv_ref, kr_ref, xr_ref, xg_ref):
    xn = _rms(x_ref[0], gmix_ref[...]).astype(BF16)
    z = jnp.dot(xn, win_ref[...], preferred_element_type=F32)
    ctab = tab_ref[:, :LANES]
    stab = tab_ref[:, LANES:]
    ckv = _rms(z[:, ZC_CKV:ZC_KR], gkv_ref[...])
    kr = z[:, ZC_KR:ZC_KRS] * ctab + z[:, ZC_KRS:ZC_XR] * stab
    ckv_ref[0] = ckv
    kr_ref[0] = kr[:, ROPE_OFF:ROPE_OFF + MLA_ROPE]
    kcat_ref[0, :, :LANES] = ckv.astype(BF16)
    lane = lax.broadcasted_iota(jnp.int32, kr.shape, 1)
    kcat_ref[0, :, LANES:] = jnp.where(lane < ROPE_OFF, 1.0, kr).astype(BF16)
    xr_ref[0] = z[:, ZC_XR:ZC_XG]
    xg_ref[0] = z[:, ZC_XG:ZC_END]
    return xn, z, ctab, stab


def _proj_rows_kernel(x_ref, tab_ref, gmix_ref, win_ref, gkv_ref, gq_ref, wuq_ref, wuk_ref,
                      kcat_ref, ckv_ref, kr_ref, xr_ref, xg_ref, q_ref):
    _, z, ctab, stab = _proj_common(x_ref, tab_ref, gmix_ref, win_ref, gkv_ref,
                                    kcat_ref, ckv_ref, kr_ref, xr_ref, xg_ref)
    cqn = _rms(z[:, ZC_CQ:ZC_CKV], gq_ref[...]).astype(BF16)
    q = jnp.dot(cqn, wuq_ref[...], preferred_element_type=F32)
    qlat = jnp.dot(q[:, QC_NOPE:QC_ROPE].astype(BF16), wuk_ref[...], preferred_element_type=F32)
    for h in range(MLA_HEADS):
        q_ref[0, h, :, :LANES] = (qlat[:, h * LANES:(h + 1) * LANES] * MLA_SCALE).astype(BF16)
        rp = (q[:, QC_ROPE + h * LANES:QC_ROPE + (h + 1) * LANES] * ctab
              + q[:, QC_ROPES + h * LANES:QC_ROPES + (h + 1) * LANES] * stab)
        q_ref[0, h, :, LANES:] = (rp * MLA_SCALE).astype(BF16)


def _proj_cols_kernel(x_ref, tab_ref, gmix_ref, win_ref, gkv_ref, tabt_ref, wint_ref, gqc_ref, gkvc_ref,
                      wuqt_ref, wukt_ref, kcat_ref, ckv_ref, kr_ref, xr_ref, xg_ref, qt_ref, vt_ref):
    xn, _, _, _ = _proj_common(x_ref, tab_ref, gmix_ref, win_ref, gkv_ref,
                               kcat_ref, ckv_ref, kr_ref, xr_ref, xg_ref)
    tm = xn.shape[0]
    zt = lax.dot_general(wint_ref[...], xn, _NT, preferred_element_type=F32)
    ckvt = _rms_cols(zt[MLA_QRANK:], gkvc_ref[...])
    vt_ref[0, :F_ONES, :] = ckvt.astype(BF16)
    vt_ref[0, F_ONES:, :] = jnp.ones((V_ROWS - F_ONES, tm), BF16)
    cqnt = _rms_cols(zt[:MLA_QRANK], gqc_ref[...]).astype(BF16)
    qt = jnp.dot(wuqt_ref[...], cqnt, preferred_element_type=F32)
    qlatt = jnp.dot(wukt_ref[...], qt[QR_NOPE:QR_ROPE].astype(BF16), preferred_element_type=F32)
    ctabt = tabt_ref[:MLA_ROPE, :]
    stabt = tabt_ref[MLA_ROPE:, :]
    for h in range(MLA_HEADS):
        qt_ref[0, h, :F_ONES, :] = (qlatt[h * MLA_KVRANK:(h + 1) * MLA_KVRANK] * MLA_SCALE).astype(BF16)
        qt_ref[0, h, F_ONES:F_ROPE, :] = jnp.zeros((F_ROPE - F_ONES, tm), BF16)
        rp = (qt[QR_ROPE + h * MLA_ROPE:QR_ROPE + (h + 1) * MLA_ROPE] * ctabt
              + qt[QR_ROPES + h * MLA_ROPE:QR_ROPES + (h + 1) * MLA_ROPE] * stabt)
        qt_ref[0, h, F_ROPE:F_ROPE + MLA_ROPE, :] = (rp * MLA_SCALE).astype(BF16)
        qt_ref[0, h, F_ROPE + MLA_ROPE:, :] = jnp.zeros((QK_PAD - F_ROPE - MLA_ROPE, tm), BF16)


def _proj(x, tab, common, extra, tm, cols):
    b, s, d = x.shape
    row = lambda w: pl.BlockSpec((1, tm, w), lambda bi, i: (bi, i, 0))
    shape = lambda *dims: jax.ShapeDtypeStruct((b,) + dims, F32)
    out_specs = [row(QK_PAD), row(MLA_KVRANK), row(MLA_ROPE), row(LRU_WIDTH), row(LRU_WIDTH)]
    out_shape = [jax.ShapeDtypeStruct((b, s, QK_PAD), BF16), shape(s, MLA_KVRANK), shape(s, MLA_ROPE),
                 shape(s, LRU_WIDTH), shape(s, LRU_WIDTH)]
    in_specs = [row(d), pl.BlockSpec((tm, 2 * LANES), lambda bi, i: (i, 0))] + [_full(a.shape) for a in common]
    if cols:
        tabt, rest = extra[0], extra[1:]
        in_specs += [pl.BlockSpec((2 * MLA_ROPE, tm), lambda bi, i: (0, i))] + [_full(a.shape) for a in rest]
        out_specs += [pl.BlockSpec((1, MLA_HEADS, QK_PAD, tm), lambda bi, i: (bi, 0, 0, i)),
                      pl.BlockSpec((1, V_ROWS, tm), lambda bi, i: (bi, 0, i))]
        out_shape += [jax.ShapeDtypeStruct((b, MLA_HEADS, QK_PAD, s), BF16),
                      jax.ShapeDtypeStruct((b, V_ROWS, s), BF16)]
    else:
        in_specs += [_full(a.shape) for a in extra]
        out_specs += [pl.BlockSpec((1, MLA_HEADS, tm, QK_PAD), lambda bi, i: (bi, 0, i, 0))]
        out_shape += [jax.ShapeDtypeStruct((b, MLA_HEADS, s, QK_PAD), BF16)]
    return pl.pallas_call(
        _proj_cols_kernel if cols else _proj_rows_kernel,
        grid=(b, s // tm),
        in_specs=in_specs,
        out_specs=tuple(out_specs),
        out_shape=tuple(out_shape),
        compiler_params=_params("parallel", "parallel"),
        name="proj_cols" if cols else "proj_rows",
    )(x, tab, *common, *extra)


def _attn_prompt_kernel(qt_ref, k_ref, vt_ref, wuvt_ref, g_ref, o_ref,
                        m_sc, acc_sc, st_sc, cmax_sc, *, qb):
    i = pl.program_id(1)
    m_sc[...] = jnp.full(m_sc.shape, -jnp.inf, F32)
    acc_sc[...] = jnp.zeros(acc_sc.shape, F32)

    def step(j, masked):
        ks = pl.ds(pl.multiple_of(j * qb, qb), qb)
        for h in range(MLA_HEADS):
            st = jnp.dot(k_ref[0, ks, :], qt_ref[0, h], preferred_element_type=F32)
            if masked:
                kc = lax.broadcasted_iota(jnp.int32, st.shape, 0) // CHUNK
                qc = lax.broadcasted_iota(jnp.int32, st.shape, 1) // CHUNK
                st = jnp.where(kc <= qc, st, -jnp.inf)
            st_sc[h] = st
            cmax_sc[h:h + 1, :] = jnp.max(st, axis=0, keepdims=True)
        for h in range(MLA_HEADS):
            m_prev = m_sc[h:h + 1, :]
            m_new = jnp.maximum(m_prev, cmax_sc[h:h + 1, :])
            p = jnp.exp(st_sc[h] - m_new).astype(BF16)
            m_sc[h:h + 1, :] = m_new
            acc_sc[h] = (jnp.exp(m_prev - m_new) * acc_sc[h]
                         + jnp.dot(vt_ref[0, :, ks], p, preferred_element_type=F32))

    def body(j, carry):
        step(j, False)
        return carry

    lax.fori_loop(0, i, body, 0)
    step(i, True)
    parts = []
    for h in range(MLA_HEADS):
        acc = acc_sc[h]
        att = (acc[:F_ONES] / acc[F_ONES:F_ONES + 1]).astype(BF16)
        parts.append(jnp.dot(wuvt_ref[h], att, preferred_element_type=F32))
    mla = jnp.concatenate(parts, axis=0).T
    o_ref[0] = _rms(mla, g_ref[...]).astype(BF16)


def _attn_prompt(qt, kcat, vt, wuvt, g, qb):
    b, _, _, s = qt.shape
    return pl.pallas_call(
        functools.partial(_attn_prompt_kernel, qb=qb),
        grid=(b, s // qb),
        in_specs=[pl.BlockSpec((1, MLA_HEADS, QK_PAD, qb), lambda bi, i: (bi, 0, 0, i)),
                  pl.BlockSpec((1, s, QK_PAD), lambda bi, i: (bi, 0, 0)),
                  pl.BlockSpec((1, V_ROWS, s), lambda bi, i: (bi, 0, 0)),
                  _full(wuvt.shape), _full(g.shape)],
        out_specs=pl.BlockSpec((1, qb, MLA_WIDTH), lambda bi, i: (bi, i, 0)),
        out_shape=jax.ShapeDtypeStruct((b, s, MLA_WIDTH), BF16),
        scratch_shapes=[pltpu.VMEM((MLA_HEADS, qb), F32), pltpu.VMEM((MLA_HEADS, V_ROWS, qb), F32),
                        pltpu.VMEM((MLA_HEADS, qb, qb), F32), pltpu.VMEM((MLA_HEADS, qb), F32)],
        compiler_params=_params("parallel", "arbitrary"),
        name="attn_prompt",
    )(qt, kcat, vt, wuvt, g)


def _attn_sample_kernel(q_ref, pckv_ref, pkr_ref, k_ref, wuv_ref, g_ref, o_ref, *, sd, past):
    rows = MLA_HEADS * sd
    q = q_ref[0].reshape(rows, QK_PAD)
    knew = k_ref[0]
    s_past = (lax.dot_general(q[:, :LANES], pckv_ref[0].astype(BF16), _NT, preferred_element_type=F32)
              + lax.dot_general(q[:, F_ROPE:F_ROPE + MLA_ROPE], pkr_ref[0].astype(BF16), _NT,
                                preferred_element_type=F32))
    s_new = lax.dot_general(q, knew, _NT, preferred_element_type=F32)

    def mask(s, k0):
        qc = (past + lax.broadcasted_iota(jnp.int32, s.shape, 0) % sd) // CHUNK
        kc = (k0 + lax.broadcasted_iota(jnp.int32, s.shape, 1)) // CHUNK
        return jnp.where(kc <= qc, s, -jnp.inf)

    s_past = mask(s_past, 0)
    s_new = mask(s_new, past)
    m = jnp.maximum(jnp.max(s_past, axis=-1, keepdims=True), jnp.max(s_new, axis=-1, keepdims=True))
    p_past = jnp.exp(s_past - m).astype(BF16)
    p_new = jnp.exp(s_new - m).astype(BF16)
    acc = jnp.dot(p_new, knew, preferred_element_type=F32)
    num = acc[:, :LANES] + jnp.dot(p_past, pckv_ref[0].astype(BF16), preferred_element_type=F32)
    den = acc[:, F_ONES:F_ONES + 1] + jnp.sum(p_past.astype(F32), axis=-1, keepdims=True)
    att = (num / den).astype(BF16)
    mla = jnp.zeros((sd, MLA_WIDTH), F32)
    for h in range(MLA_HEADS):
        mla = mla + jnp.dot(att[h * sd:(h + 1) * sd], wuv_ref[h], preferred_element_type=F32)
    o_ref[0] = _rms(mla, g_ref[...]).astype(BF16)


def _attn_sample(q, pckv, pkr, kcat, wuv, g, nb, sd):
    past = pckv.shape[1]
    return pl.pallas_call(
        functools.partial(_attn_sample_kernel, sd=sd, past=past),
        grid=(nb,),
        in_specs=[pl.BlockSpec((1, MLA_HEADS, sd, QK_PAD), lambda b: (0, 0, b, 0)),
                  pl.BlockSpec((1, past, MLA_KVRANK), lambda b: (b, 0, 0)),
                  pl.BlockSpec((1, past, MLA_ROPE), lambda b: (b, 0, 0)),
                  pl.BlockSpec((1, sd, QK_PAD), lambda b: (0, b, 0)),
                  _full(wuv.shape), _full(g.shape)],
        out_specs=pl.BlockSpec((1, sd, MLA_WIDTH), lambda b: (0, b, 0)),
        out_shape=jax.ShapeDtypeStruct((1, nb * sd, MLA_WIDTH), BF16),
        compiler_params=_params("parallel"),
        name="attn_sample",
    )(q, pckv, pkr, kcat, wuv, g)


def _lru_kernel(xr_ref, xg_ref, buf0_ref, h0_ref, cw_ref, cb_ref, wa_ref, ba_ref, wi_ref, bi_ref,
                lam_ref, g_ref, o_ref, hlast_ref, newbuf_ref, xp_sc, h_sc, a_sc, b_sc, hh_sc, *, ts):
    t = pl.program_id(1)
    nt = pl.num_programs(1)
    pad = SUBLANES

    @pl.when(t == 0)
    def _():
        xp_sc[pad - (CONV_WIDTH - 1):pad, :] = buf0_ref[0]
        h_sc[...] = h0_ref[0]

    xr = xr_ref[0]
    xp_sc[pad:pad + ts, :] = xr
    xc = cb_ref[...] + cw_ref[CONV_WIDTH - 1:CONV_WIDTH, :] * xr
    for k in range(CONV_WIDTH - 1):
        xc = xc + cw_ref[k:k + 1, :] * xp_sc[pad - (CONV_WIDTH - 1) + k:pad - (CONV_WIDTH - 1) + k + ts, :]
    tail = xp_sc[ts + pad - (CONV_WIDTH - 1):ts + pad, :]
    xp_sc[pad - (CONV_WIDTH - 1):pad, :] = tail

    xcb = xc.astype(BF16)
    r = jax.nn.sigmoid(jnp.dot(xcb, wa_ref[...], preferred_element_type=F32) + ba_ref[...])
    ig = jax.nn.sigmoid(jnp.dot(xcb, wi_ref[...], preferred_element_type=F32) + bi_ref[...])
    log_a = -LRU_C * r * jax.nn.softplus(-lam_ref[...])
    a = jnp.exp(log_a)
    bt = jnp.sqrt(-_expm1(2.0 * log_a)) * (ig * xc)

    ng = ts // SUBLANES
    a3 = a.reshape(ng, SUBLANES, LRU_WIDTH)
    b3 = bt.reshape(ng, SUBLANES, LRU_WIDTH)
    row = lax.broadcasted_iota(jnp.int32, a3.shape, 1)
    for d in (1, 2, 4):
        valid = row >= d
        a_sh = pltpu.roll(a3, d, axis=1)
        b_sh = pltpu.roll(b3, d, axis=1)
        b3 = jnp.where(valid, a3 * b_sh + b3, b3)
        a3 = jnp.where(valid, a3 * a_sh, a3)
    a_sc[...] = a3.reshape(ts, LRU_WIDTH)
    b_sc[...] = b3.reshape(ts, LRU_WIDTH)

    def group(gi, hprev):
        rs = pl.ds(pl.multiple_of(gi * SUBLANES, SUBLANES), SUBLANES)
        hg = a_sc[rs, :] * hprev + b_sc[rs, :]
        hh_sc[rs, :] = hg
        return hg[SUBLANES - 1:SUBLANES, :]

    hfin = lax.fori_loop(0, ng, group, h_sc[...])
    h_sc[...] = hfin

    lru_out = hh_sc[...] * _gelu(xg_ref[0])
    o_ref[0] = _rms(lru_out, g_ref[...]).astype(BF16)

    @pl.when(t == nt - 1)
    def _():
        hlast_ref[0] = hfin
        newbuf_ref[0] = tail


def _lru(xr, xg, buf0, h0, cw, cb, wa, ba, wi, bi, lam, g, ts):
    b, s, w = xr.shape
    row = pl.BlockSpec((1, ts, w), lambda bi_, t: (bi_, t, 0))
    return pl.pallas_call(
        functools.partial(_lru_kernel, ts=ts),
        grid=(b, s // ts),
        in_specs=[row, row,
                  pl.BlockSpec((1, CONV_WIDTH - 1, w), lambda bi_, t: (bi_, 0, 0)),
                  pl.BlockSpec((1, 1, w), lambda bi_, t: (bi_, 0, 0)),
                  _full(cw.shape), _full(cb.shape), _full(wa.shape), _full(ba.shape),
                  _full(wi.shape), _full(bi.shape), _full(lam.shape), _full(g.shape)],
        out_specs=(row,
                   pl.BlockSpec((1, 1, w), lambda bi_, t: (bi_, 0, 0)),
                   pl.BlockSpec((1, CONV_WIDTH - 1, w), lambda bi_, t: (bi_, 0, 0))),
        out_shape=(jax.ShapeDtypeStruct((b, s, w), BF16),
                   jax.ShapeDtypeStruct((b, 1, w), F32),
                   jax.ShapeDtypeStruct((b, CONV_WIDTH - 1, w), F32)),
        scratch_shapes=[pltpu.VMEM((ts + SUBLANES, w), F32), pltpu.VMEM((1, w), F32),
                        pltpu.VMEM((ts, w), F32), pltpu.VMEM((ts, w), F32), pltpu.VMEM((ts, w), F32)],
        compiler_params=_params("parallel", "arbitrary"),
        name="lru",
    )(xr, xg, buf0, h0, cw, cb, wa, ba, wi, bi, lam, g)


def _oddeven_pairs(n):
    pairs = []

    def merge(lo, m, r):
        step = r * 2
        if step < m:
            merge(lo, m, step)
            merge(lo + r, m, step)
            pairs.extend((i, i + r) for i in range(lo + r, lo + m - r, step))
        else:
            pairs.append((lo, lo + r))

    def sort(lo, m):
        if m > 1:
            sort(lo, m // 2)
            sort(lo + m // 2, m // 2)
            merge(lo, m, 1)

    sort(0, n)
    return pairs


_SORT16 = _oddeven_pairs(16)


def _top_values(s, n):
    t = s.shape[0] // SUBLANES
    assert t <= 16 and s.shape[0] % SUBLANES == 0
    v = [s[i * SUBLANES:(i + 1) * SUBLANES] for i in range(t)]
    for i, j in _SORT16:
        if j < t:
            v[i], v[j] = jnp.maximum(v[i], v[j]), jnp.minimum(v[i], v[j])
    top = []
    for r in range(n):
        m = jnp.max(v[0], axis=0, keepdims=True)
        top.append(m)
        hit = v[0] == m
        for i in range(min(n - r - 1, t)):
            v[i] = jnp.where(hit, v[i + 1] if i + 1 < t else -jnp.inf, v[i])
    return top


def _mix_kernel(x_ref, mla_ref, lru_ref, wo_ref, gffn_ref, wqt_ref, k1_ref, k2_ref,
                x1_ref, xn2_ref, s2_ref, e2_ref, theta_ref, e1_ref, s1_sc, s2_sc):
    x1 = (x_ref[...]
          + jnp.dot(mla_ref[...], wo_ref[:MLA_WIDTH, :], preferred_element_type=F32)
          + jnp.dot(lru_ref[...], wo_ref[MLA_WIDTH:, :], preferred_element_type=F32))
    x1_ref[...] = x1
    xn2t = _rms(x1, gffn_ref[...]).T.astype(BF16)
    xn2_ref[...] = xn2t
    qt = jnp.dot(wqt_ref[...], xn2t, preferred_element_type=F32).astype(BF16)
    for h in range(PEER_HEADS):
        base = h * PEER_KEY_DIM
        s1_sc[h] = jnp.dot(k1_ref[h], qt[base:base + PEER_HALF], preferred_element_type=F32)
        s2_sc[h] = jnp.dot(k2_ref[h], qt[base + PEER_HALF:base + PEER_KEY_DIM], preferred_element_type=F32)
    k = PEER_TOPK
    groups = x1.shape[0] // LANES

    def body(it, carry):
        h = it // groups
        ls = pl.ds(pl.multiple_of((it % groups) * LANES, LANES), LANES)
        s1 = s1_sc[h, :, ls]
        s2 = s2_sc[h, :, ls]
        v1 = _top_values(s1, k)
        v2 = _top_values(s2, k)
        v1a = jnp.concatenate(v1, axis=0)
        cand = [v1a + v2[0]]
        cand += [v1a[:SUBLANES] + v2[j] for j in range(1, SUBLANES)]
        cand += [v1[0] + jnp.concatenate(v2[SUBLANES:], axis=0)]
        vs = _top_values(jnp.concatenate(cand, axis=0), k)
        tau = vs[k - 1]
        sel = [c >= tau for c in cand]
        z = sum(jnp.sum(jnp.where(m, jnp.exp(c - vs[0]), 0.0), axis=0, keepdims=True)
                for m, c in zip(sel, cand))
        ones = [jnp.where(m, 1.0, 0.0) for m in sel]
        low = sum(ones[1:SUBLANES])
        cnt = ones[0] + jnp.concatenate([low, jnp.zeros_like(low)], axis=0)
        tail = jnp.sum(ones[SUBLANES], axis=0, keepdims=True)
        cnt = cnt + jnp.where(lax.broadcasted_iota(jnp.int32, cnt.shape, 0) == 0, tail, 0.0)
        th_row = jnp.full(cnt.shape, jnp.inf, F32)
        for j in range(k):
            th_row = jnp.where(cnt == float(j + 1), v2[j], th_row)
        theta = jnp.full(s1.shape, jnp.inf, F32)
        for i in range(k):
            theta = jnp.where(s1 == v1[i], th_row[i:i + 1], theta)
        s2_ref[h, :, ls] = s2
        e2_ref[h, :, ls] = jnp.exp(s2 - v2[0]) / z
        theta_ref[h, :, ls] = theta
        e1_ref[h, :, ls] = jnp.exp(s1 - v1[0])
        return carry

    lax.fori_loop(0, PEER_HEADS * groups, body, 0)


def _mix(x, mla, lru, wo, gffn, wqt, k1, k2, tm):
    t, d = x.shape
    row = lambda w: pl.BlockSpec((tm, w), lambda i: (i, 0))
    sc = pl.BlockSpec((PEER_HEADS, PEER_NKEYS, tm), lambda i: (0, 0, i))
    sc_shape = lambda dt: jax.ShapeDtypeStruct((PEER_HEADS, PEER_NKEYS, t), dt)
    return pl.pallas_call(
        _mix_kernel,
        grid=(t // tm,),
        in_specs=[row(d), row(MLA_WIDTH), row(LRU_WIDTH), _full(wo.shape), _full(gffn.shape),
                  _full(wqt.shape), _full(k1.shape), _full(k2.shape)],
        out_specs=(row(d), pl.BlockSpec((d, tm), lambda i: (0, i)), sc, sc, sc, sc),
        out_shape=(jax.ShapeDtypeStruct((t, d), F32), jax.ShapeDtypeStruct((d, t), BF16),
                   sc_shape(F32), sc_shape(F32), sc_shape(F32), sc_shape(F32)),
        scratch_shapes=[pltpu.VMEM((PEER_HEADS, PEER_NKEYS, tm), F32)] * 2,
        compiler_params=_params("parallel"),
        name="mix",
    )(x, mla, lru, wo, gffn, wqt, k1, k2)


def _peer_kernel(xn2_ref, x1_ref, u_ref, vt_ref, s2_ref, e2_ref, theta_ref, e1_ref, gfin_ref,
                 y_ref, ht0_sc, ht1_sc, wt0_sc, wt1_sc, yt_sc, *, eb, tb, ne, final_norm):
    g = pl.program_id(0)
    out_tile = jnp.maximum(g - 2, 0) % ne

    @pl.when(g == 0)
    def _():
        for ref in (ht0_sc, ht1_sc, wt0_sc, wt1_sc):
            ref[...] = jnp.zeros(ref.shape, ref.dtype)

    @pl.when(out_tile == 0)
    def _():
        yt_sc[...] = jnp.zeros(yt_sc.shape, F32)

    na = eb // PEER_NKEYS
    mh_rows = eb // 2
    nt_cols = 2 * LANES
    bq_rows = PEER_NKEYS // 4

    def stages(ht_cur, wt_cur, ht_prev, wt_prev):
        def pre_piece(mh, nt):
            ms = slice(mh * mh_rows, (mh + 1) * mh_rows)
            ns = slice(nt * nt_cols, (nt + 1) * nt_cols)
            ht_cur[ms, ns] = jnp.dot(u_ref[ms, :], xn2_ref[:, ns], preferred_element_type=F32)

        def out_piece(mh, nt):
            out_rows = yt_sc.shape[0] // (eb // mh_rows)
            ms = slice(mh * out_rows, (mh + 1) * out_rows)
            ns = slice(nt * nt_cols, (nt + 1) * nt_cols)
            yt_sc[ms, ns] += jnp.dot(vt_ref[ms, :], wt_cur[:, ns], preferred_element_type=F32)

        def mix_block(tl, bq):
            ls = slice(tl * LANES, (tl + 1) * LANES)
            bs = slice(bq * bq_rows, (bq + 1) * bq_rows)
            acc = [jnp.zeros((bq_rows, LANES), F32)] * na
            for h in range(PEER_HEADS):
                s2 = s2_ref[h, bs, ls]
                e2 = e2_ref[h, bs, ls]
                for al in range(na):
                    hit = s2 >= theta_ref[h, al:al + 1, ls]
                    acc[al] = acc[al] + jnp.where(hit, e1_ref[h, al:al + 1, ls] * e2, 0.0)
            for al in range(na):
                rs = slice(al * PEER_NKEYS + bq * bq_rows, al * PEER_NKEYS + (bq + 1) * bq_rows)
                wt_prev[rs, ls] = (acc[al] * _gelu(ht_prev[rs, ls])).astype(BF16)

        pieces = [(f, mh, nt) for nt in range(tb // nt_cols) for mh in range(eb // mh_rows)
                  for f in (pre_piece, out_piece)]
        blocks = [(tl, bq) for tl in range(tb // LANES) for bq in range(PEER_NKEYS // bq_rows)]
        per = -(-len(blocks) // len(pieces))
        for i, (f, mh, nt) in enumerate(pieces):
            f(mh, nt)
            for tl, bq in blocks[i * per:(i + 1) * per]:
                mix_block(tl, bq)

    @pl.when(g % 2 == 0)
    def _():
        stages(ht0_sc, wt0_sc, ht1_sc, wt1_sc)

    @pl.when(g % 2 == 1)
    def _():
        stages(ht1_sc, wt1_sc, ht0_sc, wt0_sc)

    @pl.when((g >= 2) & (out_tile == ne - 1))
    def _():
        y = x1_ref[...] + yt_sc[...].T
        if final_norm:
            y = _rms(y, gfin_ref[...])
        y_ref[...] = y


def _peer(xn2, x1, u, vt, s2, e2, theta, e1, gfin, tb, eb, final_norm):
    t, d = x1.shape
    ne = u.shape[0] // eb
    na = eb // PEER_NKEYS
    steps = (t // tb) * ne
    assert na == SUBLANES and u.shape[0] % eb == 0 and tb % (2 * LANES) == 0 and ne > 1

    def at(lag):
        def pos(g):
            s = jnp.clip(g - lag, 0, steps - 1)
            return s // ne, s % ne
        return pos

    pre, mid, out = at(0), at(1), at(2)
    sc = pl.BlockSpec((PEER_HEADS, PEER_NKEYS, tb), lambda g: (0, 0, mid(g)[0]))
    rows = pl.BlockSpec((PEER_HEADS, na, tb), lambda g: (0, mid(g)[1], mid(g)[0]))
    return pl.pallas_call(
        functools.partial(_peer_kernel, eb=eb, tb=tb, ne=ne, final_norm=final_norm),
        grid=(steps + 2,),
        in_specs=[pl.BlockSpec((d, tb), lambda g: (0, pre(g)[0])),
                  pl.BlockSpec((tb, d), lambda g: (out(g)[0], 0)),
                  pl.BlockSpec((eb, d), lambda g: (pre(g)[1], 0)),
                  pl.BlockSpec((d, eb), lambda g: (0, out(g)[1])),
                  sc, sc, rows, rows,
                  _full(gfin.shape)],
        out_specs=pl.BlockSpec((tb, d), lambda g: (out(g)[0], 0)),
        out_shape=jax.ShapeDtypeStruct((t, d), F32),
        scratch_shapes=[pltpu.VMEM((eb, tb), F32), pltpu.VMEM((eb, tb), F32),
                        pltpu.VMEM((eb, tb), BF16), pltpu.VMEM((eb, tb), BF16), pltpu.VMEM((d, tb), F32)],
        compiler_params=_params("arbitrary"),
        name="peer",
    )(xn2, x1, u, vt, s2, e2, theta, e1, gfin)


def _rope_tables(pos):
    half = MLA_ROPE // 2
    inv = ROPE_THETA ** (-jnp.arange(half, dtype=F32) / half)
    ang = pos.astype(F32)[:, None] * inv[None, :]
    cos, sin = jnp.cos(ang), jnp.sin(ang)
    c2 = jnp.concatenate([cos, cos], axis=-1)
    s2 = jnp.concatenate([-sin, sin], axis=-1)
    rows = jnp.concatenate([_place_cols(c2), _place_cols(s2)], axis=-1)
    return rows, jnp.concatenate([c2, s2], axis=-1).T


def _swap_halves(w):
    half = w.shape[-1] // 2
    return jnp.concatenate([w[..., half:], w[..., :half]], axis=-1)


def _place_cols(w):
    pad = [(0, 0)] * (w.ndim - 1) + [(ROPE_OFF, LANES - ROPE_OFF - w.shape[-1])]
    return jnp.pad(w, pad)


def _layer_weights(w_in, w_uq, w_uk, w_uv, lru_wa, lru_wi, w_out, peer_wq, peer_keys1, peer_keys2,
                   peer_u, peer_v):
    o1 = MLA_QRANK
    o2 = o1 + MLA_KVRANK
    o3 = o2 + MLA_ROPE
    kr = w_in[:, o2:o3]
    win = jnp.concatenate([w_in[:, :o2], _place_cols(kr), _place_cols(_swap_halves(kr)),
                           w_in[:, o3:]], axis=1).astype(BF16)
    wint = w_in[:, :o2].T.astype(BF16)
    wq3 = w_uq.reshape(MLA_QRANK, MLA_HEADS, MLA_NOPE + MLA_ROPE)
    nope = wq3[:, :, :MLA_NOPE].reshape(MLA_QRANK, MLA_HEADS * MLA_NOPE)
    rp = wq3[:, :, MLA_NOPE:]
    flat = lambda w: w.reshape(MLA_QRANK, -1)
    wuq = jnp.concatenate([nope, flat(_place_cols(rp)), flat(_place_cols(_swap_halves(rp)))], axis=1).astype(BF16)
    wuqt = jnp.concatenate([nope, flat(rp), flat(_swap_halves(rp))], axis=1).T.astype(BF16)
    eye_h = jnp.eye(MLA_HEADS, dtype=F32)
    wuk = jnp.einsum("rhd,hg->hdgr", w_uk, eye_h).reshape(MLA_HEADS * MLA_NOPE, MLA_HEADS * MLA_KVRANK)
    wuv = jnp.einsum("rhd,hg->hrgd", w_uv, eye_h).reshape(MLA_HEADS, MLA_KVRANK, MLA_WIDTH)
    wuvt = jnp.transpose(w_uv, (1, 2, 0))
    eye_b = jnp.eye(LRU_BLOCKS, dtype=F32)
    bd = lambda w: jnp.einsum("nde,nm->ndme", w, eye_b).reshape(LRU_WIDTH, LRU_WIDTH)
    return dict(win=win, wint=wint, wuq=wuq, wuqt=wuqt, wuk=wuk.astype(BF16), wukt=wuk.T.astype(BF16),
                wuv=wuv.astype(BF16), wuvt=wuvt.astype(BF16),
                wa=bd(lru_wa).astype(BF16), wi=bd(lru_wi).astype(BF16), wo=w_out.astype(BF16),
                wqt=peer_wq.T.astype(BF16), k1=peer_keys1.astype(BF16), k2=peer_keys2.astype(BF16),
                u=peer_u.astype(BF16), vt=peer_v.T.astype(BF16))


def _col(v):
    return v.reshape(-1, 1).astype(F32)


def _row(v):
    return v.reshape(1, -1).astype(F32)


def _token_tile(n, cap):
    t = min(n, cap)
    assert n % t == 0, (n, t)
    return t


def kernel(x_prompt, x_sample, cache_mla_ckv, cache_mla_krope, state_lru_h, state_lru_conv, norm_mix, w_in, norm_q, w_uq, norm_kv, w_uk, w_uv, conv_w, conv_b, lru_wa, lru_ba, lru_wi, lru_bi, lru_lambda, norm_mla_out, norm_lru_out, w_out, norm_ffn, peer_wq, peer_keys1, peer_keys2, peer_u, peer_v, norm_final):
    bp, sp, d = x_prompt.shape
    bs, sd, _ = x_sample.shape
    depth = w_in.shape[0]
    past = cache_mla_ckv.shape[2]
    ts_tok = bs * sd
    tab_p, tabt_p = _rope_tables(jnp.arange(sp))
    tab_s = jnp.tile(_rope_tables(past + jnp.arange(sd))[0], (bs, 1))
    gfin = _row(norm_final)

    xp = x_prompt
    xs = x_sample.reshape(1, ts_tok, d)
    outs = [[] for _ in range(8)]
    for l in range(depth):
        w = _layer_weights(w_in[l], w_uq[l], w_uk[l], w_uv[l], lru_wa[l], lru_wi[l], w_out[l], peer_wq[l],
                           peer_keys1[l], peer_keys2[l], peer_u[l], peer_v[l])
        last = l == depth - 1
        lru_args = (conv_w[l].astype(F32), _row(conv_b[l]), w["wa"], _row(lru_ba[l]), w["wi"], _row(lru_bi[l]),
                    _row(lru_lambda[l]), _row(norm_lru_out[l]))
        proj_common = (_row(norm_mix[l]), w["win"], _row(norm_kv[l]))
        proj_rows = (_row(norm_q[l]), w["wuq"], w["wuk"])
        proj_cols = (tabt_p, w["wint"], _col(norm_q[l]), _col(norm_kv[l]), w["wuqt"], w["wukt"])
        gmla = _row(norm_mla_out[l])

        def tail(x2d, mla, lru):
            t = x2d.shape[0]
            tm = _token_tile(t, 512)
            x1, xn2, s2, e2, theta, e1 = _mix(x2d, mla, lru, w["wo"], _row(norm_ffn[l]), w["wqt"],
                                               w["k1"], w["k2"], tm)
            return _peer(xn2, x1, w["u"], w["vt"], s2, e2, theta, e1, gfin, tm, PEER_TILE, last)

        kcat, ckv, kr, xr, xg, qt, vt = _proj(xp, tab_p, proj_common, proj_cols, _token_tile(sp, 512), True)
        mla = _attn_prompt(qt, kcat, vt, w["wuvt"], gmla, _token_tile(sp, 512))
        lru, hl, nb = _lru(xr, xg, jnp.zeros((bp, CONV_WIDTH - 1, LRU_WIDTH), F32),
                           jnp.zeros((bp, 1, LRU_WIDTH), F32), *lru_args, _token_tile(sp, 512))
        xp = tail(xp.reshape(bp * sp, d), mla.reshape(bp * sp, MLA_WIDTH),
                  lru.reshape(bp * sp, LRU_WIDTH)).reshape(bp, sp, d)
        for lst, v in zip(outs[:4], (ckv, kr, hl[:, 0], nb)):
            lst.append(v)

        kcat, ckv, kr, xr, xg, q = _proj(xs, tab_s, proj_common, proj_rows, _token_tile(ts_tok, 512), False)
        mla = _attn_sample(q, cache_mla_ckv[l], cache_mla_krope[l], kcat, w["wuv"], gmla, bs, sd)
        lru, hl, nb = _lru(xr.reshape(bs, sd, LRU_WIDTH), xg.reshape(bs, sd, LRU_WIDTH),
                           state_lru_conv[l].astype(F32), state_lru_h[l].reshape(bs, 1, LRU_WIDTH).astype(F32),
                           *lru_args, sd)
        xs = tail(xs[0], mla[0], lru.reshape(ts_tok, LRU_WIDTH)).reshape(1, ts_tok, d)
        for lst, v in zip(outs[4:], (ckv.reshape(bs, sd, MLA_KVRANK), kr.reshape(bs, sd, MLA_ROPE), hl[:, 0], nb)):
            lst.append(v)

    return (xp, xs.reshape(bs, sd, d)) + tuple(jnp.stack(o) for o in outs)
```

```python
import functools

import jax
import jax.numpy as jnp
from jax import lax
from jax.experimental import pallas as pl
from jax.experimental.pallas import tpu as pltpu

F32 = jnp.float32
BF16 = jnp.bfloat16

D_MODEL = 1024
CHUNK = 64
EPS = 1e-6
MLA_HEADS = 8
MLA_NOPE = 64
MLA_ROPE = 32
MLA_VDIM = 64
MLA_QRANK = 256
MLA_KVRANK = 128
ROPE_THETA = 10000.0
MLA_WIDTH = MLA_HEADS * MLA_VDIM
MLA_SCALE = (MLA_NOPE + MLA_ROPE) ** -0.5
QK_PAD = 256
LRU_WIDTH = 512
LRU_BLOCKS = 8
LRU_BLOCK_DIM = LRU_WIDTH // LRU_BLOCKS
CONV_WIDTH = 4
LRU_C = 8.0
PEER_HEADS = 8
PEER_NKEYS = 128
PEER_KEY_DIM = 256
PEER_HALF = PEER_KEY_DIM // 2
PEER_TOPK = 16
LANES = 128
SUBLANES = 8
SCORE_LEAD = 2
PEER_TILE = 8 * PEER_NKEYS
VMEM_LIMIT = 56 * 1024 * 1024

F_ONES = MLA_KVRANK
F_ROPE = F_ONES + 16
V_ROWS = F_ROPE
ROPE_OFF = F_ROPE - LANES
ZC_CQ, ZC_CKV, ZC_KR, ZC_KRS, ZC_XR, ZC_XG, ZC_END = 0, 256, 384, 512, 640, 1152, 1664
QC_NOPE, QC_ROPE, QC_ROPES, QC_END = 0, 512, 1536, 2560
QR_NOPE, QR_ROPE, QR_ROPES, QR_END = 0, 512, 768, 1024

_NT = (((1,), (1,)), ((), ()))


def _rms(x, g):
    return x * lax.rsqrt(jnp.mean(x * x, axis=-1, keepdims=True) + EPS) * g


def _rms_cols(xt, g):
    return xt * lax.rsqrt(jnp.mean(xt * xt, axis=0, keepdims=True) + EPS) * g


def _expm1(y):
    u = jnp.exp(y)
    um1 = u - 1.0
    return jnp.where(um1 == 0.0, y, jnp.where(um1 == -1.0, -1.0, um1 * y / jnp.log(u)))


def _gelu(x):
    return 0.5 * x * (1.0 + lax.erf(x * (2.0 ** -0.5)))


def _params(*sem):
    return pltpu.CompilerParams(dimension_semantics=sem, vmem_limit_bytes=VMEM_LIMIT)


def _full(shape):
    n = len(shape)
    return pl.BlockSpec(shape, lambda *_: (0,) * n)


def _proj_common(x_ref, tab_ref, gmix_ref, win_ref, gkv_ref, kcat_ref, ckv_ref, kr_ref, xr_ref, xg_ref):
    xn = _rms(x_ref[0], gmix_ref[...]).astype(BF16)
    z = jnp.dot(xn, win_ref[...], preferred_element_type=F32)
    ctab = tab_ref[:, :LANES]
    stab = tab_ref[:, LANES:]
    ckv = _rms(z[:, ZC_CKV:ZC_KR], gkv_ref[...])
    kr = z[:, ZC_KR:ZC_KRS] * ctab + z[:, ZC_KRS:ZC_XR] * stab
    ckv_ref[0] = ckv
    kr_ref[0] = kr[:, ROPE_OFF:ROPE_OFF + MLA_ROPE]
    kcat_ref[0, :, :LANES] = ckv.astype(BF16)
    lane = lax.broadcasted_iota(jnp.int32, kr.shape, 1)
    kcat_ref[0, :, LANES:] = jnp.where(lane < ROPE_OFF, 1.0, kr).astype(BF16)
    xr_ref[0] = z[:, ZC_XR:ZC_XG]
    xg_ref[0] = z[:, ZC_XG:ZC_END]
    return xn, z, ctab, stab


def _proj_rows_kernel(x_ref, tab_ref, gmix_ref, win_ref, gkv_ref, gq_ref, wuq_ref, wuk_ref,
                      kcat_ref, ckv_ref, kr_ref, xr_ref, xg_ref, q_ref):
    _, z, ctab, stab = _proj_common(x_ref, tab_ref, gmix_ref, win_ref, gkv_ref,
                                    kcat_ref, ckv_ref, kr_ref, xr_ref, xg_ref)
    cqn = _rms(z[:, ZC_CQ:ZC_CKV], gq_ref[...]).astype(BF16)
    q = jnp.dot(cqn, wuq_ref[...], preferred_element_type=F32)
    qlat = jnp.dot(q[:, QC_NOPE:QC_ROPE].astype(BF16), wuk_ref[...], preferred_element_type=F32)
    for h in range(MLA_HEADS):
        q_ref[0, h, :, :LANES] = (qlat[:, h * LANES:(h + 1) * LANES] * MLA_SCALE).astype(BF16)
        rp = (q[:, QC_ROPE + h * LANES:QC_ROPE + (h + 1) * LANES] * ctab
              + q[:, QC_ROPES + h * LANES:QC_ROPES + (h + 1) * LANES] * stab)
        q_ref[0, h, :, LANES:] = (rp * MLA_SCALE).astype(BF16)


def _proj_cols_kernel(x_ref, tab_ref, gmix_ref, win_ref, gkv_ref, tabt_ref, wint_ref, gqc_ref, gkvc_ref,
                      wuqt_ref, wukt_ref, kcat_ref, ckv_ref, kr_ref, xr_ref, xg_ref, qt_ref, vt_ref):
    xn, _, _, _ = _proj_common(x_ref, tab_ref, gmix_ref, win_ref, gkv_ref,
                               kcat_ref, ckv_ref, kr_ref, xr_ref, xg_ref)
    tm = xn.shape[0]
    zt = lax.dot_general(wint_ref[...], xn, _NT, preferred_element_type=F32)
    ckvt = _rms_cols(zt[MLA_QRANK:], gkvc_ref[...])
    vt_ref[0, :F_ONES, :] = ckvt.astype(BF16)
    vt_ref[0, F_ONES:, :] = jnp.ones((V_ROWS - F_ONES, tm), BF16)
    cqnt = _rms_cols(zt[:MLA_QRANK], gqc_ref[...]).astype(BF16)
    qt = jnp.dot(wuqt_ref[...], cqnt, preferred_element_type=F32)
    qlatt = jnp.dot(wukt_ref[...], qt[QR_NOPE:QR_ROPE].astype(BF16), preferred_element_type=F32)
    ctabt = tabt_ref[:MLA_ROPE, :]
    stabt = tabt_ref[MLA_ROPE:, :]
    for h in range(MLA_HEADS):
        qt_ref[0, h, :F_ONES, :] = (qlatt[h * MLA_KVRANK:(h + 1) * MLA_KVRANK] * MLA_SCALE).astype(BF16)
        qt_ref[0, h, F_ONES:F_ROPE, :] = jnp.zeros((F_ROPE - F_ONES, tm), BF16)
        rp = (qt[QR_ROPE + h * MLA_ROPE:QR_ROPE + (h + 1) * MLA_ROPE] * ctabt
              + qt[QR_ROPES + h * MLA_ROPE:QR_ROPES + (h + 1) * MLA_ROPE] * stabt)
        qt_ref[0, h, F_ROPE:F_ROPE + MLA_ROPE, :] = (rp * MLA_SCALE).astype(BF16)
        qt_ref[0, h, F_ROPE + MLA_ROPE:, :] = jnp.zeros((QK_PAD - F_ROPE - MLA_ROPE, tm), BF16)


def _proj(x, tab, common, extra, tm, cols):
    b, s, d = x.shape
    row = lambda w: pl.BlockSpec((1, tm, w), lambda bi, i: (bi, i, 0))
    shape = lambda *dims: jax.ShapeDtypeStruct((b,) + dims, F32)
    out_specs = [row(QK_PAD), row(MLA_KVRANK), row(MLA_ROPE), row(LRU_WIDTH), row(LRU_WIDTH)]
    out_shape = [jax.ShapeDtypeStruct((b, s, QK_PAD), BF16), shape(s, MLA_KVRANK), shape(s, MLA_ROPE),
                 shape(s, LRU_WIDTH), shape(s, LRU_WIDTH)]
    in_specs = [row(d), pl.BlockSpec((tm, 2 * LANES), lambda bi, i: (i, 0))] + [_full(a.shape) for a in common]
    if cols:
        tabt, rest = extra[0], extra[1:]
        in_specs += [pl.BlockSpec((2 * MLA_ROPE, tm), lambda bi, i: (0, i))] + [_full(a.shape) for a in rest]
        out_specs += [pl.BlockSpec((1, MLA_HEADS, QK_PAD, tm), lambda bi, i: (bi, 0, 0, i)),
                      pl.BlockSpec((1, V_ROWS, tm), lambda bi, i: (bi, 0, i))]
        out_shape += [jax.ShapeDtypeStruct((b, MLA_HEADS, QK_PAD, s), BF16),
                      jax.ShapeDtypeStruct((b, V_ROWS, s), BF16)]
    else:
        in_specs += [_full(a.shape) for a in extra]
        out_specs += [pl.BlockSpec((1, MLA_HEADS, tm, QK_PAD), lambda bi, i: (bi, 0, i, 0))]
        out_shape += [jax.ShapeDtypeStruct((b, MLA_HEADS, s, QK_PAD), BF16)]
    return pl.pallas_call(
        _proj_cols_kernel if cols else _proj_rows_kernel,
        grid=(b, s // tm),
        in_specs=in_specs,
        out_specs=tuple(out_specs),
        out_shape=tuple(out_shape),
        compiler_params=_params("parallel", "parallel"),
        name="proj_cols" if cols else "proj_rows",
    )(x, tab, *common, *extra)


def _attn_prompt_kernel(qt_ref, k_ref, vt_ref, wuvt_ref, g_ref, o_ref,
                        m_sc, acc_sc, st_sc, cmax_sc, *, qb):
    i = pl.program_id(1)
    m_sc[...] = jnp.full(m_sc.shape, -jnp.inf, F32)
    acc_sc[...] = jnp.zeros(acc_sc.shape, F32)

    def step(j, masked):
        ks = pl.ds(pl.multiple_of(j * qb, qb), qb)

        def scores(h):
            st = jnp.dot(k_ref[0, ks, :], qt_ref[0, h], preferred_element_type=F32)
            if masked:
                kc = lax.broadcasted_iota(jnp.int32, st.shape, 0) // CHUNK
                qc = lax.broadcasted_iota(jnp.int32, st.shape, 1) // CHUNK
                st = jnp.where(kc <= qc, st, -jnp.inf)
            st_sc[h] = st
            cmax_sc[h:h + 1, :] = jnp.max(st, axis=0, keepdims=True)

        for h in range(SCORE_LEAD):
            scores(h)
        for h in range(MLA_HEADS):
            if h + SCORE_LEAD < MLA_HEADS:
                scores(h + SCORE_LEAD)
            m_prev = m_sc[h:h + 1, :]
            m_new = jnp.maximum(m_prev, cmax_sc[h:h + 1, :])
            p = jnp.exp(st_sc[h] - m_new).astype(BF16)
            m_sc[h:h + 1, :] = m_new
            acc_sc[h] = (jnp.exp(m_prev - m_new) * acc_sc[h]
                         + jnp.dot(vt_ref[0, :, ks], p, preferred_element_type=F32))

    def body(j, carry):
        step(j, False)
        return carry

    lax.fori_loop(0, i, body, 0)
    step(i, True)
    parts = []
    for h in range(MLA_HEADS):
        acc = acc_sc[h]
        att = (acc[:F_ONES] / acc[F_ONES:F_ONES + 1]).astype(BF16)
        parts.append(jnp.dot(wuvt_ref[h], att, preferred_element_type=F32))
    mla = jnp.concatenate(parts, axis=0).T
    o_ref[0] = _rms(mla, g_ref[...]).astype(BF16)


def _attn_prompt(qt, kcat, vt, wuvt, g, qb):
    b, _, _, s = qt.shape
    return pl.pallas_call(
        functools.partial(_attn_prompt_kernel, qb=qb),
        grid=(b, s // qb),
        in_specs=[pl.BlockSpec((1, MLA_HEADS, QK_PAD, qb), lambda bi, i: (bi, 0, 0, i)),
                  pl.BlockSpec((1, s, QK_PAD), lambda bi, i: (bi, 0, 0)),
                  pl.BlockSpec((1, V_ROWS, s), lambda bi, i: (bi, 0, 0)),
                  _full(wuvt.shape), _full(g.shape)],
        out_specs=pl.BlockSpec((1, qb, MLA_WIDTH), lambda bi, i: (bi, i, 0)),
        out_shape=jax.ShapeDtypeStruct((b, s, MLA_WIDTH), BF16),
        scratch_shapes=[pltpu.VMEM((MLA_HEADS, qb), F32), pltpu.VMEM((MLA_HEADS, V_ROWS, qb), F32),
                        pltpu.VMEM((MLA_HEADS, qb, qb), F32), pltpu.VMEM((MLA_HEADS, qb), F32)],
        compiler_params=_params("parallel", "arbitrary"),
        name="attn_prompt",
    )(qt, kcat, vt, wuvt, g)


def _attn_sample_kernel(q_ref, pckv_ref, pkr_ref, k_ref, wuv_ref, g_ref, o_ref, *, sd, past):
    rows = MLA_HEADS * sd
    q = q_ref[0].reshape(rows, QK_PAD)
    knew = k_ref[0]
    s_past = (lax.dot_general(q[:, :LANES], pckv_ref[0].astype(BF16), _NT, preferred_element_type=F32)
              + lax.dot_general(q[:, F_ROPE:F_ROPE + MLA_ROPE], pkr_ref[0].astype(BF16), _NT,
                                preferred_element_type=F32))
    s_new = lax.dot_general(q, knew, _NT, preferred_element_type=F32)

    def mask(s, k0):
        qc = (past + lax.broadcasted_iota(jnp.int32, s.shape, 0) % sd) // CHUNK
        kc = (k0 + lax.broadcasted_iota(jnp.int32, s.shape, 1)) // CHUNK
        return jnp.where(kc <= qc, s, -jnp.inf)

    s_past = mask(s_past, 0)
    s_new = mask(s_new, past)
    m = jnp.maximum(jnp.max(s_past, axis=-1, keepdims=True), jnp.max(s_new, axis=-1, keepdims=True))
    p_past = jnp.exp(s_past - m).astype(BF16)
    p_new = jnp.exp(s_new - m).astype(BF16)
    acc = jnp.dot(p_new, knew, preferred_element_type=F32)
    num = acc[:, :LANES] + jnp.dot(p_past, pckv_ref[0].astype(BF16), preferred_element_type=F32)
    den = acc[:, F_ONES:F_ONES + 1] + jnp.sum(p_past.astype(F32), axis=-1, keepdims=True)
    att = (num / den).astype(BF16)
    mla = jnp.zeros((sd, MLA_WIDTH), F32)
    for h in range(MLA_HEADS):
        mla = mla + jnp.dot(att[h * sd:(h + 1) * sd], wuv_ref[h], preferred_element_type=F32)
    o_ref[0] = _rms(mla, g_ref[...]).astype(BF16)


def _attn_sample(q, pckv, pkr, kcat, wuv, g, nb, sd):
    past = pckv.shape[1]
    return pl.pallas_call(
        functools.partial(_attn_sample_kernel, sd=sd, past=past),
        grid=(nb,),
        in_specs=[pl.BlockSpec((1, MLA_HEADS, sd, QK_PAD), lambda b: (0, 0, b, 0)),
                  pl.BlockSpec((1, past, MLA_KVRANK), lambda b: (b, 0, 0)),
                  pl.BlockSpec((1, past, MLA_ROPE), lambda b: (b, 0, 0)),
                  pl.BlockSpec((1, sd, QK_PAD), lambda b: (0, b, 0)),
                  _full(wuv.shape), _full(g.shape)],
        out_specs=pl.BlockSpec((1, sd, MLA_WIDTH), lambda b: (0, b, 0)),
        out_shape=jax.ShapeDtypeStruct((1, nb * sd, MLA_WIDTH), BF16),
        compiler_params=_params("parallel"),
        name="attn_sample",
    )(q, pckv, pkr, kcat, wuv, g)


def _lru_kernel(xr_ref, xg_ref, buf0_ref, h0_ref, cw_ref, cb_ref, wa_ref, ba_ref, wi_ref, bi_ref,
                lam_ref, g_ref, o_ref, hlast_ref, newbuf_ref, xp_sc, h_sc, a_sc, b_sc, hh_sc, *, ts):
    t = pl.program_id(1)
    nt = pl.num_programs(1)
    pad = SUBLANES

    @pl.when(t == 0)
    def _():
        xp_sc[pad - (CONV_WIDTH - 1):pad, :] = buf0_ref[0]
        h_sc[...] = h0_ref[0]

    xr = xr_ref[0]
    xp_sc[pad:pad + ts, :] = xr
    xc = cb_ref[...] + cw_ref[CONV_WIDTH - 1:CONV_WIDTH, :] * xr
    for k in range(CONV_WIDTH - 1):
        xc = xc + cw_ref[k:k + 1, :] * xp_sc[pad - (CONV_WIDTH - 1) + k:pad - (CONV_WIDTH - 1) + k + ts, :]
    tail = xp_sc[ts + pad - (CONV_WIDTH - 1):ts + pad, :]
    xp_sc[pad - (CONV_WIDTH - 1):pad, :] = tail

    xcb = xc.astype(BF16)
    r = jax.nn.sigmoid(jnp.dot(xcb, wa_ref[...], preferred_element_type=F32) + ba_ref[...])
    ig = jax.nn.sigmoid(jnp.dot(xcb, wi_ref[...], preferred_element_type=F32) + bi_ref[...])
    log_a = -LRU_C * r * jax.nn.softplus(-lam_ref[...])
    a = jnp.exp(log_a)
    bt = jnp.sqrt(-_expm1(2.0 * log_a)) * (ig * xc)

    ng = ts // SUBLANES
    a3 = a.reshape(ng, SUBLANES, LRU_WIDTH)
    b3 = bt.reshape(ng, SUBLANES, LRU_WIDTH)
    row = lax.broadcasted_iota(jnp.int32, a3.shape, 1)
    for d in (1, 2, 4):
        valid = row >= d
        a_sh = pltpu.roll(a3, d, axis=1)
        b_sh = pltpu.roll(b3, d, axis=1)
        b3 = jnp.where(valid, a3 * b_sh + b3, b3)
        a3 = jnp.where(valid, a3 * a_sh, a3)
    a_sc[...] = a3.reshape(ts, LRU_WIDTH)
    b_sc[...] = b3.reshape(ts, LRU_WIDTH)

    def group(gi, hprev):
        rs = pl.ds(pl.multiple_of(gi * SUBLANES, SUBLANES), SUBLANES)
        hg = a_sc[rs, :] * hprev + b_sc[rs, :]
        hh_sc[rs, :] = hg
        return hg[SUBLANES - 1:SUBLANES, :]

    hfin = lax.fori_loop(0, ng, group, h_sc[...])
    h_sc[...] = hfin

    lru_out = hh_sc[...] * _gelu(xg_ref[0])
    o_ref[0] = _rms(lru_out, g_ref[...]).astype(BF16)

    @pl.when(t == nt - 1)
    def _():
        hlast_ref[0] = hfin
        newbuf_ref[0] = tail


def _lru(xr, xg, buf0, h0, cw, cb, wa, ba, wi, bi, lam, g, ts):
    b, s, w = xr.shape
    row = pl.BlockSpec((1, ts, w), lambda bi_, t: (bi_, t, 0))
    return pl.pallas_call(
        functools.partial(_lru_kernel, ts=ts),
        grid=(b, s // ts),
        in_specs=[row, row,
                  pl.BlockSpec((1, CONV_WIDTH - 1, w), lambda bi_, t: (bi_, 0, 0)),
                  pl.BlockSpec((1, 1, w), lambda bi_, t: (bi_, 0, 0)),
                  _full(cw.shape), _full(cb.shape), _full(wa.shape), _full(ba.shape),
                  _full(wi.shape), _full(bi.shape), _full(lam.shape), _full(g.shape)],
        out_specs=(row,
                   pl.BlockSpec((1, 1, w), lambda bi_, t: (bi_, 0, 0)),
                   pl.BlockSpec((1, CONV_WIDTH - 1, w), lambda bi_, t: (bi_, 0, 0))),
        out_shape=(jax.ShapeDtypeStruct((b, s, w), BF16),
                   jax.ShapeDtypeStruct((b, 1, w), F32),
                   jax.ShapeDtypeStruct((b, CONV_WIDTH - 1, w), F32)),
        scratch_shapes=[pltpu.VMEM((ts + SUBLANES, w), F32), pltpu.VMEM((1, w), F32),
                        pltpu.VMEM((ts, w), F32), pltpu.VMEM((ts, w), F32), pltpu.VMEM((ts, w), F32)],
        compiler_params=_params("parallel", "arbitrary"),
        name="lru",
    )(xr, xg, buf0, h0, cw, cb, wa, ba, wi, bi, lam, g)


def _oddeven_pairs(n):
    pairs = []

    def merge(lo, m, r):
        step = r * 2
        if step < m:
            merge(lo, m, step)
            merge(lo + r, m, step)
            pairs.extend((i, i + r) for i in range(lo + r, lo + m - r, step))
        else:
            pairs.append((lo, lo + r))

    def sort(lo, m):
        if m > 1:
            sort(lo, m // 2)
            sort(lo + m // 2, m // 2)
            merge(lo, m, 1)

    sort(0, n)
    return pairs


_SORT16 = _oddeven_pairs(16)


def _top_values(s, n):
    t = s.shape[0] // SUBLANES
    assert t <= 16 and s.shape[0] % SUBLANES == 0
    v = [s[i * SUBLANES:(i + 1) * SUBLANES] for i in range(t)]
    for i, j in _SORT16:
        if j < t:
            v[i], v[j] = jnp.maximum(v[i], v[j]), jnp.minimum(v[i], v[j])
    top = []
    for r in range(n):
        m = jnp.max(v[0], axis=0, keepdims=True)
        top.append(m)
        hit = v[0] == m
        for i in range(min(n - r - 1, t)):
            v[i] = jnp.where(hit, v[i + 1] if i + 1 < t else -jnp.inf, v[i])
    return top


def _mix_kernel(x_ref, mla_ref, lru_ref, wo_ref, gffn_ref, wqt_ref, k1_ref, k2_ref,
                x1_ref, xn2_ref, s2_ref, e2_ref, theta_ref, e1_ref, s1_sc, s2_sc):
    x1 = (x_ref[...]
          + jnp.dot(mla_ref[...], wo_ref[:MLA_WIDTH, :], preferred_element_type=F32)
          + jnp.dot(lru_ref[...], wo_ref[MLA_WIDTH:, :], preferred_element_type=F32))
    x1_ref[...] = x1
    xn2t = _rms(x1, gffn_ref[...]).T.astype(BF16)
    xn2_ref[...] = xn2t
    qt = jnp.dot(wqt_ref[...], xn2t, preferred_element_type=F32).astype(BF16)
    for h in range(PEER_HEADS):
        base = h * PEER_KEY_DIM
        s1_sc[h] = jnp.dot(k1_ref[h], qt[base:base + PEER_HALF], preferred_element_type=F32)
        s2_sc[h] = jnp.dot(k2_ref[h], qt[base + PEER_HALF:base + PEER_KEY_DIM], preferred_element_type=F32)
    k = PEER_TOPK
    groups = x1.shape[0] // LANES

    def unit(h, lane_group):
        ls = pl.ds(pl.multiple_of(lane_group * LANES, LANES), LANES)
        s1 = s1_sc[h, :, ls]
        s2 = s2_sc[h, :, ls]
        v1 = _top_values(s1, k)
        v2 = _top_values(s2, k)
        v1a = jnp.concatenate(v1, axis=0)
        cand = [v1a + v2[0]]
        cand += [v1a[:SUBLANES] + v2[j] for j in range(1, SUBLANES)]
        cand += [v1[0] + jnp.concatenate(v2[SUBLANES:], axis=0)]
        vs = _top_values(jnp.concatenate(cand, axis=0), k)
        tau = vs[k - 1]
        sel = [c >= tau for c in cand]
        z = sum(jnp.sum(jnp.where(m, jnp.exp(c - vs[0]), 0.0), axis=0, keepdims=True)
                for m, c in zip(sel, cand))
        ones = [jnp.where(m, 1.0, 0.0) for m in sel]
        low = sum(ones[1:SUBLANES])
        cnt = ones[0] + jnp.concatenate([low, jnp.zeros_like(low)], axis=0)
        tail = jnp.sum(ones[SUBLANES], axis=0, keepdims=True)
        cnt = cnt + jnp.where(lax.broadcasted_iota(jnp.int32, cnt.shape, 0) == 0, tail, 0.0)
        th_row = jnp.full(cnt.shape, jnp.inf, F32)
        for j in range(k):
            th_row = jnp.where(cnt == float(j + 1), v2[j], th_row)
        theta = jnp.full(s1.shape, jnp.inf, F32)
        for i in range(k):
            theta = jnp.where(s1 == v1[i], th_row[i:i + 1], theta)
        s2_ref[h, :, ls] = s2
        e2_ref[h, :, ls] = jnp.exp(s2 - v2[0]) / z
        theta_ref[h, :, ls] = theta
        e1_ref[h, :, ls] = jnp.exp(s1 - v1[0])

    per = 2 if groups % 2 == 0 else 1

    def body(it, carry):
        for u in range(per):
            unit(it // (groups // per), (it % (groups // per)) * per + u)
        return carry

    lax.fori_loop(0, PEER_HEADS * groups // per, body, 0)


def _mix(x, mla, lru, wo, gffn, wqt, k1, k2, tm):
    t, d = x.shape
    row = lambda w: pl.BlockSpec((tm, w), lambda i: (i, 0))
    sc = pl.BlockSpec((PEER_HEADS, PEER_NKEYS, tm), lambda i: (0, 0, i))
    sc_shape = lambda dt: jax.ShapeDtypeStruct((PEER_HEADS, PEER_NKEYS, t), dt)
    return pl.pallas_call(
        _mix_kernel,
        grid=(t // tm,),
        in_specs=[row(d), row(MLA_WIDTH), row(LRU_WIDTH), _full(wo.shape), _full(gffn.shape),
                  _full(wqt.shape), _full(k1.shape), _full(k2.shape)],
        out_specs=(row(d), pl.BlockSpec((d, tm), lambda i: (0, i)), sc, sc, sc, sc),
        out_shape=(jax.ShapeDtypeStruct((t, d), F32), jax.ShapeDtypeStruct((d, t), BF16),
                   sc_shape(F32), sc_shape(F32), sc_shape(F32), sc_shape(F32)),
        scratch_shapes=[pltpu.VMEM((PEER_HEADS, PEER_NKEYS, tm), F32)] * 2,
        compiler_params=_params("parallel"),
        name="mix",
    )(x, mla, lru, wo, gffn, wqt, k1, k2)


def _peer_kernel(xn2_ref, x1_ref, u_ref, vt_ref, s2_ref, e2_ref, theta_ref, e1_ref, gfin_ref,
                 y_ref, ht0_sc, ht1_sc, wt0_sc, wt1_sc, yt_sc, *, eb, tb, ne, final_norm):
    g = pl.program_id(0)
    out_tile = jnp.maximum(g - 2, 0) % ne

    @pl.when(g == 0)
    def _():
        for ref in (ht0_sc, ht1_sc, wt0_sc, wt1_sc):
            ref[...] = jnp.zeros(ref.shape, ref.dtype)

    @pl.when(out_tile == 0)
    def _():
        yt_sc[...] = jnp.zeros(yt_sc.shape, F32)

    na = eb // PEER_NKEYS
    mh_rows = eb // 2
    nt_cols = 2 * LANES
    bq_rows = PEER_NKEYS // 4

    def stages(ht_cur, wt_cur, ht_prev, wt_prev):
        def pre_piece(mh, nt):
            ms = slice(mh * mh_rows, (mh + 1) * mh_rows)
            ns = slice(nt * nt_cols, (nt + 1) * nt_cols)
            ht_cur[ms, ns] = jnp.dot(u_ref[ms, :], xn2_ref[:, ns], preferred_element_type=F32)

        def out_piece(mh, nt):
            out_rows = yt_sc.shape[0] // (eb // mh_rows)
            ms = slice(mh * out_rows, (mh + 1) * out_rows)
            ns = slice(nt * nt_cols, (nt + 1) * nt_cols)
            yt_sc[ms, ns] += jnp.dot(vt_ref[ms, :], wt_cur[:, ns], preferred_element_type=F32)

        def mix_block(tl, bq):
            ls = slice(tl * LANES, (tl + 1) * LANES)
            bs = slice(bq * bq_rows, (bq + 1) * bq_rows)
            acc = [jnp.zeros((bq_rows, LANES), F32)] * na
            for h in range(PEER_HEADS):
                s2 = s2_ref[h, bs, ls]
                e2 = e2_ref[h, bs, ls]
                for al in range(na):
                    hit = s2 >= theta_ref[h, al:al + 1, ls]
                    acc[al] = acc[al] + jnp.where(hit, e1_ref[h, al:al + 1, ls] * e2, 0.0)
            for al in range(na):
                rs = slice(al * PEER_NKEYS + bq * bq_rows, al * PEER_NKEYS + (bq + 1) * bq_rows)
                wt_prev[rs, ls] = (acc[al] * _gelu(ht_prev[rs, ls])).astype(BF16)

        pieces = [(f, mh, nt) for nt in range(tb // nt_cols) for mh in range(eb // mh_rows)
                  for f in (pre_piece, out_piece)]
        blocks = [(tl, bq) for tl in range(tb // LANES) for bq in range(PEER_NKEYS // bq_rows)]
        per = -(-len(blocks) // len(pieces))
        for i, (f, mh, nt) in enumerate(pieces):
            f(mh, nt)
            for tl, bq in blocks[i * per:(i + 1) * per]:
                mix_block(tl, bq)

    @pl.when(g % 2 == 0)
    def _():
        stages(ht0_sc, wt0_sc, ht1_sc, wt1_sc)

    @pl.when(g % 2 == 1)
    def _():
        stages(ht1_sc, wt1_sc, ht0_sc, wt0_sc)

    @pl.when((g >= 2) & (out_tile == ne - 1))
    def _():
        y = x1_ref[...] + yt_sc[...].T
        if final_norm:
            y = _rms(y, gfin_ref[...])
        y_ref[...] = y


def _peer(xn2, x1, u, vt, s2, e2, theta, e1, gfin, tb, eb, final_norm):
    t, d = x1.shape
    ne = u.shape[0] // eb
    na = eb // PEER_NKEYS
    steps = (t // tb) * ne
    assert na == SUBLANES and u.shape[0] % eb == 0 and tb % (2 * LANES) == 0 and ne > 1

    def at(lag):
        def pos(g):
            s = jnp.clip(g - lag, 0, steps - 1)
            return s // ne, s % ne
        return pos

    pre, mid, out = at(0), at(1), at(2)
    sc = pl.BlockSpec((PEER_HEADS, PEER_NKEYS, tb), lambda g: (0, 0, mid(g)[0]))
    rows = pl.BlockSpec((PEER_HEADS, na, tb), lambda g: (0, mid(g)[1], mid(g)[0]))
    return pl.pallas_call(
        functools.partial(_peer_kernel, eb=eb, tb=tb, ne=ne, final_norm=final_norm),
        grid=(steps + 2,),
        in_specs=[pl.BlockSpec((d, tb), lambda g: (0, pre(g)[0])),
                  pl.BlockSpec((tb, d), lambda g: (out(g)[0], 0)),
                  pl.BlockSpec((eb, d), lambda g: (pre(g)[1], 0)),
                  pl.BlockSpec((d, eb), lambda g: (0, out(g)[1])),
                  sc, sc, rows, rows,
                  _full(gfin.shape)],
        out_specs=pl.BlockSpec((tb, d), lambda g: (out(g)[0], 0)),
        out_shape=jax.ShapeDtypeStruct((t, d), F32),
        scratch_shapes=[pltpu.VMEM((eb, tb), F32), pltpu.VMEM((eb, tb), F32),
                        pltpu.VMEM((eb, tb), BF16), pltpu.VMEM((eb, tb), BF16), pltpu.VMEM((d, tb), F32)],
        compiler_params=_params("arbitrary"),
        name="peer",
    )(xn2, x1, u, vt, s2, e2, theta, e1, gfin)


def _rope_tables(pos):
    half = MLA_ROPE // 2
    inv = ROPE_THETA ** (-jnp.arange(half, dtype=F32) / half)
    ang = pos.astype(F32)[:, None] * inv[None, :]
    cos, sin = jnp.cos(ang), jnp.sin(ang)
    c2 = jnp.concatenate([cos, cos], axis=-1)
    s2 = jnp.concatenate([-sin, sin], axis=-1)
    rows = jnp.concatenate([_place_cols(c2), _place_cols(s2)], axis=-1)
    return rows, jnp.concatenate([c2, s2], axis=-1).T


def _swap_halves(w):
    half = w.shape[-1] // 2
    return jnp.concatenate([w[..., half:], w[..., :half]], axis=-1)


def _place_cols(w):
    pad = [(0, 0)] * (w.ndim - 1) + [(ROPE_OFF, LANES - ROPE_OFF - w.shape[-1])]
    return jnp.pad(w, pad)


def _layer_weights(w_in, w_uq, w_uk, w_uv, lru_wa, lru_wi, w_out, peer_wq, peer_keys1, peer_keys2,
                   peer_u, peer_v):
    o1 = MLA_QRANK
    o2 = o1 + MLA_KVRANK
    o3 = o2 + MLA_ROPE
    kr = w_in[:, o2:o3]
    win = jnp.concatenate([w_in[:, :o2], _place_cols(kr), _place_cols(_swap_halves(kr)),
                           w_in[:, o3:]], axis=1).astype(BF16)
    wint = w_in[:, :o2].T.astype(BF16)
    wq3 = w_uq.reshape(MLA_QRANK, MLA_HEADS, MLA_NOPE + MLA_ROPE)
    nope = wq3[:, :, :MLA_NOPE].reshape(MLA_QRANK, MLA_HEADS * MLA_NOPE)
    rp = wq3[:, :, MLA_NOPE:]
    flat = lambda w: w.reshape(MLA_QRANK, -1)
    wuq = jnp.concatenate([nope, flat(_place_cols(rp)), flat(_place_cols(_swap_halves(rp)))], axis=1).astype(BF16)
    wuqt = jnp.concatenate([nope, flat(rp), flat(_swap_halves(rp))], axis=1).T.astype(BF16)
    eye_h = jnp.eye(MLA_HEADS, dtype=F32)
    wuk = jnp.einsum("rhd,hg->hdgr", w_uk, eye_h).reshape(MLA_HEADS * MLA_NOPE, MLA_HEADS * MLA_KVRANK)
    wuv = jnp.einsum("rhd,hg->hrgd", w_uv, eye_h).reshape(MLA_HEADS, MLA_KVRANK, MLA_WIDTH)
    wuvt = jnp.transpose(w_uv, (1, 2, 0))
    eye_b = jnp.eye(LRU_BLOCKS, dtype=F32)
    bd = lambda w: jnp.einsum("nde,nm->ndme", w, eye_b).reshape(LRU_WIDTH, LRU_WIDTH)
    return dict(win=win, wint=wint, wuq=wuq, wuqt=wuqt, wuk=wuk.astype(BF16), wukt=wuk.T.astype(BF16),
                wuv=wuv.astype(BF16), wuvt=wuvt.astype(BF16),
                wa=bd(lru_wa).astype(BF16), wi=bd(lru_wi).astype(BF16), wo=w_out.astype(BF16),
                wqt=peer_wq.T.astype(BF16), k1=peer_keys1.astype(BF16), k2=peer_keys2.astype(BF16),
                u=peer_u.astype(BF16), vt=peer_v.T.astype(BF16))


def _col(v):
    return v.reshape(-1, 1).astype(F32)


def _row(v):
    return v.reshape(1, -1).astype(F32)


def _token_tile(n, cap):
    t = min(n, cap)
    assert n % t == 0, (n, t)
    return t


def kernel(x_prompt, x_sample, cache_mla_ckv, cache_mla_krope, state_lru_h, state_lru_conv, norm_mix, w_in, norm_q, w_uq, norm_kv, w_uk, w_uv, conv_w, conv_b, lru_wa, lru_ba, lru_wi, lru_bi, lru_lambda, norm_mla_out, norm_lru_out, w_out, norm_ffn, peer_wq, peer_keys1, peer_keys2, peer_u, peer_v, norm_final):
    bp, sp, d = x_prompt.shape
    bs, sd, _ = x_sample.shape
    depth = w_in.shape[0]
    past = cache_mla_ckv.shape[2]
    ts_tok = bs * sd
    tab_p, tabt_p = _rope_tables(jnp.arange(sp))
    tab_s = jnp.tile(_rope_tables(past + jnp.arange(sd))[0], (bs, 1))
    gfin = _row(norm_final)

    xp = x_prompt
    xs = x_sample.reshape(1, ts_tok, d)
    outs = [[] for _ in range(8)]
    for l in range(depth):
        w = _layer_weights(w_in[l], w_uq[l], w_uk[l], w_uv[l], lru_wa[l], lru_wi[l], w_out[l], peer_wq[l],
                           peer_keys1[l], peer_keys2[l], peer_u[l], peer_v[l])
        last = l == depth - 1
        lru_args = (conv_w[l].astype(F32), _row(conv_b[l]), w["wa"], _row(lru_ba[l]), w["wi"], _row(lru_bi[l]),
                    _row(lru_lambda[l]), _row(norm_lru_out[l]))
        proj_common = (_row(norm_mix[l]), w["win"], _row(norm_kv[l]))
        proj_rows = (_row(norm_q[l]), w["wuq"], w["wuk"])
        proj_cols = (tabt_p, w["wint"], _col(norm_q[l]), _col(norm_kv[l]), w["wuqt"], w["wukt"])
        gmla = _row(norm_mla_out[l])

        def tail(x2d, mla, lru):
            t = x2d.shape[0]
            tm = _token_tile(t, 512)
            x1, xn2, s2, e2, theta, e1 = _mix(x2d, mla, lru, w["wo"], _row(norm_ffn[l]), w["wqt"],
                                               w["k1"], w["k2"], tm)
            return _peer(xn2, x1, w["u"], w["vt"], s2, e2, theta, e1, gfin, tm, PEER_TILE, last)

        kcat, ckv, kr, xr, xg, qt, vt = _proj(xp, tab_p, proj_common, proj_cols, _token_tile(sp, 512), True)
        mla = _attn_prompt(qt, kcat, vt, w["wuvt"], gmla, _token_tile(sp, 512))
        lru, hl, nb = _lru(xr, xg, jnp.zeros((bp, CONV_WIDTH - 1, LRU_WIDTH), F32),
                           jnp.zeros((bp, 1, LRU_WIDTH), F32), *lru_args, _token_tile(sp, 512))
        xp = tail(xp.reshape(bp * sp, d), mla.reshape(bp * sp, MLA_WIDTH),
                  lru.reshape(bp * sp, LRU_WIDTH)).reshape(bp, sp, d)
        for lst, v in zip(outs[:4], (ckv, kr, hl[:, 0], nb)):
            lst.append(v)

        kcat, ckv, kr, xr, xg, q = _proj(xs, tab_s, proj_common, proj_rows, _token_tile(ts_tok, 512), False)
        mla = _attn_sample(q, cache_mla_ckv[l], cache_mla_krope[l], kcat, w["wuv"], gmla, bs, sd)
        lru, hl, nb = _lru(xr.reshape(bs, sd, LRU_WIDTH), xg.reshape(bs, sd, LRU_WIDTH),
                           state_lru_conv[l].astype(F32), state_lru_h[l].reshape(bs, 1, LRU_WIDTH).astype(F32),
                           *lru_args, sd)
        xs = tail(xs[0], mla[0], lru.reshape(ts_tok, LRU_WIDTH)).reshape(1, ts_tok, d)
        for lst, v in zip(outs[4:], (ckv.reshape(bs, sd, MLA_KVRANK), kr.reshape(bs, sd, MLA_ROPE), hl[:, 0], nb)):
            lst.append(v)

    return (xp, xs.reshape(bs, sd, d)) + tuple(jnp.stack(o) for o in outs)
```

```python
import functools

import jax
import jax.numpy as jnp
from jax import lax
from jax.experimental import pallas as pl
from jax.experimental.pallas import tpu as pltpu

F32 = jnp.float32
BF16 = jnp.bfloat16

LANES = 128
SUBLANES = 8
PACKED_ROWS = 2 * SUBLANES
CHUNK = 64
EPS = 1e-6
MLA_HEADS = 8
MLA_NOPE = 64
MLA_ROPE = 32
MLA_VDIM = 64
MLA_QRANK = 256
MLA_KVRANK = 128
ROPE_THETA = 10000.0
MLA_WIDTH = MLA_HEADS * MLA_VDIM
MLA_SCALE = (MLA_NOPE + MLA_ROPE) ** -0.5
QK_PAD = 256
LRU_WIDTH = 512
LRU_BLOCKS = 8
CONV_WIDTH = 4
LRU_C = 8.0
PEER_HEADS = 8
PEER_NKEYS = 128
PEER_KEY_DIM = 256
PEER_HALF = PEER_KEY_DIM // 2
PEER_TOPK = 16
TOKEN_TILE = 512
SCORE_LEAD = 2
PEER_TILE = 8 * PEER_NKEYS
VMEM_LIMIT = 56 * 1024 * 1024

F_ONES = MLA_KVRANK
F_ROPE = F_ONES + PACKED_ROWS
V_ROWS = F_ROPE
ROPE_OFF = F_ROPE - LANES
ZC_CQ = 0
ZC_CKV = ZC_CQ + MLA_QRANK
ZC_KR = ZC_CKV + MLA_KVRANK
ZC_KRS = ZC_KR + LANES
ZC_XR = ZC_KRS + LANES
ZC_XG = ZC_XR + LRU_WIDTH
ZC_END = ZC_XG + LRU_WIDTH
QC_NOPE = 0
QC_ROPE = QC_NOPE + MLA_HEADS * MLA_NOPE
QC_ROPES = QC_ROPE + MLA_HEADS * LANES
QR_NOPE = 0
QR_ROPE = QR_NOPE + MLA_HEADS * MLA_NOPE
QR_ROPES = QR_ROPE + MLA_HEADS * MLA_ROPE

_NT = (((1,), (1,)), ((), ()))


def _rms(x, g):
    return x * lax.rsqrt(jnp.mean(x * x, axis=-1, keepdims=True) + EPS) * g


def _rms_cols(xt, g):
    return xt * lax.rsqrt(jnp.mean(xt * xt, axis=0, keepdims=True) + EPS) * g


def _expm1(y):
    u = jnp.exp(y)
    um1 = u - 1.0
    return jnp.where(um1 == 0.0, y, jnp.where(um1 == -1.0, -1.0, um1 * y / jnp.log(u)))


def _gelu(x):
    return 0.5 * x * (1.0 + lax.erf(x * (2.0 ** -0.5)))


def _params(*sem):
    return pltpu.CompilerParams(dimension_semantics=sem, vmem_limit_bytes=VMEM_LIMIT)


def _full(shape):
    n = len(shape)
    return pl.BlockSpec(shape, lambda *_: (0,) * n)


def _proj_common(x_ref, tab_ref, gmix_ref, win_ref, gkv_ref, kcat_ref, ckv_ref, kr_ref, xr_ref, xg_ref):
    xn = _rms(x_ref[0], gmix_ref[...]).astype(BF16)
    z = jnp.dot(xn, win_ref[...], preferred_element_type=F32)
    ctab = tab_ref[:, :LANES]
    stab = tab_ref[:, LANES:]
    ckv = _rms(z[:, ZC_CKV:ZC_KR], gkv_ref[...])
    kr = z[:, ZC_KR:ZC_KRS] * ctab + z[:, ZC_KRS:ZC_XR] * stab
    ckv_ref[0] = ckv
    kr_ref[0] = kr[:, ROPE_OFF:ROPE_OFF + MLA_ROPE]
    kcat_ref[0, :, :LANES] = ckv.astype(BF16)
    lane = lax.broadcasted_iota(jnp.int32, kr.shape, 1)
    kcat_ref[0, :, LANES:] = jnp.where(lane < ROPE_OFF, 1.0, kr).astype(BF16)
    xr_ref[0] = z[:, ZC_XR:ZC_XG]
    xg_ref[0] = z[:, ZC_XG:ZC_END]
    return xn, z, ctab, stab


def _proj_rows_kernel(x_ref, tab_ref, gmix_ref, win_ref, gkv_ref, gq_ref, wuq_ref, wuk_ref,
                      kcat_ref, ckv_ref, kr_ref, xr_ref, xg_ref, q_ref):
    _, z, ctab, stab = _proj_common(x_ref, tab_ref, gmix_ref, win_ref, gkv_ref,
                                    kcat_ref, ckv_ref, kr_ref, xr_ref, xg_ref)
    cqn = _rms(z[:, ZC_CQ:ZC_CKV], gq_ref[...]).astype(BF16)
    q = jnp.dot(cqn, wuq_ref[...], preferred_element_type=F32)
    qlat = jnp.dot(q[:, QC_NOPE:QC_ROPE].astype(BF16), wuk_ref[...], preferred_element_type=F32)
    for h in range(MLA_HEADS):
        q_ref[0, h, :, :LANES] = (qlat[:, h * LANES:(h + 1) * LANES] * MLA_SCALE).astype(BF16)
        rp = (q[:, QC_ROPE + h * LANES:QC_ROPE + (h + 1) * LANES] * ctab
              + q[:, QC_ROPES + h * LANES:QC_ROPES + (h + 1) * LANES] * stab)
        q_ref[0, h, :, LANES:] = (rp * MLA_SCALE).astype(BF16)


def _proj_cols_kernel(x_ref, tab_ref, gmix_ref, win_ref, gkv_ref, tabt_ref, wint_ref, gqc_ref, gkvc_ref,
                      wuqt_ref, wukt_ref, kcat_ref, ckv_ref, kr_ref, xr_ref, xg_ref, qt_ref, vt_ref):
    xn, _, _, _ = _proj_common(x_ref, tab_ref, gmix_ref, win_ref, gkv_ref,
                               kcat_ref, ckv_ref, kr_ref, xr_ref, xg_ref)
    tm = xn.shape[0]
    zt = lax.dot_general(wint_ref[...], xn, _NT, preferred_element_type=F32)
    ckvt = _rms_cols(zt[MLA_QRANK:], gkvc_ref[...])
    vt_ref[0, :F_ONES, :] = ckvt.astype(BF16)
    vt_ref[0, F_ONES:, :] = jnp.ones((V_ROWS - F_ONES, tm), BF16)
    cqnt = _rms_cols(zt[:MLA_QRANK], gqc_ref[...]).astype(BF16)
    qt = jnp.dot(wuqt_ref[...], cqnt, preferred_element_type=F32)
    qlatt = jnp.dot(wukt_ref[...], qt[QR_NOPE:QR_ROPE].astype(BF16), preferred_element_type=F32)
    ctabt = tabt_ref[:MLA_ROPE, :]
    stabt = tabt_ref[MLA_ROPE:, :]
    for h in range(MLA_HEADS):
        qt_ref[0, h, :F_ONES, :] = (qlatt[h * MLA_KVRANK:(h + 1) * MLA_KVRANK] * MLA_SCALE).astype(BF16)
        qt_ref[0, h, F_ONES:F_ROPE, :] = jnp.zeros((F_ROPE - F_ONES, tm), BF16)
        rp = (qt[QR_ROPE + h * MLA_ROPE:QR_ROPE + (h + 1) * MLA_ROPE] * ctabt
              + qt[QR_ROPES + h * MLA_ROPE:QR_ROPES + (h + 1) * MLA_ROPE] * stabt)
        qt_ref[0, h, F_ROPE:F_ROPE + MLA_ROPE, :] = (rp * MLA_SCALE).astype(BF16)
        qt_ref[0, h, F_ROPE + MLA_ROPE:, :] = jnp.zeros((QK_PAD - F_ROPE - MLA_ROPE, tm), BF16)


def _proj(x, tab, common, extra, tm, cols):
    b, s, d = x.shape
    row = lambda w: pl.BlockSpec((1, tm, w), lambda bi, i: (bi, i, 0))
    shape = lambda *dims: jax.ShapeDtypeStruct((b,) + dims, F32)
    out_specs = [row(QK_PAD), row(MLA_KVRANK), row(MLA_ROPE), row(LRU_WIDTH), row(LRU_WIDTH)]
    out_shape = [jax.ShapeDtypeStruct((b, s, QK_PAD), BF16), shape(s, MLA_KVRANK), shape(s, MLA_ROPE),
                 shape(s, LRU_WIDTH), shape(s, LRU_WIDTH)]
    in_specs = [row(d), pl.BlockSpec((tm, 2 * LANES), lambda bi, i: (i, 0))] + [_full(a.shape) for a in common]
    if cols:
        tabt, rest = extra[0], extra[1:]
        in_specs += [pl.BlockSpec((2 * MLA_ROPE, tm), lambda bi, i: (0, i))] + [_full(a.shape) for a in rest]
        out_specs += [pl.BlockSpec((1, MLA_HEADS, QK_PAD, tm), lambda bi, i: (bi, 0, 0, i)),
                      pl.BlockSpec((1, V_ROWS, tm), lambda bi, i: (bi, 0, i))]
        out_shape += [jax.ShapeDtypeStruct((b, MLA_HEADS, QK_PAD, s), BF16),
                      jax.ShapeDtypeStruct((b, V_ROWS, s), BF16)]
    else:
        in_specs += [_full(a.shape) for a in extra]
        out_specs += [pl.BlockSpec((1, MLA_HEADS, tm, QK_PAD), lambda bi, i: (bi, 0, i, 0))]
        out_shape += [jax.ShapeDtypeStruct((b, MLA_HEADS, s, QK_PAD), BF16)]
    return pl.pallas_call(
        _proj_cols_kernel if cols else _proj_rows_kernel,
        grid=(b, s // tm),
        in_specs=in_specs,
        out_specs=tuple(out_specs),
        out_shape=tuple(out_shape),
        compiler_params=_params("parallel", "parallel"),
        name="proj_cols" if cols else "proj_rows",
    )(x, tab, *common, *extra)


def _attn_prompt_kernel(qt_ref, k_ref, vt_ref, wuvt_ref, g_ref, o_ref,
                        m_sc, acc_sc, st_sc, cmax_sc, *, qb):
    i = pl.program_id(1)
    m_sc[...] = jnp.full(m_sc.shape, -jnp.inf, F32)
    acc_sc[...] = jnp.zeros(acc_sc.shape, F32)

    def step(j, masked):
        ks = pl.ds(pl.multiple_of(j * qb, qb), qb)

        def scores(h):
            st = jnp.dot(k_ref[0, ks, :], qt_ref[0, h], preferred_element_type=F32)
            if masked:
                kc = lax.broadcasted_iota(jnp.int32, st.shape, 0) // CHUNK
                qc = lax.broadcasted_iota(jnp.int32, st.shape, 1) // CHUNK
                st = jnp.where(kc <= qc, st, -jnp.inf)
            st_sc[h] = st
            cmax_sc[h:h + 1, :] = jnp.max(st, axis=0, keepdims=True)

        for h in range(SCORE_LEAD):
            scores(h)
        for h in range(MLA_HEADS):
            if h + SCORE_LEAD < MLA_HEADS:
                scores(h + SCORE_LEAD)
            m_prev = m_sc[h:h + 1, :]
            m_new = jnp.maximum(m_prev, cmax_sc[h:h + 1, :])
            p = jnp.exp(st_sc[h] - m_new).astype(BF16)
            m_sc[h:h + 1, :] = m_new
            acc_sc[h] = (jnp.exp(m_prev - m_new) * acc_sc[h]
                         + jnp.dot(vt_ref[0, :, ks], p, preferred_element_type=F32))

    def body(j, carry):
        step(j, False)
        return carry

    lax.fori_loop(0, i, body, 0)
    step(i, True)
    parts = []
    for h in range(MLA_HEADS):
        acc = acc_sc[h]
        att = (acc[:F_ONES] / acc[F_ONES:F_ONES + 1]).astype(BF16)
        parts.append(jnp.dot(wuvt_ref[h], att, preferred_element_type=F32))
    mla = jnp.concatenate(parts, axis=0).T
    o_ref[0] = _rms(mla, g_ref[...]).astype(BF16)


def _attn_prompt(qt, kcat, vt, wuvt, g, qb):
    b, _, _, s = qt.shape
    return pl.pallas_call(
        functools.partial(_attn_prompt_kernel, qb=qb),
        grid=(b, s // qb),
        in_specs=[pl.BlockSpec((1, MLA_HEADS, QK_PAD, qb), lambda bi, i: (bi, 0, 0, i)),
                  pl.BlockSpec((1, s, QK_PAD), lambda bi, i: (bi, 0, 0)),
                  pl.BlockSpec((1, V_ROWS, s), lambda bi, i: (bi, 0, 0)),
                  _full(wuvt.shape), _full(g.shape)],
        out_specs=pl.BlockSpec((1, qb, MLA_WIDTH), lambda bi, i: (bi, i, 0)),
        out_shape=jax.ShapeDtypeStruct((b, s, MLA_WIDTH), BF16),
        scratch_shapes=[pltpu.VMEM((MLA_HEADS, qb), F32), pltpu.VMEM((MLA_HEADS, V_ROWS, qb), F32),
                        pltpu.VMEM((MLA_HEADS, qb, qb), F32), pltpu.VMEM((MLA_HEADS, qb), F32)],
        compiler_params=_params("parallel", "arbitrary"),
        name="attn_prompt",
    )(qt, kcat, vt, wuvt, g)


def _attn_sample_kernel(q_ref, pckv_ref, pkr_ref, k_ref, wuv_ref, g_ref, o_ref, *, sd, past):
    rows = MLA_HEADS * sd
    q = q_ref[0].reshape(rows, QK_PAD)
    knew = k_ref[0]
    s_past = (lax.dot_general(q[:, :LANES], pckv_ref[0].astype(BF16), _NT, preferred_element_type=F32)
              + lax.dot_general(q[:, F_ROPE:F_ROPE + MLA_ROPE], pkr_ref[0].astype(BF16), _NT,
                                preferred_element_type=F32))
    s_new = lax.dot_general(q, knew, _NT, preferred_element_type=F32)

    def mask(s, k0):
        qc = (past + lax.broadcasted_iota(jnp.int32, s.shape, 0) % sd) // CHUNK
        kc = (k0 + lax.broadcasted_iota(jnp.int32, s.shape, 1)) // CHUNK
        return jnp.where(kc <= qc, s, -jnp.inf)

    s_past = mask(s_past, 0)
    s_new = mask(s_new, past)
    m = jnp.maximum(jnp.max(s_past, axis=-1, keepdims=True), jnp.max(s_new, axis=-1, keepdims=True))
    p_past = jnp.exp(s_past - m).astype(BF16)
    p_new = jnp.exp(s_new - m).astype(BF16)
    acc = jnp.dot(p_new, knew, preferred_element_type=F32)
    num = acc[:, :LANES] + jnp.dot(p_past, pckv_ref[0].astype(BF16), preferred_element_type=F32)
    den = acc[:, F_ONES:F_ONES + 1] + jnp.sum(p_past.astype(F32), axis=-1, keepdims=True)
    att = (num / den).astype(BF16)
    mla = jnp.zeros((sd, MLA_WIDTH), F32)
    for h in range(MLA_HEADS):
        mla = mla + jnp.dot(att[h * sd:(h + 1) * sd], wuv_ref[h], preferred_element_type=F32)
    o_ref[0] = _rms(mla, g_ref[...]).astype(BF16)


def _attn_sample(q, pckv, pkr, kcat, wuv, g, nb, sd):
    past = pckv.shape[1]
    return pl.pallas_call(
        functools.partial(_attn_sample_kernel, sd=sd, past=past),
        grid=(nb,),
        in_specs=[pl.BlockSpec((1, MLA_HEADS, sd, QK_PAD), lambda b: (0, 0, b, 0)),
                  pl.BlockSpec((1, past, MLA_KVRANK), lambda b: (b, 0, 0)),
                  pl.BlockSpec((1, past, MLA_ROPE), lambda b: (b, 0, 0)),
                  pl.BlockSpec((1, sd, QK_PAD), lambda b: (0, b, 0)),
                  _full(wuv.shape), _full(g.shape)],
        out_specs=pl.BlockSpec((1, sd, MLA_WIDTH), lambda b: (0, b, 0)),
        out_shape=jax.ShapeDtypeStruct((1, nb * sd, MLA_WIDTH), BF16),
        compiler_params=_params("parallel"),
        name="attn_sample",
    )(q, pckv, pkr, kcat, wuv, g)


def _lru_kernel(xr_ref, xg_ref, buf0_ref, h0_ref, cw_ref, cb_ref, wa_ref, ba_ref, wi_ref, bi_ref,
                lam_ref, g_ref, o_ref, hlast_ref, newbuf_ref, xp_sc, h_sc, a_sc, b_sc, hh_sc, *, ts):
    t = pl.program_id(1)
    nt = pl.num_programs(1)
    pad = SUBLANES

    @pl.when(t == 0)
    def _():
        xp_sc[pad - (CONV_WIDTH - 1):pad, :] = buf0_ref[0]
        h_sc[...] = h0_ref[0]

    xr = xr_ref[0]
    xp_sc[pad:pad + ts, :] = xr
    xc = cb_ref[...] + cw_ref[CONV_WIDTH - 1:CONV_WIDTH, :] * xr
    for k in range(CONV_WIDTH - 1):
        xc = xc + cw_ref[k:k + 1, :] * xp_sc[pad - (CONV_WIDTH - 1) + k:pad - (CONV_WIDTH - 1) + k + ts, :]
    tail = xp_sc[ts + pad - (CONV_WIDTH - 1):ts + pad, :]
    xp_sc[pad - (CONV_WIDTH - 1):pad, :] = tail

    xcb = xc.astype(BF16)
    r = jax.nn.sigmoid(jnp.dot(xcb, wa_ref[...], preferred_element_type=F32) + ba_ref[...])
    ig = jax.nn.sigmoid(jnp.dot(xcb, wi_ref[...], preferred_element_type=F32) + bi_ref[...])
    log_a = -LRU_C * r * jax.nn.softplus(-lam_ref[...])
    a = jnp.exp(log_a)
    bt = jnp.sqrt(-_expm1(2.0 * log_a)) * (ig * xc)

    ng = ts // SUBLANES
    a3 = a.reshape(ng, SUBLANES, LRU_WIDTH)
    b3 = bt.reshape(ng, SUBLANES, LRU_WIDTH)
    row = lax.broadcasted_iota(jnp.int32, a3.shape, 1)
    for d in (1, 2, 4):
        valid = row >= d
        a_sh = pltpu.roll(a3, d, axis=1)
        b_sh = pltpu.roll(b3, d, axis=1)
        b3 = jnp.where(valid, a3 * b_sh + b3, b3)
        a3 = jnp.where(valid, a3 * a_sh, a3)
    a_sc[...] = a3.reshape(ts, LRU_WIDTH)
    b_sc[...] = b3.reshape(ts, LRU_WIDTH)

    def group(gi, hprev):
        rs = pl.ds(pl.multiple_of(gi * SUBLANES, SUBLANES), SUBLANES)
        hg = a_sc[rs, :] * hprev + b_sc[rs, :]
        hh_sc[rs, :] = hg
        return hg[SUBLANES - 1:SUBLANES, :]

    hfin = lax.fori_loop(0, ng, group, h_sc[...])
    h_sc[...] = hfin

    lru_out = hh_sc[...] * _gelu(xg_ref[0])
    o_ref[0] = _rms(lru_out, g_ref[...]).astype(BF16)

    @pl.when(t == nt - 1)
    def _():
        hlast_ref[0] = hfin
        newbuf_ref[0] = tail


def _lru(xr, xg, buf0, h0, cw, cb, wa, ba, wi, bi, lam, g, ts):
    b, s, w = xr.shape
    row = pl.BlockSpec((1, ts, w), lambda bi_, t: (bi_, t, 0))
    return pl.pallas_call(
        functools.partial(_lru_kernel, ts=ts),
        grid=(b, s // ts),
        in_specs=[row, row,
                  pl.BlockSpec((1, CONV_WIDTH - 1, w), lambda bi_, t: (bi_, 0, 0)),
                  pl.BlockSpec((1, 1, w), lambda bi_, t: (bi_, 0, 0)),
                  _full(cw.shape), _full(cb.shape), _full(wa.shape), _full(ba.shape),
                  _full(wi.shape), _full(bi.shape), _full(lam.shape), _full(g.shape)],
        out_specs=(row,
                   pl.BlockSpec((1, 1, w), lambda bi_, t: (bi_, 0, 0)),
                   pl.BlockSpec((1, CONV_WIDTH - 1, w), lambda bi_, t: (bi_, 0, 0))),
        out_shape=(jax.ShapeDtypeStruct((b, s, w), BF16),
                   jax.ShapeDtypeStruct((b, 1, w), F32),
                   jax.ShapeDtypeStruct((b, CONV_WIDTH - 1, w), F32)),
        scratch_shapes=[pltpu.VMEM((ts + SUBLANES, w), F32), pltpu.VMEM((1, w), F32),
                        pltpu.VMEM((ts, w), F32), pltpu.VMEM((ts, w), F32), pltpu.VMEM((ts, w), F32)],
        compiler_params=_params("parallel", "arbitrary"),
        name="lru",
    )(xr, xg, buf0, h0, cw, cb, wa, ba, wi, bi, lam, g)


def _oddeven_pairs(n):
    pairs = []

    def merge(lo, m, r):
        step = r * 2
        if step < m:
            merge(lo, m, step)
            merge(lo + r, m, step)
            pairs.extend((i, i + r) for i in range(lo + r, lo + m - r, step))
        else:
            pairs.append((lo, lo + r))

    def sort(lo, m):
        if m > 1:
            sort(lo, m // 2)
            sort(lo + m // 2, m // 2)
            merge(lo, m, 1)

    sort(0, n)
    return pairs


_SORT16 = _oddeven_pairs(16)


def _top_values(s, n):
    t = s.shape[0] // SUBLANES
    assert t <= 16 and s.shape[0] % SUBLANES == 0
    v = [s[i * SUBLANES:(i + 1) * SUBLANES] for i in range(t)]
    for i, j in _SORT16:
        if j < t:
            v[i], v[j] = jnp.maximum(v[i], v[j]), jnp.minimum(v[i], v[j])
    top = []
    for r in range(n):
        m = jnp.max(v[0], axis=0, keepdims=True)
        top.append(m)
        hit = v[0] == m
        for i in range(min(n - r - 1, t)):
            v[i] = jnp.where(hit, v[i + 1] if i + 1 < t else -jnp.inf, v[i])
    return top


def _mix_kernel(x_ref, mla_ref, lru_ref, wo_ref, gffn_ref, wqt_ref, k1_ref, k2_ref,
                x1_ref, xn2_ref, s2_ref, e2_ref, theta_ref, e1_ref, s1_sc, s2_sc):
    x1 = (x_ref[...]
          + jnp.dot(mla_ref[...], wo_ref[:MLA_WIDTH, :], preferred_element_type=F32)
          + jnp.dot(lru_ref[...], wo_ref[MLA_WIDTH:, :], preferred_element_type=F32))
    x1_ref[...] = x1
    xn2t = _rms(x1, gffn_ref[...]).T.astype(BF16)
    xn2_ref[...] = xn2t
    qt = jnp.dot(wqt_ref[...], xn2t, preferred_element_type=F32).astype(BF16)
    for h in range(PEER_HEADS):
        base = h * PEER_KEY_DIM
        s1_sc[h] = jnp.dot(k1_ref[h], qt[base:base + PEER_HALF], preferred_element_type=F32)
        s2_sc[h] = jnp.dot(k2_ref[h], qt[base + PEER_HALF:base + PEER_KEY_DIM], preferred_element_type=F32)
    k = PEER_TOPK
    groups = x1.shape[0] // LANES

    def unit(h, lane_group):
        ls = pl.ds(pl.multiple_of(lane_group * LANES, LANES), LANES)
        s1 = s1_sc[h, :, ls]
        s2 = s2_sc[h, :, ls]
        v1 = _top_values(s1, k)
        v2 = _top_values(s2, k)
        v1a = jnp.concatenate(v1, axis=0)
        cand = [v1a + v2[0]]
        cand += [v1a[:SUBLANES] + v2[j] for j in range(1, SUBLANES)]
        cand += [v1[0] + jnp.concatenate(v2[SUBLANES:], axis=0)]
        vs = _top_values(jnp.concatenate(cand, axis=0), k)
        tau = vs[k - 1]
        sel = [c >= tau for c in cand]
        z = sum(jnp.sum(jnp.where(m, jnp.exp(c - vs[0]), 0.0), axis=0, keepdims=True)
                for m, c in zip(sel, cand))
        ones = [jnp.where(m, 1.0, 0.0) for m in sel]
        low = sum(ones[1:SUBLANES])
        cnt = ones[0] + jnp.concatenate([low, jnp.zeros_like(low)], axis=0)
        tail = jnp.sum(ones[SUBLANES], axis=0, keepdims=True)
        cnt = cnt + jnp.where(lax.broadcasted_iota(jnp.int32, cnt.shape, 0) == 0, tail, 0.0)
        th_row = jnp.full(cnt.shape, jnp.inf, F32)
        for j in range(k):
            th_row = jnp.where(cnt == float(j + 1), v2[j], th_row)
        theta = jnp.full(s1.shape, jnp.inf, F32)
        for i in range(k):
            theta = jnp.where(s1 == v1[i], th_row[i:i + 1], theta)
        s2_ref[h, :, ls] = s2
        e2_ref[h, :, ls] = jnp.exp(s2 - v2[0]) / z
        theta_ref[h, :, ls] = theta
        e1_ref[h, :, ls] = jnp.exp(s1 - v1[0])

    per = 2 if groups % 2 == 0 else 1

    def body(it, carry):
        for u in range(per):
            unit(it // (groups // per), (it % (groups // per)) * per + u)
        return carry

    lax.fori_loop(0, PEER_HEADS * groups // per, body, 0)


def _mix(x, mla, lru, wo, gffn, wqt, k1, k2, tm):
    t, d = x.shape
    row = lambda w: pl.BlockSpec((tm, w), lambda i: (i, 0))
    sc = pl.BlockSpec((PEER_HEADS, PEER_NKEYS, tm), lambda i: (0, 0, i))
    sc_shape = lambda dt: jax.ShapeDtypeStruct((PEER_HEADS, PEER_NKEYS, t), dt)
    return pl.pallas_call(
        _mix_kernel,
        grid=(t // tm,),
        in_specs=[row(d), row(MLA_WIDTH), row(LRU_WIDTH), _full(wo.shape), _full(gffn.shape),
                  _full(wqt.shape), _full(k1.shape), _full(k2.shape)],
        out_specs=(row(d), pl.BlockSpec((d, tm), lambda i: (0, i)), sc, sc, sc, sc),
        out_shape=(jax.ShapeDtypeStruct((t, d), F32), jax.ShapeDtypeStruct((d, t), BF16),
                   sc_shape(F32), sc_shape(F32), sc_shape(F32), sc_shape(F32)),
        scratch_shapes=[pltpu.VMEM((PEER_HEADS, PEER_NKEYS, tm), F32)] * 2,
        compiler_params=_params("parallel"),
        name="mix",
    )(x, mla, lru, wo, gffn, wqt, k1, k2)


def _peer_kernel(xn2_ref, x1_ref, u_ref, vt_ref, s2_ref, e2_ref, theta_ref, e1_ref, gfin_ref,
                 y_ref, ht0_sc, ht1_sc, wt0_sc, wt1_sc, yt_sc, *, eb, tb, ne, final_norm):
    g = pl.program_id(0)
    out_tile = jnp.maximum(g - 2, 0) % ne

    @pl.when(g == 0)
    def _():
        for ref in (ht0_sc, ht1_sc, wt0_sc, wt1_sc):
            ref[...] = jnp.zeros(ref.shape, ref.dtype)

    @pl.when(out_tile == 0)
    def _():
        yt_sc[...] = jnp.zeros(yt_sc.shape, F32)

    na = eb // PEER_NKEYS
    mh_rows = eb // 2
    nt_cols = 2 * LANES
    bq_rows = PEER_NKEYS // 4

    def stages(ht_cur, wt_cur, ht_prev, wt_prev):
        def pre_piece(mh, nt):
            ms = slice(mh * mh_rows, (mh + 1) * mh_rows)
            ns = slice(nt * nt_cols, (nt + 1) * nt_cols)
            ht_cur[ms, ns] = jnp.dot(u_ref[ms, :], xn2_ref[:, ns], preferred_element_type=F32)

        def out_piece(mh, nt):
            out_rows = yt_sc.shape[0] // (eb // mh_rows)
            ms = slice(mh * out_rows, (mh + 1) * out_rows)
            ns = slice(nt * nt_cols, (nt + 1) * nt_cols)
            yt_sc[ms, ns] += jnp.dot(vt_ref[ms, :], wt_cur[:, ns], preferred_element_type=F32)

        def mix_block(tl, bq):
            ls = slice(tl * LANES, (tl + 1) * LANES)
            bs = slice(bq * bq_rows, (bq + 1) * bq_rows)
            acc = [jnp.zeros((bq_rows, LANES), F32)] * na
            for h in range(PEER_HEADS):
                s2 = s2_ref[h, bs, ls]
                e2 = e2_ref[h, bs, ls]
                for al in range(na):
                    hit = s2 >= theta_ref[h, al:al + 1, ls]
                    acc[al] = acc[al] + jnp.where(hit, e1_ref[h, al:al + 1, ls] * e2, 0.0)
            for al in range(na):
                rs = slice(al * PEER_NKEYS + bq * bq_rows, al * PEER_NKEYS + (bq + 1) * bq_rows)
                wt_prev[rs, ls] = (acc[al] * _gelu(ht_prev[rs, ls])).astype(BF16)

        pieces = [(f, mh, nt) for nt in range(tb // nt_cols) for mh in range(eb // mh_rows)
                  for f in (pre_piece, out_piece)]
        blocks = [(tl, bq) for tl in range(tb // LANES) for bq in range(PEER_NKEYS // bq_rows)]
        per = -(-len(blocks) // len(pieces))
        for i, (f, mh, nt) in enumerate(pieces):
            f(mh, nt)
            for tl, bq in blocks[i * per:(i + 1) * per]:
                mix_block(tl, bq)

    @pl.when(g % 2 == 0)
    def _():
        stages(ht0_sc, wt0_sc, ht1_sc, wt1_sc)

    @pl.when(g % 2 == 1)
    def _():
        stages(ht1_sc, wt1_sc, ht0_sc, wt0_sc)

    @pl.when((g >= 2) & (out_tile == ne - 1))
    def _():
        y = x1_ref[...] + yt_sc[...].T
        if final_norm:
            y = _rms(y, gfin_ref[...])
        y_ref[...] = y


def _peer(xn2, x1, u, vt, s2, e2, theta, e1, gfin, tb, eb, final_norm):
    t, d = x1.shape
    ne = u.shape[0] // eb
    na = eb // PEER_NKEYS
    steps = (t // tb) * ne
    assert na == SUBLANES and u.shape[0] % eb == 0 and tb % (2 * LANES) == 0 and ne > 1

    def at(lag):
        def pos(g):
            s = jnp.clip(g - lag, 0, steps - 1)
            return s // ne, s % ne
        return pos

    pre, mid, out = at(0), at(1), at(2)
    sc = pl.BlockSpec((PEER_HEADS, PEER_NKEYS, tb), lambda g: (0, 0, mid(g)[0]))
    rows = pl.BlockSpec((PEER_HEADS, na, tb), lambda g: (0, mid(g)[1], mid(g)[0]))
    return pl.pallas_call(
        functools.partial(_peer_kernel, eb=eb, tb=tb, ne=ne, final_norm=final_norm),
        grid=(steps + 2,),
        in_specs=[pl.BlockSpec((d, tb), lambda g: (0, pre(g)[0])),
                  pl.BlockSpec((tb, d), lambda g: (out(g)[0], 0)),
                  pl.BlockSpec((eb, d), lambda g: (pre(g)[1], 0)),
                  pl.BlockSpec((d, eb), lambda g: (0, out(g)[1])),
                  sc, sc, rows, rows,
                  _full(gfin.shape)],
        out_specs=pl.BlockSpec((tb, d), lambda g: (out(g)[0], 0)),
        out_shape=jax.ShapeDtypeStruct((t, d), F32),
        scratch_shapes=[pltpu.VMEM((eb, tb), F32), pltpu.VMEM((eb, tb), F32),
                        pltpu.VMEM((eb, tb), BF16), pltpu.VMEM((eb, tb), BF16), pltpu.VMEM((d, tb), F32)],
        compiler_params=_params("arbitrary"),
        name="peer",
    )(xn2, x1, u, vt, s2, e2, theta, e1, gfin)


def _rope_tables(pos):
    half = MLA_ROPE // 2
    inv = ROPE_THETA ** (-jnp.arange(half, dtype=F32) / half)
    ang = pos.astype(F32)[:, None] * inv[None, :]
    cos, sin = jnp.cos(ang), jnp.sin(ang)
    c2 = jnp.concatenate([cos, cos], axis=-1)
    s2 = jnp.concatenate([-sin, sin], axis=-1)
    rows = jnp.concatenate([_place_cols(c2), _place_cols(s2)], axis=-1)
    return rows, jnp.concatenate([c2, s2], axis=-1).T


def _swap_halves(w):
    half = w.shape[-1] // 2
    return jnp.concatenate([w[..., half:], w[..., :half]], axis=-1)


def _place_cols(w):
    pad = [(0, 0)] * (w.ndim - 1) + [(ROPE_OFF, LANES - ROPE_OFF - w.shape[-1])]
    return jnp.pad(w, pad)


def _layer_weights(w_in, w_uq, w_uk, w_uv, lru_wa, lru_wi, w_out, peer_wq, peer_keys1, peer_keys2,
                   peer_u, peer_v):
    o1 = MLA_QRANK
    o2 = o1 + MLA_KVRANK
    o3 = o2 + MLA_ROPE
    kr = w_in[:, o2:o3]
    win = jnp.concatenate([w_in[:, :o2], _place_cols(kr), _place_cols(_swap_halves(kr)),
                           w_in[:, o3:]], axis=1).astype(BF16)
    wint = w_in[:, :o2].T.astype(BF16)
    wq3 = w_uq.reshape(MLA_QRANK, MLA_HEADS, MLA_NOPE + MLA_ROPE)
    nope = wq3[:, :, :MLA_NOPE].reshape(MLA_QRANK, MLA_HEADS * MLA_NOPE)
    rp = wq3[:, :, MLA_NOPE:]
    flat = lambda w: w.reshape(MLA_QRANK, -1)
    wuq = jnp.concatenate([nope, flat(_place_cols(rp)), flat(_place_cols(_swap_halves(rp)))], axis=1).astype(BF16)
    wuqt = jnp.concatenate([nope, flat(rp), flat(_swap_halves(rp))], axis=1).T.astype(BF16)
    eye_h = jnp.eye(MLA_HEADS, dtype=F32)
    wuk = jnp.einsum("rhd,hg->hdgr", w_uk, eye_h).reshape(MLA_HEADS * MLA_NOPE, MLA_HEADS * MLA_KVRANK)
    wuv = jnp.einsum("rhd,hg->hrgd", w_uv, eye_h).reshape(MLA_HEADS, MLA_KVRANK, MLA_WIDTH)
    wuvt = jnp.transpose(w_uv, (1, 2, 0))
    eye_b = jnp.eye(LRU_BLOCKS, dtype=F32)
    bd = lambda w: jnp.einsum("nde,nm->ndme", w, eye_b).reshape(LRU_WIDTH, LRU_WIDTH)
    return dict(win=win, wint=wint, wuq=wuq, wuqt=wuqt, wuk=wuk.astype(BF16), wukt=wuk.T.astype(BF16),
                wuv=wuv.astype(BF16), wuvt=wuvt.astype(BF16),
                wa=bd(lru_wa).astype(BF16), wi=bd(lru_wi).astype(BF16), wo=w_out.astype(BF16),
                wqt=peer_wq.T.astype(BF16), k1=peer_keys1.astype(BF16), k2=peer_keys2.astype(BF16),
                u=peer_u.astype(BF16), vt=peer_v.T.astype(BF16))


def _col(v):
    return v.reshape(-1, 1).astype(F32)


def _row(v):
    return v.reshape(1, -1).astype(F32)


def _token_tile(n, cap):
    t = min(n, cap)
    assert n % t == 0, (n, t)
    return t


def kernel(x_prompt, x_sample, cache_mla_ckv, cache_mla_krope, state_lru_h, state_lru_conv, norm_mix, w_in, norm_q, w_uq, norm_kv, w_uk, w_uv, conv_w, conv_b, lru_wa, lru_ba, lru_wi, lru_bi, lru_lambda, norm_mla_out, norm_lru_out, w_out, norm_ffn, peer_wq, peer_keys1, peer_keys2, peer_u, peer_v, norm_final):
    bp, sp, d = x_prompt.shape
    bs, sd, _ = x_sample.shape
    depth = w_in.shape[0]
    past = cache_mla_ckv.shape[2]
    ts_tok = bs * sd
    tab_p, tabt_p = _rope_tables(jnp.arange(sp))
    tab_s = jnp.tile(_rope_tables(past + jnp.arange(sd))[0], (bs, 1))
    gfin = _row(norm_final)

    xp = x_prompt
    xs = x_sample.reshape(1, ts_tok, d)
    outs = [[] for _ in range(8)]
    for l in range(depth):
        w = _layer_weights(w_in[l], w_uq[l], w_uk[l], w_uv[l], lru_wa[l], lru_wi[l], w_out[l], peer_wq[l],
                           peer_keys1[l], peer_keys2[l], peer_u[l], peer_v[l])
        last = l == depth - 1
        lru_args = (conv_w[l].astype(F32), _row(conv_b[l]), w["wa"], _row(lru_ba[l]), w["wi"], _row(lru_bi[l]),
                    _row(lru_lambda[l]), _row(norm_lru_out[l]))
        proj_common = (_row(norm_mix[l]), w["win"], _row(norm_kv[l]))
        proj_rows = (_row(norm_q[l]), w["wuq"], w["wuk"])
        proj_cols = (tabt_p, w["wint"], _col(norm_q[l]), _col(norm_kv[l]), w["wuqt"], w["wukt"])
        gmla = _row(norm_mla_out[l])

        def tail(x2d, mla, lru):
            t = x2d.shape[0]
            tm = _token_tile(t, TOKEN_TILE)
            x1, xn2, s2, e2, theta, e1 = _mix(x2d, mla, lru, w["wo"], _row(norm_ffn[l]), w["wqt"],
                                               w["k1"], w["k2"], tm)
            return _peer(xn2, x1, w["u"], w["vt"], s2, e2, theta, e1, gfin, tm, PEER_TILE, last)

        kcat, ckv, kr, xr, xg, qt, vt = _proj(xp, tab_p, proj_common, proj_cols, _token_tile(sp, TOKEN_TILE), True)
        mla = _attn_prompt(qt, kcat, vt, w["wuvt"], gmla, _token_tile(sp, TOKEN_TILE))
        lru, hl, nb = _lru(xr, xg, jnp.zeros((bp, CONV_WIDTH - 1, LRU_WIDTH), F32),
                           jnp.zeros((bp, 1, LRU_WIDTH), F32), *lru_args, _token_tile(sp, TOKEN_TILE))
        xp = tail(xp.reshape(bp * sp, d), mla.reshape(bp * sp, MLA_WIDTH),
                  lru.reshape(bp * sp, LRU_WIDTH)).reshape(bp, sp, d)
        for lst, v in zip(outs[:4], (ckv, kr, hl[:, 0], nb)):
            lst.append(v)

        kcat, ckv, kr, xr, xg, q = _proj(xs, tab_s, proj_common, proj_rows, _token_tile(ts_tok, TOKEN_TILE), False)
        mla = _attn_sample(q, cache_mla_ckv[l], cache_mla_krope[l], kcat, w["wuv"], gmla, bs, sd)
        lru, hl, nb = _lru(xr.reshape(bs, sd, LRU_WIDTH), xg.reshape(bs, sd, LRU_WIDTH),
                           state_lru_conv[l].astype(F32), state_lru_h[l].reshape(bs, 1, LRU_WIDTH).astype(F32),
                           *lru_args, sd)
        xs = tail(xs[0], mla[0], lru.reshape(ts_tok, LRU_WIDTH)).reshape(1, ts_tok, d)
        for lst, v in zip(outs[4:], (ckv.reshape(bs, sd, MLA_KVRANK), kr.reshape(bs, sd, MLA_ROPE), hl[:, 0], nb)):
            lst.append(v)

    return (xp, xs.reshape(bs, sd, d)) + tuple(jnp.stack(o) for o in outs)
```

```python
import functools

import jax
import jax.numpy as jnp
from jax import lax
from jax.experimental import pallas as pl
from jax.experimental.pallas import tpu as pltpu

F32 = jnp.float32
BF16 = jnp.bfloat16

LANES = 128
SUBLANES = 8
PACKED_ROWS = 2 * SUBLANES
CHUNK = 64
EPS = 1e-6
MLA_HEADS = 8
MLA_NOPE = 64
MLA_ROPE = 32
MLA_VDIM = 64
MLA_QRANK = 256
MLA_KVRANK = 128
ROPE_THETA = 10000.0
MLA_WIDTH = MLA_HEADS * MLA_VDIM
MLA_SCALE = (MLA_NOPE + MLA_ROPE) ** -0.5
QK_PAD = 256
LRU_WIDTH = 512
LRU_BLOCKS = 8
CONV_WIDTH = 4
LRU_C = 8.0
PEER_HEADS = 8
PEER_NKEYS = 128
PEER_KEY_DIM = 256
PEER_HALF = PEER_KEY_DIM // 2
PEER_TOPK = 16
TOKEN_TILE = 512
SCORE_LEAD = 2
PEER_TILE = 8 * PEER_NKEYS
VMEM_LIMIT = 56 * 1024 * 1024

F_ONES = MLA_KVRANK
F_ROPE = F_ONES + PACKED_ROWS
V_ROWS = F_ROPE
ROPE_OFF = F_ROPE - LANES
ZC_CQ = 0
ZC_CKV = ZC_CQ + MLA_QRANK
ZC_KR = ZC_CKV + MLA_KVRANK
ZC_KRS = ZC_KR + LANES
ZC_XR = ZC_KRS + LANES
ZC_XG = ZC_XR + LRU_WIDTH
ZC_END = ZC_XG + LRU_WIDTH
QC_NOPE = 0
QC_ROPE = QC_NOPE + MLA_HEADS * MLA_NOPE
QC_ROPES = QC_ROPE + MLA_HEADS * LANES
QR_NOPE = 0
QR_ROPE = QR_NOPE + MLA_HEADS * MLA_NOPE
QR_ROPES = QR_ROPE + MLA_HEADS * MLA_ROPE

_NT = (((1,), (1,)), ((), ()))


def _rms(x, g):
    return x * lax.rsqrt(jnp.mean(x * x, axis=-1, keepdims=True) + EPS) * g


def _rms_cols(xt, g):
    return xt * lax.rsqrt(jnp.mean(xt * xt, axis=0, keepdims=True) + EPS) * g


def _expm1(y):
    u = jnp.exp(y)
    um1 = u - 1.0
    return jnp.where(um1 == 0.0, y, jnp.where(um1 == -1.0, -1.0, um1 * y / jnp.log(u)))


def _gelu(x):
    return 0.5 * x * (1.0 + lax.erf(x * (2.0 ** -0.5)))


def _params(*sem):
    return pltpu.CompilerParams(dimension_semantics=sem, vmem_limit_bytes=VMEM_LIMIT)


def _full(shape):
    n = len(shape)
    return pl.BlockSpec(shape, lambda *_: (0,) * n)


def _proj_common(x_ref, tab_ref, gmix_ref, win_ref, gkv_ref, kcat_ref, ckv_ref, kr_ref, xr_ref, xg_ref):
    xn = _rms(x_ref[0], gmix_ref[...]).astype(BF16)
    z = jnp.dot(xn, win_ref[...], preferred_element_type=F32)
    ctab = tab_ref[:, :LANES]
    stab = tab_ref[:, LANES:]
    ckv = _rms(z[:, ZC_CKV:ZC_KR], gkv_ref[...])
    kr = z[:, ZC_KR:ZC_KRS] * ctab + z[:, ZC_KRS:ZC_XR] * stab
    ckv_ref[0] = ckv
    kr_ref[0] = kr[:, ROPE_OFF:ROPE_OFF + MLA_ROPE]
    kcat_ref[0, :, :LANES] = ckv.astype(BF16)
    lane = lax.broadcasted_iota(jnp.int32, kr.shape, 1)
    kcat_ref[0, :, LANES:] = jnp.where(lane < ROPE_OFF, 1.0, kr).astype(BF16)
    xr_ref[0] = z[:, ZC_XR:ZC_XG]
    xg_ref[0] = z[:, ZC_XG:ZC_END]
    return xn, z, ctab, stab


def _proj_rows_kernel(x_ref, tab_ref, gmix_ref, win_ref, gkv_ref, gq_ref, wuq_ref, wuk_ref,
                      kcat_ref, ckv_ref, kr_ref, xr_ref, xg_ref, q_ref):
    _, z, ctab, stab = _proj_common(x_ref, tab_ref, gmix_ref, win_ref, gkv_ref,
                                    kcat_ref, ckv_ref, kr_ref, xr_ref, xg_ref)
    cqn = _rms(z[:, ZC_CQ:ZC_CKV], gq_ref[...]).astype(BF16)
    q = jnp.dot(cqn, wuq_ref[...], preferred_element_type=F32)
    qlat = jnp.dot(q[:, QC_NOPE:QC_ROPE].astype(BF16), wuk_ref[...], preferred_element_type=F32)
    for h in range(MLA_HEADS):
        q_ref[0, h, :, :LANES] = (qlat[:, h * LANES:(h + 1) * LANES] * MLA_SCALE).astype(BF16)
        rp = (q[:, QC_ROPE + h * LANES:QC_ROPE + (h + 1) * LANES] * ctab
              + q[:, QC_ROPES + h * LANES:QC_ROPES + (h + 1) * LANES] * stab)
        q_ref[0, h, :, LANES:] = (rp * MLA_SCALE).astype(BF16)


def _proj_cols_kernel(x_ref, tab_ref, gmix_ref, win_ref, gkv_ref, tabt_ref, wint_ref, gqc_ref, gkvc_ref,
                      wuqt_ref, wukt_ref, kcat_ref, ckv_ref, kr_ref, xr_ref, xg_ref, qt_ref, vt_ref):
    xn, _, _, _ = _proj_common(x_ref, tab_ref, gmix_ref, win_ref, gkv_ref,
                               kcat_ref, ckv_ref, kr_ref, xr_ref, xg_ref)
    tm = xn.shape[0]
    zt = lax.dot_general(wint_ref[...], xn, _NT, preferred_element_type=F32)
    ckvt = _rms_cols(zt[MLA_QRANK:], gkvc_ref[...])
    vt_ref[0, :F_ONES, :] = ckvt.astype(BF16)
    vt_ref[0, F_ONES:, :] = jnp.ones((V_ROWS - F_ONES, tm), BF16)
    cqnt = _rms_cols(zt[:MLA_QRANK], gqc_ref[...]).astype(BF16)
    qt = jnp.dot(wuqt_ref[...], cqnt, preferred_element_type=F32)
    qlatt = jnp.dot(wukt_ref[...], qt[QR_NOPE:QR_ROPE].astype(BF16), preferred_element_type=F32)
    ctabt = tabt_ref[:MLA_ROPE, :]
    stabt = tabt_ref[MLA_ROPE:, :]
    for h in range(MLA_HEADS):
        qt_ref[0, h, :F_ONES, :] = (qlatt[h * MLA_KVRANK:(h + 1) * MLA_KVRANK] * MLA_SCALE).astype(BF16)
        qt_ref[0, h, F_ONES:F_ROPE, :] = jnp.zeros((F_ROPE - F_ONES, tm), BF16)
        rp = (qt[QR_ROPE + h * MLA_ROPE:QR_ROPE + (h + 1) * MLA_ROPE] * ctabt
              + qt[QR_ROPES + h * MLA_ROPE:QR_ROPES + (h + 1) * MLA_ROPE] * stabt)
        qt_ref[0, h, F_ROPE:F_ROPE + MLA_ROPE, :] = (rp * MLA_SCALE).astype(BF16)
        qt_ref[0, h, F_ROPE + MLA_ROPE:, :] = jnp.zeros((QK_PAD - F_ROPE - MLA_ROPE, tm), BF16)


def _proj(x, tab, common, extra, tm, cols):
    b, s, d = x.shape
    row = lambda w: pl.BlockSpec((1, tm, w), lambda bi, i: (bi, i, 0))
    shape = lambda *dims: jax.ShapeDtypeStruct((b,) + dims, F32)
    out_specs = [row(QK_PAD), row(MLA_KVRANK), row(MLA_ROPE), row(LRU_WIDTH), row(LRU_WIDTH)]
    out_shape = [jax.ShapeDtypeStruct((b, s, QK_PAD), BF16), shape(s, MLA_KVRANK), shape(s, MLA_ROPE),
                 shape(s, LRU_WIDTH), shape(s, LRU_WIDTH)]
    in_specs = [row(d), pl.BlockSpec((tm, 2 * LANES), lambda bi, i: (i, 0))] + [_full(a.shape) for a in common]
    if cols:
        tabt, rest = extra[0], extra[1:]
        in_specs += [pl.BlockSpec((2 * MLA_ROPE, tm), lambda bi, i: (0, i))] + [_full(a.shape) for a in rest]
        out_specs += [pl.BlockSpec((1, MLA_HEADS, QK_PAD, tm), lambda bi, i: (bi, 0, 0, i)),
                      pl.BlockSpec((1, V_ROWS, tm), lambda bi, i: (bi, 0, i))]
        out_shape += [jax.ShapeDtypeStruct((b, MLA_HEADS, QK_PAD, s), BF16),
                      jax.ShapeDtypeStruct((b, V_ROWS, s), BF16)]
    else:
        in_specs += [_full(a.shape) for a in extra]
        out_specs += [pl.BlockSpec((1, MLA_HEADS, tm, QK_PAD), lambda bi, i: (bi, 0, i, 0))]
        out_shape += [jax.ShapeDtypeStruct((b, MLA_HEADS, s, QK_PAD), BF16)]
    return pl.pallas_call(
        _proj_cols_kernel if cols else _proj_rows_kernel,
        grid=(b, s // tm),
        in_specs=in_specs,
        out_specs=tuple(out_specs),
        out_shape=tuple(out_shape),
        compiler_params=_params("parallel", "parallel"),
        name="proj_cols" if cols else "proj_rows",
    )(x, tab, *common, *extra)


def _attn_prompt_kernel(qt_ref, k_ref, vt_ref, wuvt_ref, g_ref, o_ref,
                        m_sc, acc_sc, st_sc, cmax_sc, *, qb):
    i = pl.program_id(1)
    m_sc[...] = jnp.full(m_sc.shape, -jnp.inf, F32)
    acc_sc[...] = jnp.zeros(acc_sc.shape, F32)

    def step(j, masked):
        ks = pl.ds(pl.multiple_of(j * qb, qb), qb)

        def scores(h):
            st = jnp.dot(k_ref[0, ks, :], qt_ref[0, h], preferred_element_type=F32)
            if masked:
                kc = lax.broadcasted_iota(jnp.int32, st.shape, 0) // CHUNK
                qc = lax.broadcasted_iota(jnp.int32, st.shape, 1) // CHUNK
                st = jnp.where(kc <= qc, st, -jnp.inf)
            st_sc[h] = st
            cmax_sc[h:h + 1, :] = jnp.max(st, axis=0, keepdims=True)

        for h in range(SCORE_LEAD):
            scores(h)
        for h in range(MLA_HEADS):
            if h + SCORE_LEAD < MLA_HEADS:
                scores(h + SCORE_LEAD)
            m_prev = m_sc[h:h + 1, :]
            m_new = jnp.maximum(m_prev, cmax_sc[h:h + 1, :])
            p = jnp.exp(st_sc[h] - m_new).astype(BF16)
            m_sc[h:h + 1, :] = m_new
            acc_sc[h] = (jnp.exp(m_prev - m_new) * acc_sc[h]
                         + jnp.dot(vt_ref[0, :, ks], p, preferred_element_type=F32))

    def body(j, carry):
        step(j, False)
        return carry

    lax.fori_loop(0, i, body, 0)
    step(i, True)
    parts = []
    for h in range(MLA_HEADS):
        acc = acc_sc[h]
        att = (acc[:F_ONES] / acc[F_ONES:F_ONES + 1]).astype(BF16)
        parts.append(jnp.dot(wuvt_ref[h], att, preferred_element_type=F32))
    mla = jnp.concatenate(parts, axis=0).T
    o_ref[0] = _rms(mla, g_ref[...]).astype(BF16)


def _attn_prompt(qt, kcat, vt, wuvt, g, qb):
    b, _, _, s = qt.shape
    return pl.pallas_call(
        functools.partial(_attn_prompt_kernel, qb=qb),
        grid=(b, s // qb),
        in_specs=[pl.BlockSpec((1, MLA_HEADS, QK_PAD, qb), lambda bi, i: (bi, 0, 0, i)),
                  pl.BlockSpec((1, s, QK_PAD), lambda bi, i: (bi, 0, 0)),
                  pl.BlockSpec((1, V_ROWS, s), lambda bi, i: (bi, 0, 0)),
                  _full(wuvt.shape), _full(g.shape)],
        out_specs=pl.BlockSpec((1, qb, MLA_WIDTH), lambda bi, i: (bi, i, 0)),
        out_shape=jax.ShapeDtypeStruct((b, s, MLA_WIDTH), BF16),
        scratch_shapes=[pltpu.VMEM((MLA_HEADS, qb), F32), pltpu.VMEM((MLA_HEADS, V_ROWS, qb), F32),
                        pltpu.VMEM((MLA_HEADS, qb, qb), F32), pltpu.VMEM((MLA_HEADS, qb), F32)],
        compiler_params=_params("parallel", "arbitrary"),
        name="attn_prompt",
    )(qt, kcat, vt, wuvt, g)


def _attn_sample_kernel(q_ref, pckv_ref, pkr_ref, k_ref, wuv_ref, g_ref, o_ref, *, sd, past):
    rows = MLA_HEADS * sd
    q = q_ref[0].reshape(rows, QK_PAD)
    knew = k_ref[0]
    s_past = (lax.dot_general(q[:, :LANES], pckv_ref[0].astype(BF16), _NT, preferred_element_type=F32)
              + lax.dot_general(q[:, F_ROPE:F_ROPE + MLA_ROPE], pkr_ref[0].astype(BF16), _NT,
                                preferred_element_type=F32))
    s_new = lax.dot_general(q, knew, _NT, preferred_element_type=F32)

    def mask(s, k0):
        qc = (past + lax.broadcasted_iota(jnp.int32, s.shape, 0) % sd) // CHUNK
        kc = (k0 + lax.broadcasted_iota(jnp.int32, s.shape, 1)) // CHUNK
        return jnp.where(kc <= qc, s, -jnp.inf)

    s_past = mask(s_past, 0)
    s_new = mask(s_new, past)
    m = jnp.maximum(jnp.max(s_past, axis=-1, keepdims=True), jnp.max(s_new, axis=-1, keepdims=True))
    p_past = jnp.exp(s_past - m).astype(BF16)
    p_new = jnp.exp(s_new - m).astype(BF16)
    acc = jnp.dot(p_new, knew, preferred_element_type=F32)
    num = acc[:, :LANES] + jnp.dot(p_past, pckv_ref[0].astype(BF16), preferred_element_type=F32)
    den = acc[:, F_ONES:F_ONES + 1] + jnp.sum(p_past.astype(F32), axis=-1, keepdims=True)
    att = (num / den).astype(BF16)
    mla = jnp.zeros((sd, MLA_WIDTH), F32)
    for h in range(MLA_HEADS):
        mla = mla + jnp.dot(att[h * sd:(h + 1) * sd], wuv_ref[h], preferred_element_type=F32)
    o_ref[0] = _rms(mla, g_ref[...]).astype(BF16)


def _attn_sample(q, pckv, pkr, kcat, wuv, g, nb, sd):
    past = pckv.shape[1]
    return pl.pallas_call(
        functools.partial(_attn_sample_kernel, sd=sd, past=past),
        grid=(nb,),
        in_specs=[pl.BlockSpec((1, MLA_HEADS, sd, QK_PAD), lambda b: (0, 0, b, 0)),
                  pl.BlockSpec((1, past, MLA_KVRANK), lambda b: (b, 0, 0)),
                  pl.BlockSpec((1, past, MLA_ROPE), lambda b: (b, 0, 0)),
                  pl.BlockSpec((1, sd, QK_PAD), lambda b: (0, b, 0)),
                  _full(wuv.shape), _full(g.shape)],
        out_specs=pl.BlockSpec((1, sd, MLA_WIDTH), lambda b: (0, b, 0)),
        out_shape=jax.ShapeDtypeStruct((1, nb * sd, MLA_WIDTH), BF16),
        compiler_params=_params("parallel"),
        name="attn_sample",
    )(q, pckv, pkr, kcat, wuv, g)


def _lru_kernel(xr_ref, xg_ref, buf0_ref, h0_ref, cw_ref, cb_ref, wa_ref, ba_ref, wi_ref, bi_ref,
                lam_ref, g_ref, o_ref, hlast_ref, newbuf_ref, xp_sc, h_sc, a_sc, b_sc, hh_sc, *, ts):
    t = pl.program_id(1)
    nt = pl.num_programs(1)
    pad = SUBLANES

    @pl.when(t == 0)
    def _():
        xp_sc[pad - (CONV_WIDTH - 1):pad, :] = buf0_ref[0]
        h_sc[...] = h0_ref[0]

    xr = xr_ref[0]
    xp_sc[pad:pad + ts, :] = xr
    xc = cb_ref[...] + cw_ref[CONV_WIDTH - 1:CONV_WIDTH, :] * xr
    for k in range(CONV_WIDTH - 1):
        xc = xc + cw_ref[k:k + 1, :] * xp_sc[pad - (CONV_WIDTH - 1) + k:pad - (CONV_WIDTH - 1) + k + ts, :]
    tail = xp_sc[ts + pad - (CONV_WIDTH - 1):ts + pad, :]
    xp_sc[pad - (CONV_WIDTH - 1):pad, :] = tail

    xcb = xc.astype(BF16)
    r = jax.nn.sigmoid(jnp.dot(xcb, wa_ref[...], preferred_element_type=F32) + ba_ref[...])
    ig = jax.nn.sigmoid(jnp.dot(xcb, wi_ref[...], preferred_element_type=F32) + bi_ref[...])
    log_a = -LRU_C * r * jax.nn.softplus(-lam_ref[...])
    a = jnp.exp(log_a)
    bt = jnp.sqrt(-_expm1(2.0 * log_a)) * (ig * xc)

    ng = ts // SUBLANES
    a3 = a.reshape(ng, SUBLANES, LRU_WIDTH)
    b3 = bt.reshape(ng, SUBLANES, LRU_WIDTH)
    row = lax.broadcasted_iota(jnp.int32, a3.shape, 1)
    for d in (1, 2, 4):
        valid = row >= d
        a_sh = pltpu.roll(a3, d, axis=1)
        b_sh = pltpu.roll(b3, d, axis=1)
        b3 = jnp.where(valid, a3 * b_sh + b3, b3)
        a3 = jnp.where(valid, a3 * a_sh, a3)
    a_sc[...] = a3.reshape(ts, LRU_WIDTH)
    b_sc[...] = b3.reshape(ts, LRU_WIDTH)

    def group(gi, hprev):
        rs = pl.ds(pl.multiple_of(gi * SUBLANES, SUBLANES), SUBLANES)
        hg = a_sc[rs, :] * hprev + b_sc[rs, :]
        hh_sc[rs, :] = hg
        return hg[SUBLANES - 1:SUBLANES, :]

    hfin = lax.fori_loop(0, ng, group, h_sc[...])
    h_sc[...] = hfin

    lru_out = hh_sc[...] * _gelu(xg_ref[0])
    o_ref[0] = _rms(lru_out, g_ref[...]).astype(BF16)

    @pl.when(t == nt - 1)
    def _():
        hlast_ref[0] = hfin
        newbuf_ref[0] = tail


def _lru(xr, xg, buf0, h0, cw, cb, wa, ba, wi, bi, lam, g, ts):
    b, s, w = xr.shape
    row = pl.BlockSpec((1, ts, w), lambda bi_, t: (bi_, t, 0))
    return pl.pallas_call(
        functools.partial(_lru_kernel, ts=ts),
        grid=(b, s // ts),
        in_specs=[row, row,
                  pl.BlockSpec((1, CONV_WIDTH - 1, w), lambda bi_, t: (bi_, 0, 0)),
                  pl.BlockSpec((1, 1, w), lambda bi_, t: (bi_, 0, 0)),
                  _full(cw.shape), _full(cb.shape), _full(wa.shape), _full(ba.shape),
                  _full(wi.shape), _full(bi.shape), _full(lam.shape), _full(g.shape)],
        out_specs=(row,
                   pl.BlockSpec((1, 1, w), lambda bi_, t: (bi_, 0, 0)),
                   pl.BlockSpec((1, CONV_WIDTH - 1, w), lambda bi_, t: (bi_, 0, 0))),
        out_shape=(jax.ShapeDtypeStruct((b, s, w), BF16),
                   jax.ShapeDtypeStruct((b, 1, w), F32),
                   jax.ShapeDtypeStruct((b, CONV_WIDTH - 1, w), F32)),
        scratch_shapes=[pltpu.VMEM((ts + SUBLANES, w), F32), pltpu.VMEM((1, w), F32),
                        pltpu.VMEM((ts, w), F32), pltpu.VMEM((ts, w), F32), pltpu.VMEM((ts, w), F32)],
        compiler_params=_params("parallel", "arbitrary"),
        name="lru",
    )(xr, xg, buf0, h0, cw, cb, wa, ba, wi, bi, lam, g)


def _oddeven_pairs(n):
    pairs = []

    def merge(lo, m, r):
        step = r * 2
        if step < m:
            merge(lo, m, step)
            merge(lo + r, m, step)
            pairs.extend((i, i + r) for i in range(lo + r, lo + m - r, step))
        else:
            pairs.append((lo, lo + r))

    def sort(lo, m):
        if m > 1:
            sort(lo, m // 2)
            sort(lo + m // 2, m // 2)
            merge(lo, m, 1)

    sort(0, n)
    return pairs


_SORT16 = _oddeven_pairs(16)


def _top_values(s, n):
    t = s.shape[0] // SUBLANES
    assert t <= 16 and s.shape[0] % SUBLANES == 0
    v = [s[i * SUBLANES:(i + 1) * SUBLANES] for i in range(t)]
    for i, j in _SORT16:
        if j < t:
            v[i], v[j] = jnp.maximum(v[i], v[j]), jnp.minimum(v[i], v[j])
    top = []
    for r in range(n):
        m = jnp.max(v[0], axis=0, keepdims=True)
        top.append(m)
        hit = v[0] == m
        for i in range(min(n - r - 1, t)):
            v[i] = jnp.where(hit, v[i + 1] if i + 1 < t else -jnp.inf, v[i])
    return top


def _mix_kernel(x_ref, mla_ref, lru_ref, wo_ref, gffn_ref, wqt_ref, k1_ref, k2_ref,
                x1_ref, xn2_ref, s2_ref, e2_ref, theta_ref, e1_ref, s1_sc, s2_sc):
    x1 = (x_ref[...]
          + jnp.dot(mla_ref[...], wo_ref[:MLA_WIDTH, :], preferred_element_type=F32)
          + jnp.dot(lru_ref[...], wo_ref[MLA_WIDTH:, :], preferred_element_type=F32))
    x1_ref[...] = x1
    xn2t = _rms(x1, gffn_ref[...]).T.astype(BF16)
    xn2_ref[0] = xn2t
    qt = jnp.dot(wqt_ref[...], xn2t, preferred_element_type=F32).astype(BF16)
    for h in range(PEER_HEADS):
        base = h * PEER_KEY_DIM
        s1_sc[h] = jnp.dot(k1_ref[h], qt[base:base + PEER_HALF], preferred_element_type=F32)
        s2_sc[h] = jnp.dot(k2_ref[h], qt[base + PEER_HALF:base + PEER_KEY_DIM], preferred_element_type=F32)
    k = PEER_TOPK
    groups = x1.shape[0] // LANES

    def unit(h, lane_group):
        ls = pl.ds(pl.multiple_of(lane_group * LANES, LANES), LANES)
        s1 = s1_sc[h, :, ls]
        s2 = s2_sc[h, :, ls]
        v1 = _top_values(s1, k)
        v2 = _top_values(s2, k)
        v1a = jnp.concatenate(v1, axis=0)
        cand = [v1a + v2[0]]
        cand += [v1a[:SUBLANES] + v2[j] for j in range(1, SUBLANES)]
        cand += [v1[0] + jnp.concatenate(v2[SUBLANES:], axis=0)]
        vs = _top_values(jnp.concatenate(cand, axis=0), k)
        tau = vs[k - 1]
        sel = [c >= tau for c in cand]
        z = sum(jnp.sum(jnp.where(m, jnp.exp(c - vs[0]), 0.0), axis=0, keepdims=True)
                for m, c in zip(sel, cand))
        ones = [jnp.where(m, 1.0, 0.0) for m in sel]
        low = sum(ones[1:SUBLANES])
        cnt = ones[0] + jnp.concatenate([low, jnp.zeros_like(low)], axis=0)
        tail = jnp.sum(ones[SUBLANES], axis=0, keepdims=True)
        cnt = cnt + jnp.where(lax.broadcasted_iota(jnp.int32, cnt.shape, 0) == 0, tail, 0.0)
        th_row = jnp.full(cnt.shape, jnp.inf, F32)
        for j in range(k):
            th_row = jnp.where(cnt == float(j + 1), v2[j], th_row)
        theta = jnp.full(s1.shape, jnp.inf, F32)
        for i in range(k):
            theta = jnp.where(s1 == v1[i], th_row[i:i + 1], theta)
        s2_ref[0, h, :, ls] = s2
        e2_ref[0, h, :, ls] = jnp.exp(s2 - v2[0]) / z
        theta_ref[0, h, :, ls] = theta
        e1_ref[0, h, :, ls] = jnp.exp(s1 - v1[0])

    per = 2 if groups % 2 == 0 else 1

    def body(it, carry):
        for u in range(per):
            unit(it // (groups // per), (it % (groups // per)) * per + u)
        return carry

    lax.fori_loop(0, PEER_HEADS * groups // per, body, 0)


def _mix(x, mla, lru, wo, gffn, wqt, k1, k2, tm):
    t, d = x.shape
    row = lambda w: pl.BlockSpec((tm, w), lambda i: (i, 0))
    sc = pl.BlockSpec((1, PEER_HEADS, PEER_NKEYS, tm), lambda i: (i, 0, 0, 0))
    sc_shape = lambda dt: jax.ShapeDtypeStruct((t // tm, PEER_HEADS, PEER_NKEYS, tm), dt)
    return pl.pallas_call(
        _mix_kernel,
        grid=(t // tm,),
        in_specs=[row(d), row(MLA_WIDTH), row(LRU_WIDTH), _full(wo.shape), _full(gffn.shape),
                  _full(wqt.shape), _full(k1.shape), _full(k2.shape)],
        out_specs=(row(d), pl.BlockSpec((1, d, tm), lambda i: (i, 0, 0)), sc, sc, sc, sc),
        out_shape=(jax.ShapeDtypeStruct((t, d), F32), jax.ShapeDtypeStruct((t // tm, d, tm), BF16),
                   sc_shape(F32), sc_shape(F32), sc_shape(F32), sc_shape(F32)),
        scratch_shapes=[pltpu.VMEM((PEER_HEADS, PEER_NKEYS, tm), F32)] * 2,
        compiler_params=_params("parallel"),
        name="mix",
    )(x, mla, lru, wo, gffn, wqt, k1, k2)


def _peer_kernel(xn2_ref, x1_ref, u_ref, vt_ref, s2_ref, e2_ref, theta_ref, e1_ref, gfin_ref,
                 y_ref, ht0_sc, ht1_sc, wt0_sc, wt1_sc, yt_sc, *, eb, tb, ne, final_norm):
    g = pl.program_id(0)
    out_tile = jnp.maximum(g - 2, 0) % ne

    @pl.when(g == 0)
    def _():
        for ref in (ht0_sc, ht1_sc, wt0_sc, wt1_sc):
            ref[...] = jnp.zeros(ref.shape, ref.dtype)

    @pl.when(out_tile == 0)
    def _():
        yt_sc[...] = jnp.zeros(yt_sc.shape, F32)

    na = eb // PEER_NKEYS
    mh_rows = eb // 2
    nt_cols = 2 * LANES
    bq_rows = PEER_NKEYS // 4

    def stages(ht_cur, wt_cur, ht_prev, wt_prev):
        def pre_piece(mh, nt):
            ms = slice(mh * mh_rows, (mh + 1) * mh_rows)
            ns = slice(nt * nt_cols, (nt + 1) * nt_cols)
            ht_cur[ms, ns] = jnp.dot(u_ref[ms, :], xn2_ref[0, :, ns], preferred_element_type=F32)

        def out_piece(mh, nt):
            out_rows = yt_sc.shape[0] // (eb // mh_rows)
            ms = slice(mh * out_rows, (mh + 1) * out_rows)
            ns = slice(nt * nt_cols, (nt + 1) * nt_cols)
            yt_sc[ms, ns] += jnp.dot(vt_ref[0, ms, :], wt_cur[:, ns], preferred_element_type=F32)

        def mix_block(tl, bq):
            ls = slice(tl * LANES, (tl + 1) * LANES)
            bs = slice(bq * bq_rows, (bq + 1) * bq_rows)
            acc = [jnp.zeros((bq_rows, LANES), F32)] * na
            for h in range(PEER_HEADS):
                s2 = s2_ref[0, h, bs, ls]
                e2 = e2_ref[0, h, bs, ls]
                for al in range(na):
                    hit = s2 >= theta_ref[0, h, al:al + 1, ls]
                    acc[al] = acc[al] + jnp.where(hit, e1_ref[0, h, al:al + 1, ls] * e2, 0.0)
            for al in range(na):
                rs = slice(al * PEER_NKEYS + bq * bq_rows, al * PEER_NKEYS + (bq + 1) * bq_rows)
                wt_prev[rs, ls] = (acc[al] * _gelu(ht_prev[rs, ls])).astype(BF16)

        pieces = [(f, mh, nt) for nt in range(tb // nt_cols) for mh in range(eb // mh_rows)
                  for f in (pre_piece, out_piece)]
        blocks = [(tl, bq) for tl in range(tb // LANES) for bq in range(PEER_NKEYS // bq_rows)]
        per = -(-len(blocks) // len(pieces))
        for i, (f, mh, nt) in enumerate(pieces):
            f(mh, nt)
            for tl, bq in blocks[i * per:(i + 1) * per]:
                mix_block(tl, bq)

    @pl.when(g % 2 == 0)
    def _():
        stages(ht0_sc, wt0_sc, ht1_sc, wt1_sc)

    @pl.when(g % 2 == 1)
    def _():
        stages(ht1_sc, wt1_sc, ht0_sc, wt0_sc)

    @pl.when((g >= 2) & (out_tile == ne - 1))
    def _():
        y = x1_ref[...] + yt_sc[...].T
        if final_norm:
            y = _rms(y, gfin_ref[...])
        y_ref[...] = y


def _peer(xn2, x1, u, vt, s2, e2, theta, e1, gfin, tb, eb, final_norm):
    t, d = x1.shape
    ne = u.shape[0] // eb
    na = eb // PEER_NKEYS
    steps = (t // tb) * ne
    assert na == SUBLANES and u.shape[0] % eb == 0 and tb % (2 * LANES) == 0 and ne > 1

    def at(lag):
        def pos(g):
            s = jnp.clip(g - lag, 0, steps - 1)
            return s // ne, s % ne
        return pos

    pre, mid, out = at(0), at(1), at(2)
    sc = pl.BlockSpec((1, PEER_HEADS, PEER_NKEYS, tb), lambda g: (mid(g)[0], 0, 0, 0))
    rows = pl.BlockSpec((1, PEER_HEADS, na, tb), lambda g: (mid(g)[0], 0, mid(g)[1], 0))
    return pl.pallas_call(
        functools.partial(_peer_kernel, eb=eb, tb=tb, ne=ne, final_norm=final_norm),
        grid=(steps + 2,),
        in_specs=[pl.BlockSpec((1, d, tb), lambda g: (pre(g)[0], 0, 0)),
                  pl.BlockSpec((tb, d), lambda g: (out(g)[0], 0)),
                  pl.BlockSpec((eb, d), lambda g: (pre(g)[1], 0)),
                  pl.BlockSpec((1, d, eb), lambda g: (out(g)[1], 0, 0)),
                  sc, sc, rows, rows,
                  _full(gfin.shape)],
        out_specs=pl.BlockSpec((tb, d), lambda g: (out(g)[0], 0)),
        out_shape=jax.ShapeDtypeStruct((t, d), F32),
        scratch_shapes=[pltpu.VMEM((eb, tb), F32), pltpu.VMEM((eb, tb), F32),
                        pltpu.VMEM((eb, tb), BF16), pltpu.VMEM((eb, tb), BF16), pltpu.VMEM((d, tb), F32)],
        compiler_params=_params("arbitrary"),
        name="peer",
    )(xn2, x1, u, vt, s2, e2, theta, e1, gfin)


def _rope_tables(pos):
    half = MLA_ROPE // 2
    inv = ROPE_THETA ** (-jnp.arange(half, dtype=F32) / half)
    ang = pos.astype(F32)[:, None] * inv[None, :]
    cos, sin = jnp.cos(ang), jnp.sin(ang)
    c2 = jnp.concatenate([cos, cos], axis=-1)
    s2 = jnp.concatenate([-sin, sin], axis=-1)
    rows = jnp.concatenate([_place_cols(c2), _place_cols(s2)], axis=-1)
    return rows, jnp.concatenate([c2, s2], axis=-1).T


def _swap_halves(w):
    half = w.shape[-1] // 2
    return jnp.concatenate([w[..., half:], w[..., :half]], axis=-1)


def _place_cols(w):
    pad = [(0, 0)] * (w.ndim - 1) + [(ROPE_OFF, LANES - ROPE_OFF - w.shape[-1])]
    return jnp.pad(w, pad)


def _layer_weights(w_in, w_uq, w_uk, w_uv, lru_wa, lru_wi, w_out, peer_wq, peer_keys1, peer_keys2,
                   peer_u, peer_v):
    o1 = MLA_QRANK
    o2 = o1 + MLA_KVRANK
    o3 = o2 + MLA_ROPE
    kr = w_in[:, o2:o3]
    win = jnp.concatenate([w_in[:, :o2], _place_cols(kr), _place_cols(_swap_halves(kr)),
                           w_in[:, o3:]], axis=1).astype(BF16)
    wint = w_in[:, :o2].T.astype(BF16)
    wq3 = w_uq.reshape(MLA_QRANK, MLA_HEADS, MLA_NOPE + MLA_ROPE)
    nope = wq3[:, :, :MLA_NOPE].reshape(MLA_QRANK, MLA_HEADS * MLA_NOPE)
    rp = wq3[:, :, MLA_NOPE:]
    flat = lambda w: w.reshape(MLA_QRANK, -1)
    wuq = jnp.concatenate([nope, flat(_place_cols(rp)), flat(_place_cols(_swap_halves(rp)))], axis=1).astype(BF16)
    wuqt = jnp.concatenate([nope, flat(rp), flat(_swap_halves(rp))], axis=1).T.astype(BF16)
    eye_h = jnp.eye(MLA_HEADS, dtype=F32)
    wuk = jnp.einsum("rhd,hg->hdgr", w_uk, eye_h).reshape(MLA_HEADS * MLA_NOPE, MLA_HEADS * MLA_KVRANK)
    wuv = jnp.einsum("rhd,hg->hrgd", w_uv, eye_h).reshape(MLA_HEADS, MLA_KVRANK, MLA_WIDTH)
    wuvt = jnp.transpose(w_uv, (1, 2, 0))
    eye_b = jnp.eye(LRU_BLOCKS, dtype=F32)
    bd = lambda w: jnp.einsum("nde,nm->ndme", w, eye_b).reshape(LRU_WIDTH, LRU_WIDTH)
    return dict(win=win, wint=wint, wuq=wuq, wuqt=wuqt, wuk=wuk.astype(BF16), wukt=wuk.T.astype(BF16),
                wuv=wuv.astype(BF16), wuvt=wuvt.astype(BF16),
                wa=bd(lru_wa).astype(BF16), wi=bd(lru_wi).astype(BF16), wo=w_out.astype(BF16),
                wqt=peer_wq.T.astype(BF16), k1=peer_keys1.astype(BF16), k2=peer_keys2.astype(BF16),
                u=peer_u.astype(BF16),
                vt=peer_v.reshape(-1, PEER_TILE, peer_v.shape[-1]).transpose(0, 2, 1).astype(BF16))


def _col(v):
    return v.reshape(-1, 1).astype(F32)


def _row(v):
    return v.reshape(1, -1).astype(F32)


def _token_tile(n, cap):
    t = min(n, cap)
    assert n % t == 0, (n, t)
    return t


def kernel(x_prompt, x_sample, cache_mla_ckv, cache_mla_krope, state_lru_h, state_lru_conv, norm_mix, w_in, norm_q, w_uq, norm_kv, w_uk, w_uv, conv_w, conv_b, lru_wa, lru_ba, lru_wi, lru_bi, lru_lambda, norm_mla_out, norm_lru_out, w_out, norm_ffn, peer_wq, peer_keys1, peer_keys2, peer_u, peer_v, norm_final):
    bp, sp, d = x_prompt.shape
    bs, sd, _ = x_sample.shape
    depth = w_in.shape[0]
    past = cache_mla_ckv.shape[2]
    ts_tok = bs * sd
    tab_p, tabt_p = _rope_tables(jnp.arange(sp))
    tab_s = jnp.tile(_rope_tables(past + jnp.arange(sd))[0], (bs, 1))
    gfin = _row(norm_final)

    xp = x_prompt
    xs = x_sample.reshape(1, ts_tok, d)
    outs = [[] for _ in range(8)]
    for l in range(depth):
        w = _layer_weights(w_in[l], w_uq[l], w_uk[l], w_uv[l], lru_wa[l], lru_wi[l], w_out[l], peer_wq[l],
                           peer_keys1[l], peer_keys2[l], peer_u[l], peer_v[l])
        last = l == depth - 1
        lru_args = (conv_w[l].astype(F32), _row(conv_b[l]), w["wa"], _row(lru_ba[l]), w["wi"], _row(lru_bi[l]),
                    _row(lru_lambda[l]), _row(norm_lru_out[l]))
        proj_common = (_row(norm_mix[l]), w["win"], _row(norm_kv[l]))
        proj_rows = (_row(norm_q[l]), w["wuq"], w["wuk"])
        proj_cols = (tabt_p, w["wint"], _col(norm_q[l]), _col(norm_kv[l]), w["wuqt"], w["wukt"])
        gmla = _row(norm_mla_out[l])

        def tail(x2d, mla, lru):
            t = x2d.shape[0]
            tm = _token_tile(t, TOKEN_TILE)
            x1, xn2, s2, e2, theta, e1 = _mix(x2d, mla, lru, w["wo"], _row(norm_ffn[l]), w["wqt"],
                                               w["k1"], w["k2"], tm)
            return _peer(xn2, x1, w["u"], w["vt"], s2, e2, theta, e1, gfin, tm, PEER_TILE, last)

        kcat, ckv, kr, xr, xg, qt, vt = _proj(xp, tab_p, proj_common, proj_cols, _token_tile(sp, TOKEN_TILE), True)
        mla = _attn_prompt(qt, kcat, vt, w["wuvt"], gmla, _token_tile(sp, TOKEN_TILE))
        lru, hl, nb = _lru(xr, xg, jnp.zeros((bp, CONV_WIDTH - 1, LRU_WIDTH), F32),
                           jnp.zeros((bp, 1, LRU_WIDTH), F32), *lru_args, _token_tile(sp, TOKEN_TILE))
        xp = tail(xp.reshape(bp * sp, d), mla.reshape(bp * sp, MLA_WIDTH),
                  lru.reshape(bp * sp, LRU_WIDTH)).reshape(bp, sp, d)
        for lst, v in zip(outs[:4], (ckv, kr, hl[:, 0], nb)):
            lst.append(v)

        kcat, ckv, kr, xr, xg, q = _proj(xs, tab_s, proj_common, proj_rows, _token_tile(ts_tok, TOKEN_TILE), False)
        mla = _attn_sample(q, cache_mla_ckv[l], cache_mla_krope[l], kcat, w["wuv"], gmla, bs, sd)
        lru, hl, nb = _lru(xr.reshape(bs, sd, LRU_WIDTH), xg.reshape(bs, sd, LRU_WIDTH),
                           state_lru_conv[l].astype(F32), state_lru_h[l].reshape(bs, 1, LRU_WIDTH).astype(F32),
                           *lru_args, sd)
        xs = tail(xs[0], mla[0], lru.reshape(ts_tok, LRU_WIDTH)).reshape(1, ts_tok, d)
        for lst, v in zip(outs[4:], (ckv.reshape(bs, sd, MLA_KVRANK), kr.reshape(bs, sd, MLA_ROPE), hl[:, 0], nb)):
            lst.append(v)

    return (xp, xs.reshape(bs, sd, d)) + tuple(jnp.stack(o) for o in outs)
```

```python
import functools

import jax
import jax.numpy as jnp
from jax import lax
from jax.experimental import pallas as pl
from jax.experimental.pallas import tpu as pltpu

F32 = jnp.float32
BF16 = jnp.bfloat16

LANES = 128
SUBLANES = 8
PACKED_ROWS = 2 * SUBLANES
CHUNK = 64
EPS = 1e-6
MLA_HEADS = 8
MLA_NOPE = 64
MLA_ROPE = 32
MLA_VDIM = 64
MLA_QRANK = 256
MLA_KVRANK = 128
ROPE_THETA = 10000.0
MLA_WIDTH = MLA_HEADS * MLA_VDIM
MLA_SCALE = (MLA_NOPE + MLA_ROPE) ** -0.5
QK_PAD = 256
LRU_WIDTH = 512
LRU_BLOCKS = 8
CONV_WIDTH = 4
LRU_C = 8.0
PEER_HEADS = 8
PEER_NKEYS = 128
PEER_KEY_DIM = 256
PEER_HALF = PEER_KEY_DIM // 2
PEER_TOPK = 16
TOKEN_TILE = 512
SCORE_LEAD = 2
PEER_TILE = 8 * PEER_NKEYS
VMEM_LIMIT = 56 * 1024 * 1024

F_ONES = MLA_KVRANK
F_ROPE = F_ONES + PACKED_ROWS
V_ROWS = F_ROPE
ROPE_OFF = F_ROPE - LANES
ZC_CQ = 0
ZC_CKV = ZC_CQ + MLA_QRANK
ZC_KR = ZC_CKV + MLA_KVRANK
ZC_KRS = ZC_KR + LANES
ZC_XR = ZC_KRS + LANES
ZC_XG = ZC_XR + LRU_WIDTH
ZC_END = ZC_XG + LRU_WIDTH
QC_NOPE = 0
QC_ROPE = QC_NOPE + MLA_HEADS * MLA_NOPE
QC_ROPES = QC_ROPE + MLA_HEADS * LANES
QR_NOPE = 0
QR_ROPE = QR_NOPE + MLA_HEADS * MLA_NOPE
QR_ROPES = QR_ROPE + MLA_HEADS * MLA_ROPE

_NT = (((1,), (1,)), ((), ()))


def _rms(x, g):
    return x * lax.rsqrt(jnp.mean(x * x, axis=-1, keepdims=True) + EPS) * g


def _rms_cols(xt, g):
    return xt * lax.rsqrt(jnp.mean(xt * xt, axis=0, keepdims=True) + EPS) * g


def _expm1(y):
    u = jnp.exp(y)
    um1 = u - 1.0
    return jnp.where(um1 == 0.0, y, jnp.where(um1 == -1.0, -1.0, um1 * y / jnp.log(u)))


def _gelu(x):
    return 0.5 * x * (1.0 + lax.erf(x * (2.0 ** -0.5)))


def _params(*sem):
    return pltpu.CompilerParams(dimension_semantics=sem, vmem_limit_bytes=VMEM_LIMIT)


def _full(shape):
    n = len(shape)
    return pl.BlockSpec(shape, lambda *_: (0,) * n)


def _proj_common(x_ref, tab_ref, gmix_ref, win_ref, gkv_ref, kcat_ref, ckv_ref, kr_ref, xr_ref, xg_ref):
    xn = _rms(x_ref[0], gmix_ref[...]).astype(BF16)
    z = jnp.dot(xn, win_ref[...], preferred_element_type=F32)
    ctab = tab_ref[:, :LANES]
    stab = tab_ref[:, LANES:]
    ckv = _rms(z[:, ZC_CKV:ZC_KR], gkv_ref[...])
    kr = z[:, ZC_KR:ZC_KRS] * ctab + z[:, ZC_KRS:ZC_XR] * stab
    ckv_ref[0] = ckv
    kr_ref[0] = kr[:, ROPE_OFF:ROPE_OFF + MLA_ROPE]
    kcat_ref[0, :, :LANES] = ckv.astype(BF16)
    lane = lax.broadcasted_iota(jnp.int32, kr.shape, 1)
    kcat_ref[0, :, LANES:] = jnp.where(lane < ROPE_OFF, 1.0, kr).astype(BF16)
    xr_ref[0] = z[:, ZC_XR:ZC_XG]
    xg_ref[0] = z[:, ZC_XG:ZC_END]
    return xn, z, ctab, stab


def _proj_rows_kernel(x_ref, tab_ref, gmix_ref, win_ref, gkv_ref, gq_ref, wuq_ref, wuk_ref,
                      kcat_ref, ckv_ref, kr_ref, xr_ref, xg_ref, q_ref):
    _, z, ctab, stab = _proj_common(x_ref, tab_ref, gmix_ref, win_ref, gkv_ref,
                                    kcat_ref, ckv_ref, kr_ref, xr_ref, xg_ref)
    cqn = _rms(z[:, ZC_CQ:ZC_CKV], gq_ref[...]).astype(BF16)
    q = jnp.dot(cqn, wuq_ref[...], preferred_element_type=F32)
    qlat = jnp.dot(q[:, QC_NOPE:QC_ROPE].astype(BF16), wuk_ref[...], preferred_element_type=F32)
    for h in range(MLA_HEADS):
        q_ref[0, h, :, :LANES] = (qlat[:, h * LANES:(h + 1) * LANES] * MLA_SCALE).astype(BF16)
        rp = (q[:, QC_ROPE + h * LANES:QC_ROPE + (h + 1) * LANES] * ctab
              + q[:, QC_ROPES + h * LANES:QC_ROPES + (h + 1) * LANES] * stab)
        q_ref[0, h, :, LANES:] = (rp * MLA_SCALE).astype(BF16)


def _proj_cols_kernel(x_ref, tab_ref, gmix_ref, win_ref, gkv_ref, tabt_ref, wint_ref, gqc_ref, gkvc_ref,
                      wuqt_ref, wukt_ref, kcat_ref, ckv_ref, kr_ref, xr_ref, xg_ref, qt_ref, vt_ref):
    xn, _, _, _ = _proj_common(x_ref, tab_ref, gmix_ref, win_ref, gkv_ref,
                               kcat_ref, ckv_ref, kr_ref, xr_ref, xg_ref)
    tm = xn.shape[0]
    zt = lax.dot_general(wint_ref[...], xn, _NT, preferred_element_type=F32)
    ckvt = _rms_cols(zt[MLA_QRANK:], gkvc_ref[...])
    vt_ref[0, :F_ONES, :] = ckvt.astype(BF16)
    vt_ref[0, F_ONES:, :] = jnp.ones((V_ROWS - F_ONES, tm), BF16)
    cqnt = _rms_cols(zt[:MLA_QRANK], gqc_ref[...]).astype(BF16)
    qt = jnp.dot(wuqt_ref[...], cqnt, preferred_element_type=F32)
    qlatt = jnp.dot(wukt_ref[...], qt[QR_NOPE:QR_ROPE].astype(BF16), preferred_element_type=F32)
    ctabt = tabt_ref[:MLA_ROPE, :]
    stabt = tabt_ref[MLA_ROPE:, :]
    for h in range(MLA_HEADS):
        qt_ref[0, h, :F_ONES, :] = (qlatt[h * MLA_KVRANK:(h + 1) * MLA_KVRANK] * MLA_SCALE).astype(BF16)
        qt_ref[0, h, F_ONES:F_ROPE, :] = jnp.zeros((F_ROPE - F_ONES, tm), BF16)
        rp = (qt[QR_ROPE + h * MLA_ROPE:QR_ROPE + (h + 1) * MLA_ROPE] * ctabt
              + qt[QR_ROPES + h * MLA_ROPE:QR_ROPES + (h + 1) * MLA_ROPE] * stabt)
        qt_ref[0, h, F_ROPE:F_ROPE + MLA_ROPE, :] = (rp * MLA_SCALE).astype(BF16)
        qt_ref[0, h, F_ROPE + MLA_ROPE:, :] = jnp.zeros((QK_PAD - F_ROPE - MLA_ROPE, tm), BF16)


def _proj(x, tab, common, extra, tm, cols):
    b, s, d = x.shape
    row = lambda w: pl.BlockSpec((1, tm, w), lambda bi, i: (bi, i, 0))
    shape = lambda *dims: jax.ShapeDtypeStruct((b,) + dims, F32)
    out_specs = [row(QK_PAD), row(MLA_KVRANK), row(MLA_ROPE), row(LRU_WIDTH), row(LRU_WIDTH)]
    out_shape = [jax.ShapeDtypeStruct((b, s, QK_PAD), BF16), shape(s, MLA_KVRANK), shape(s, MLA_ROPE),
                 shape(s, LRU_WIDTH), shape(s, LRU_WIDTH)]
    in_specs = [row(d), pl.BlockSpec((tm, 2 * LANES), lambda bi, i: (i, 0))] + [_full(a.shape) for a in common]
    if cols:
        tabt, rest = extra[0], extra[1:]
        in_specs += [pl.BlockSpec((2 * MLA_ROPE, tm), lambda bi, i: (0, i))] + [_full(a.shape) for a in rest]
        out_specs += [pl.BlockSpec((1, MLA_HEADS, QK_PAD, tm), lambda bi, i: (bi, 0, 0, i)),
                      pl.BlockSpec((1, V_ROWS, tm), lambda bi, i: (bi, 0, i))]
        out_shape += [jax.ShapeDtypeStruct((b, MLA_HEADS, QK_PAD, s), BF16),
                      jax.ShapeDtypeStruct((b, V_ROWS, s), BF16)]
    else:
        in_specs += [_full(a.shape) for a in extra]
        out_specs += [pl.BlockSpec((1, MLA_HEADS, tm, QK_PAD), lambda bi, i: (bi, 0, i, 0))]
        out_shape += [jax.ShapeDtypeStruct((b, MLA_HEADS, s, QK_PAD), BF16)]
    return pl.pallas_call(
        _proj_cols_kernel if cols else _proj_rows_kernel,
        grid=(b, s // tm),
        in_specs=in_specs,
        out_specs=tuple(out_specs),
        out_shape=tuple(out_shape),
        compiler_params=_params("parallel", "parallel"),
        name="proj_cols" if cols else "proj_rows",
    )(x, tab, *common, *extra)


def _attn_prompt_kernel(qt_ref, k_ref, vt_ref, wuvt_ref, g_ref, o_ref,
                        m_sc, acc_sc, st_sc, cmax_sc, *, qb):
    i = pl.program_id(1)
    m_sc[...] = jnp.full(m_sc.shape, -jnp.inf, F32)
    acc_sc[...] = jnp.zeros(acc_sc.shape, F32)

    def step(j, masked):
        ks = pl.ds(pl.multiple_of(j * qb, qb), qb)

        def scores(h):
            st = jnp.dot(k_ref[0, ks, :], qt_ref[0, h], preferred_element_type=F32)
            if masked:
                kc = lax.broadcasted_iota(jnp.int32, st.shape, 0) // CHUNK
                qc = lax.broadcasted_iota(jnp.int32, st.shape, 1) // CHUNK
                st = jnp.where(kc <= qc, st, -jnp.inf)
            st_sc[h] = st
            cmax_sc[h:h + 1, :] = jnp.max(st, axis=0, keepdims=True)

        for h in range(SCORE_LEAD):
            scores(h)
        for h in range(MLA_HEADS):
            if h + SCORE_LEAD < MLA_HEADS:
                scores(h + SCORE_LEAD)
            m_prev = m_sc[h:h + 1, :]
            m_new = jnp.maximum(m_prev, cmax_sc[h:h + 1, :])
            p = jnp.exp(st_sc[h] - m_new).astype(BF16)
            m_sc[h:h + 1, :] = m_new
            acc_sc[h] = (jnp.exp(m_prev - m_new) * acc_sc[h]
                         + jnp.dot(vt_ref[0, :, ks], p, preferred_element_type=F32))

    def body(j, carry):
        step(j, False)
        return carry

    lax.fori_loop(0, i, body, 0)
    step(i, True)
    parts = []
    for h in range(MLA_HEADS):
        acc = acc_sc[h]
        att = (acc[:F_ONES] / acc[F_ONES:F_ONES + 1]).astype(BF16)
        parts.append(jnp.dot(wuvt_ref[h], att, preferred_element_type=F32))
    mla = jnp.concatenate(parts, axis=0).T
    o_ref[0] = _rms(mla, g_ref[...]).astype(BF16)


def _attn_prompt(qt, kcat, vt, wuvt, g, qb):
    b, _, _, s = qt.shape
    return pl.pallas_call(
        functools.partial(_attn_prompt_kernel, qb=qb),
        grid=(b, s // qb),
        in_specs=[pl.BlockSpec((1, MLA_HEADS, QK_PAD, qb), lambda bi, i: (bi, 0, 0, i)),
                  pl.BlockSpec((1, s, QK_PAD), lambda bi, i: (bi, 0, 0)),
                  pl.BlockSpec((1, V_ROWS, s), lambda bi, i: (bi, 0, 0)),
                  _full(wuvt.shape), _full(g.shape)],
        out_specs=pl.BlockSpec((1, qb, MLA_WIDTH), lambda bi, i: (bi, i, 0)),
        out_shape=jax.ShapeDtypeStruct((b, s, MLA_WIDTH), BF16),
        scratch_shapes=[pltpu.VMEM((MLA_HEADS, qb), F32), pltpu.VMEM((MLA_HEADS, V_ROWS, qb), F32),
                        pltpu.VMEM((MLA_HEADS, qb, qb), F32), pltpu.VMEM((MLA_HEADS, qb), F32)],
        compiler_params=_params("parallel", "arbitrary"),
        name="attn_prompt",
    )(qt, kcat, vt, wuvt, g)


def _attn_sample_kernel(q_ref, pckv_ref, pkr_ref, k_ref, wuv_ref, g_ref, o_ref, *, sd, past):
    rows = MLA_HEADS * sd
    q = q_ref[0].reshape(rows, QK_PAD)
    knew = k_ref[0]
    s_past = (lax.dot_general(q[:, :LANES], pckv_ref[0].astype(BF16), _NT, preferred_element_type=F32)
              + lax.dot_general(q[:, F_ROPE:F_ROPE + MLA_ROPE], pkr_ref[0].astype(BF16), _NT,
                                preferred_element_type=F32))
    s_new = lax.dot_general(q, knew, _NT, preferred_element_type=F32)

    def mask(s, k0):
        qc = (past + lax.broadcasted_iota(jnp.int32, s.shape, 0) % sd) // CHUNK
        kc = (k0 + lax.broadcasted_iota(jnp.int32, s.shape, 1)) // CHUNK
        return jnp.where(kc <= qc, s, -jnp.inf)

    s_past = mask(s_past, 0)
    s_new = mask(s_new, past)
    m = jnp.maximum(jnp.max(s_past, axis=-1, keepdims=True), jnp.max(s_new, axis=-1, keepdims=True))
    p_past = jnp.exp(s_past - m).astype(BF16)
    p_new = jnp.exp(s_new - m).astype(BF16)
    acc = jnp.dot(p_new, knew, preferred_element_type=F32)
    num = acc[:, :LANES] + jnp.dot(p_past, pckv_ref[0].astype(BF16), preferred_element_type=F32)
    den = acc[:, F_ONES:F_ONES + 1] + jnp.sum(p_past.astype(F32), axis=-1, keepdims=True)
    att = (num / den).astype(BF16)
    mla = jnp.zeros((sd, MLA_WIDTH), F32)
    for h in range(MLA_HEADS):
        mla = mla + jnp.dot(att[h * sd:(h + 1) * sd], wuv_ref[h], preferred_element_type=F32)
    o_ref[0] = _rms(mla, g_ref[...]).astype(BF16)


def _attn_sample(q, pckv, pkr, kcat, wuv, g, nb, sd):
    past = pckv.shape[1]
    return pl.pallas_call(
        functools.partial(_attn_sample_kernel, sd=sd, past=past),
        grid=(nb,),
        in_specs=[pl.BlockSpec((1, MLA_HEADS, sd, QK_PAD), lambda b: (0, 0, b, 0)),
                  pl.BlockSpec((1, past, MLA_KVRANK), lambda b: (b, 0, 0)),
                  pl.BlockSpec((1, past, MLA_ROPE), lambda b: (b, 0, 0)),
                  pl.BlockSpec((1, sd, QK_PAD), lambda b: (0, b, 0)),
                  _full(wuv.shape), _full(g.shape)],
        out_specs=pl.BlockSpec((1, sd, MLA_WIDTH), lambda b: (0, b, 0)),
        out_shape=jax.ShapeDtypeStruct((1, nb * sd, MLA_WIDTH), BF16),
        compiler_params=_params("parallel"),
        name="attn_sample",
    )(q, pckv, pkr, kcat, wuv, g)


def _lru_kernel(xr_ref, xg_ref, buf0_ref, h0_ref, cw_ref, cb_ref, wa_ref, ba_ref, wi_ref, bi_ref,
                lam_ref, g_ref, o_ref, hlast_ref, newbuf_ref, xp_sc, h_sc, a_sc, b_sc, hh_sc, *, ts):
    t = pl.program_id(1)
    nt = pl.num_programs(1)
    pad = SUBLANES

    @pl.when(t == 0)
    def _():
        xp_sc[pad - (CONV_WIDTH - 1):pad, :] = buf0_ref[0]
        h_sc[...] = h0_ref[0]

    xr = xr_ref[0]
    xp_sc[pad:pad + ts, :] = xr
    xc = cb_ref[...] + cw_ref[CONV_WIDTH - 1:CONV_WIDTH, :] * xr
    for k in range(CONV_WIDTH - 1):
        xc = xc + cw_ref[k:k + 1, :] * xp_sc[pad - (CONV_WIDTH - 1) + k:pad - (CONV_WIDTH - 1) + k + ts, :]
    tail = xp_sc[ts + pad - (CONV_WIDTH - 1):ts + pad, :]
    xp_sc[pad - (CONV_WIDTH - 1):pad, :] = tail

    xcb = xc.astype(BF16)
    r = jax.nn.sigmoid(jnp.dot(xcb, wa_ref[...], preferred_element_type=F32) + ba_ref[...])
    ig = jax.nn.sigmoid(jnp.dot(xcb, wi_ref[...], preferred_element_type=F32) + bi_ref[...])
    log_a = -LRU_C * r * jax.nn.softplus(-lam_ref[...])
    a = jnp.exp(log_a)
    bt = jnp.sqrt(-_expm1(2.0 * log_a)) * (ig * xc)

    ng = ts // SUBLANES
    a3 = a.reshape(ng, SUBLANES, LRU_WIDTH)
    b3 = bt.reshape(ng, SUBLANES, LRU_WIDTH)
    row = lax.broadcasted_iota(jnp.int32, a3.shape, 1)
    for d in (1, 2, 4):
        valid = row >= d
        a_sh = pltpu.roll(a3, d, axis=1)
        b_sh = pltpu.roll(b3, d, axis=1)
        b3 = jnp.where(valid, a3 * b_sh + b3, b3)
        a3 = jnp.where(valid, a3 * a_sh, a3)
    a_sc[...] = a3.reshape(ts, LRU_WIDTH)
    b_sc[...] = b3.reshape(ts, LRU_WIDTH)

    def group(gi, hprev):
        rs = pl.ds(pl.multiple_of(gi * SUBLANES, SUBLANES), SUBLANES)
        hg = a_sc[rs, :] * hprev + b_sc[rs, :]
        hh_sc[rs, :] = hg
        return hg[SUBLANES - 1:SUBLANES, :]

    hfin = lax.fori_loop(0, ng, group, h_sc[...])
    h_sc[...] = hfin

    lru_out = hh_sc[...] * _gelu(xg_ref[0])
    o_ref[0] = _rms(lru_out, g_ref[...]).astype(BF16)

    @pl.when(t == nt - 1)
    def _():
        hlast_ref[0] = hfin
        newbuf_ref[0] = tail


def _lru(xr, xg, buf0, h0, cw, cb, wa, ba, wi, bi, lam, g, ts):
    b, s, w = xr.shape
    row = pl.BlockSpec((1, ts, w), lambda bi_, t: (bi_, t, 0))
    return pl.pallas_call(
        functools.partial(_lru_kernel, ts=ts),
        grid=(b, s // ts),
        in_specs=[row, row,
                  pl.BlockSpec((1, CONV_WIDTH - 1, w), lambda bi_, t: (bi_, 0, 0)),
                  pl.BlockSpec((1, 1, w), lambda bi_, t: (bi_, 0, 0)),
                  _full(cw.shape), _full(cb.shape), _full(wa.shape), _full(ba.shape),
                  _full(wi.shape), _full(bi.shape), _full(lam.shape), _full(g.shape)],
        out_specs=(row,
                   pl.BlockSpec((1, 1, w), lambda bi_, t: (bi_, 0, 0)),
                   pl.BlockSpec((1, CONV_WIDTH - 1, w), lambda bi_, t: (bi_, 0, 0))),
        out_shape=(jax.ShapeDtypeStruct((b, s, w), BF16),
                   jax.ShapeDtypeStruct((b, 1, w), F32),
                   jax.ShapeDtypeStruct((b, CONV_WIDTH - 1, w), F32)),
        scratch_shapes=[pltpu.VMEM((ts + SUBLANES, w), F32), pltpu.VMEM((1, w), F32),
                        pltpu.VMEM((ts, w), F32), pltpu.VMEM((ts, w), F32), pltpu.VMEM((ts, w), F32)],
        compiler_params=_params("parallel", "arbitrary"),
        name="lru",
    )(xr, xg, buf0, h0, cw, cb, wa, ba, wi, bi, lam, g)


def _oddeven_pairs(n):
    pairs = []

    def merge(lo, m, r):
        step = r * 2
        if step < m:
            merge(lo, m, step)
            merge(lo + r, m, step)
            pairs.extend((i, i + r) for i in range(lo + r, lo + m - r, step))
        else:
            pairs.append((lo, lo + r))

    def sort(lo, m):
        if m > 1:
            sort(lo, m // 2)
            sort(lo + m // 2, m // 2)
            merge(lo, m, 1)

    sort(0, n)
    return pairs


_SORT16 = _oddeven_pairs(16)


def _top_values(s, n):
    t = s.shape[0] // SUBLANES
    assert t <= 16 and s.shape[0] % SUBLANES == 0
    v = [s[i * SUBLANES:(i + 1) * SUBLANES] for i in range(t)]
    for i, j in _SORT16:
        if j < t:
            v[i], v[j] = jnp.maximum(v[i], v[j]), jnp.minimum(v[i], v[j])
    top = []
    for r in range(n):
        m = jnp.max(v[0], axis=0, keepdims=True)
        top.append(m)
        hit = v[0] == m
        for i in range(min(n - r - 1, t)):
            v[i] = jnp.where(hit, v[i + 1] if i + 1 < t else -jnp.inf, v[i])
    return top


def _mix_kernel(x_ref, mla_ref, lru_ref, wo_ref, gffn_ref, wqt_ref, k1_ref, k2_ref,
                x1_ref, xn2_ref, s2_ref, e2_ref, theta_ref, e1_ref, s1_sc, s2_sc):
    x1 = (x_ref[...]
          + jnp.dot(mla_ref[...], wo_ref[:MLA_WIDTH, :], preferred_element_type=F32)
          + jnp.dot(lru_ref[...], wo_ref[MLA_WIDTH:, :], preferred_element_type=F32))
    x1_ref[...] = x1
    xn2t = _rms(x1, gffn_ref[...]).T.astype(BF16)
    xn2_ref[0] = xn2t
    qt = jnp.dot(wqt_ref[...], xn2t, preferred_element_type=F32).astype(BF16)
    for h in range(PEER_HEADS):
        base = h * PEER_KEY_DIM
        s1_sc[h] = jnp.dot(k1_ref[h], qt[base:base + PEER_HALF], preferred_element_type=F32)
        s2_sc[h] = jnp.dot(k2_ref[h], qt[base + PEER_HALF:base + PEER_KEY_DIM], preferred_element_type=F32)
    k = PEER_TOPK
    groups = x1.shape[0] // LANES

    def unit(h, lane_group):
        ls = pl.ds(pl.multiple_of(lane_group * LANES, LANES), LANES)
        s1 = s1_sc[h, :, ls]
        s2 = s2_sc[h, :, ls]
        v1 = _top_values(s1, k)
        v2 = _top_values(s2, k)
        v1a = jnp.concatenate(v1, axis=0)
        cand = [v1a + v2[0]]
        cand += [v1a[:SUBLANES] + v2[j] for j in range(1, SUBLANES)]
        cand += [v1[0] + jnp.concatenate(v2[SUBLANES:], axis=0)]
        vs = _top_values(jnp.concatenate(cand, axis=0), k)
        tau = vs[k - 1]
        sel = [c >= tau for c in cand]
        z = sum(jnp.sum(jnp.where(m, jnp.exp(c - vs[0]), 0.0), axis=0, keepdims=True)
                for m, c in zip(sel, cand))
        ones = [jnp.where(m, 1.0, 0.0) for m in sel]
        low = sum(ones[1:SUBLANES])
        cnt = ones[0] + jnp.concatenate([low, jnp.zeros_like(low)], axis=0)
        tail = jnp.sum(ones[SUBLANES], axis=0, keepdims=True)
        cnt = cnt + jnp.where(lax.broadcasted_iota(jnp.int32, cnt.shape, 0) == 0, tail, 0.0)
        th_row = jnp.full(cnt.shape, jnp.inf, F32)
        for j in range(k):
            th_row = jnp.where(cnt == float(j + 1), v2[j], th_row)
        theta = jnp.full(s1.shape, jnp.inf, F32)
        for i in range(k):
            theta = jnp.where(s1 == v1[i], th_row[i:i + 1], theta)
        s2_ref[0, h, :, ls] = s2
        e2_ref[0, h, :, ls] = jnp.exp(s2 - v2[0]) / z
        theta_ref[0, h, :, ls] = theta
        e1_ref[0, h, :, ls] = jnp.exp(s1 - v1[0])

    per = 2 if groups % 2 == 0 else 1

    def body(it, carry):
        for u in range(per):
            unit(it // (groups // per), (it % (groups // per)) * per + u)
        return carry

    lax.fori_loop(0, PEER_HEADS * groups // per, body, 0)


def _mix(x, mla, lru, wo, gffn, wqt, k1, k2, tm):
    t, d = x.shape
    row = lambda w: pl.BlockSpec((tm, w), lambda i: (i, 0))
    sc = pl.BlockSpec((1, PEER_HEADS, PEER_NKEYS, tm), lambda i: (i, 0, 0, 0))
    sc_shape = lambda dt: jax.ShapeDtypeStruct((t // tm, PEER_HEADS, PEER_NKEYS, tm), dt)
    return pl.pallas_call(
        _mix_kernel,
        grid=(t // tm,),
        in_specs=[row(d), row(MLA_WIDTH), row(LRU_WIDTH), _full(wo.shape), _full(gffn.shape),
                  _full(wqt.shape), _full(k1.shape), _full(k2.shape)],
        out_specs=(row(d), pl.BlockSpec((1, d, tm), lambda i: (i, 0, 0)), sc, sc, sc, sc),
        out_shape=(jax.ShapeDtypeStruct((t, d), F32), jax.ShapeDtypeStruct((t // tm, d, tm), BF16),
                   sc_shape(F32), sc_shape(F32), sc_shape(F32), sc_shape(F32)),
        scratch_shapes=[pltpu.VMEM((PEER_HEADS, PEER_NKEYS, tm), F32)] * 2,
        compiler_params=_params("parallel"),
        name="mix",
    )(x, mla, lru, wo, gffn, wqt, k1, k2)


def _peer_kernel(xn2_ref, x1_ref, u_ref, vt_ref, s2_ref, e2_ref, theta_ref, e1_ref, gfin_ref,
                 y_ref, ht0_sc, ht1_sc, wt0_sc, wt1_sc, yt_sc, *, eb, tb, ne, final_norm):
    g = pl.program_id(0)
    out_tile = jnp.maximum(g - 2, 0) % ne

    @pl.when(g == 0)
    def _():
        for ref in (ht0_sc, ht1_sc, wt0_sc, wt1_sc):
            ref[...] = jnp.zeros(ref.shape, ref.dtype)

    @pl.when(out_tile == 0)
    def _():
        yt_sc[...] = jnp.zeros(yt_sc.shape, F32)

    na = eb // PEER_NKEYS
    mh_rows = eb // 4
    nt_cols = 2 * LANES
    bq_rows = PEER_NKEYS // 4

    def stages(ht_cur, wt_cur, ht_prev, wt_prev):
        def pre_piece(mh, nt):
            ms = slice(mh * mh_rows, (mh + 1) * mh_rows)
            ns = slice(nt * nt_cols, (nt + 1) * nt_cols)
            ht_cur[ms, ns] = jnp.dot(u_ref[ms, :], xn2_ref[0, :, ns], preferred_element_type=F32)

        def out_piece(mh, nt):
            out_rows = yt_sc.shape[0] // (eb // mh_rows)
            ms = slice(mh * out_rows, (mh + 1) * out_rows)
            ns = slice(nt * nt_cols, (nt + 1) * nt_cols)
            yt_sc[ms, ns] += jnp.dot(vt_ref[0, ms, :], wt_cur[:, ns], preferred_element_type=F32)

        def mix_block(tl, bq):
            ls = slice(tl * LANES, (tl + 1) * LANES)
            bs = slice(bq * bq_rows, (bq + 1) * bq_rows)
            acc = [jnp.zeros((bq_rows, LANES), F32)] * na
            for h in range(PEER_HEADS):
                s2 = s2_ref[0, h, bs, ls]
                e2 = e2_ref[0, h, bs, ls]
                for al in range(na):
                    hit = s2 >= theta_ref[0, h, al:al + 1, ls]
                    acc[al] = acc[al] + jnp.where(hit, e1_ref[0, h, al:al + 1, ls] * e2, 0.0)
            for al in range(na):
                rs = slice(al * PEER_NKEYS + bq * bq_rows, al * PEER_NKEYS + (bq + 1) * bq_rows)
                wt_prev[rs, ls] = (acc[al] * _gelu(ht_prev[rs, ls])).astype(BF16)

        pieces = [(f, mh, nt) for nt in range(tb // nt_cols) for mh in range(eb // mh_rows)
                  for f in (pre_piece, out_piece)]
        blocks = [(tl, bq) for tl in range(tb // LANES) for bq in range(PEER_NKEYS // bq_rows)]
        per = -(-len(blocks) // len(pieces))
        for i, (f, mh, nt) in enumerate(pieces):
            f(mh, nt)
            for tl, bq in blocks[i * per:(i + 1) * per]:
                mix_block(tl, bq)

    @pl.when(g % 2 == 0)
    def _():
        stages(ht0_sc, wt0_sc, ht1_sc, wt1_sc)

    @pl.when(g % 2 == 1)
    def _():
        stages(ht1_sc, wt1_sc, ht0_sc, wt0_sc)

    @pl.when((g >= 2) & (out_tile == ne - 1))
    def _():
        y = x1_ref[...] + yt_sc[...].T
        if final_norm:
            y = _rms(y, gfin_ref[...])
        y_ref[...] = y


def _peer(xn2, x1, u, vt, s2, e2, theta, e1, gfin, tb, eb, final_norm):
    t, d = x1.shape
    ne = u.shape[0] // eb
    na = eb // PEER_NKEYS
    steps = (t // tb) * ne
    assert na == SUBLANES and u.shape[0] % eb == 0 and tb % (2 * LANES) == 0 and ne > 1

    def at(lag):
        def pos(g):
            s = jnp.clip(g - lag, 0, steps - 1)
            return s // ne, s % ne
        return pos

    pre, mid, out = at(0), at(1), at(2)
    sc = pl.BlockSpec((1, PEER_HEADS, PEER_NKEYS, tb), lambda g: (mid(g)[0], 0, 0, 0))
    rows = pl.BlockSpec((1, PEER_HEADS, na, tb), lambda g: (mid(g)[0], 0, mid(g)[1], 0))
    return pl.pallas_call(
        functools.partial(_peer_kernel, eb=eb, tb=tb, ne=ne, final_norm=final_norm),
        grid=(steps + 2,),
        in_specs=[pl.BlockSpec((1, d, tb), lambda g: (pre(g)[0], 0, 0)),
                  pl.BlockSpec((tb, d), lambda g: (out(g)[0], 0)),
                  pl.BlockSpec((eb, d), lambda g: (pre(g)[1], 0)),
                  pl.BlockSpec((1, d, eb), lambda g: (out(g)[1], 0, 0)),
                  sc, sc, rows, rows,
                  _full(gfin.shape)],
        out_specs=pl.BlockSpec((tb, d), lambda g: (out(g)[0], 0)),
        out_shape=jax.ShapeDtypeStruct((t, d), F32),
        scratch_shapes=[pltpu.VMEM((eb, tb), F32), pltpu.VMEM((eb, tb), F32),
                        pltpu.VMEM((eb, tb), BF16), pltpu.VMEM((eb, tb), BF16), pltpu.VMEM((d, tb), F32)],
        compiler_params=_params("arbitrary"),
        name="peer",
    )(xn2, x1, u, vt, s2, e2, theta, e1, gfin)


def _rope_tables(pos):
    half = MLA_ROPE // 2
    inv = ROPE_THETA ** (-jnp.arange(half, dtype=F32) / half)
    ang = pos.astype(F32)[:, None] * inv[None, :]
    cos, sin = jnp.cos(ang), jnp.sin(ang)
    c2 = jnp.concatenate([cos, cos], axis=-1)
    s2 = jnp.concatenate([-sin, sin], axis=-1)
    rows = jnp.concatenate([_place_cols(c2), _place_cols(s2)], axis=-1)
    return rows, jnp.concatenate([c2, s2], axis=-1).T


def _swap_halves(w):
    half = w.shape[-1] // 2
    return jnp.concatenate([w[..., half:], w[..., :half]], axis=-1)


def _place_cols(w):
    pad = [(0, 0)] * (w.ndim - 1) + [(ROPE_OFF, LANES - ROPE_OFF - w.shape[-1])]
    return jnp.pad(w, pad)


def _layer_weights(w_in, w_uq, w_uk, w_uv, lru_wa, lru_wi, w_out, peer_wq, peer_keys1, peer_keys2,
                   peer_u, peer_v):
    o1 = MLA_QRANK
    o2 = o1 + MLA_KVRANK
    o3 = o2 + MLA_ROPE
    kr = w_in[:, o2:o3]
    win = jnp.concatenate([w_in[:, :o2], _place_cols(kr), _place_cols(_swap_halves(kr)),
                           w_in[:, o3:]], axis=1).astype(BF16)
    wint = w_in[:, :o2].T.astype(BF16)
    wq3 = w_uq.reshape(MLA_QRANK, MLA_HEADS, MLA_NOPE + MLA_ROPE)
    nope = wq3[:, :, :MLA_NOPE].reshape(MLA_QRANK, MLA_HEADS * MLA_NOPE)
    rp = wq3[:, :, MLA_NOPE:]
    flat = lambda w: w.reshape(MLA_QRANK, -1)
    wuq = jnp.concatenate([nope, flat(_place_cols(rp)), flat(_place_cols(_swap_halves(rp)))], axis=1).astype(BF16)
    wuqt = jnp.concatenate([nope, flat(rp), flat(_swap_halves(rp))], axis=1).T.astype(BF16)
    eye_h = jnp.eye(MLA_HEADS, dtype=F32)
    wuk = jnp.einsum("rhd,hg->hdgr", w_uk, eye_h).reshape(MLA_HEADS * MLA_NOPE, MLA_HEADS * MLA_KVRANK)
    wuv = jnp.einsum("rhd,hg->hrgd", w_uv, eye_h).reshape(MLA_HEADS, MLA_KVRANK, MLA_WIDTH)
    wuvt = jnp.transpose(w_uv, (1, 2, 0))
    eye_b = jnp.eye(LRU_BLOCKS, dtype=F32)
    bd = lambda w: jnp.einsum("nde,nm->ndme", w, eye_b).reshape(LRU_WIDTH, LRU_WIDTH)
    return dict(win=win, wint=wint, wuq=wuq, wuqt=wuqt, wuk=wuk.astype(BF16), wukt=wuk.T.astype(BF16),
                wuv=wuv.astype(BF16), wuvt=wuvt.astype(BF16),
                wa=bd(lru_wa).astype(BF16), wi=bd(lru_wi).astype(BF16), wo=w_out.astype(BF16),
                wqt=peer_wq.T.astype(BF16), k1=peer_keys1.astype(BF16), k2=peer_keys2.astype(BF16),
                u=peer_u.astype(BF16),
                vt=peer_v.reshape(-1, PEER_TILE, peer_v.shape[-1]).transpose(0, 2, 1).astype(BF16))


def _col(v):
    return v.reshape(-1, 1).astype(F32)


def _row(v):
    return v.reshape(1, -1).astype(F32)


def _token_tile(n, cap):
    t = min(n, cap)
    assert n % t == 0, (n, t)
    return t


def kernel(x_prompt, x_sample, cache_mla_ckv, cache_mla_krope, state_lru_h, state_lru_conv, norm_mix, w_in, norm_q, w_uq, norm_kv, w_uk, w_uv, conv_w, conv_b, lru_wa, lru_ba, lru_wi, lru_bi, lru_lambda, norm_mla_out, norm_lru_out, w_out, norm_ffn, peer_wq, peer_keys1, peer_keys2, peer_u, peer_v, norm_final):
    bp, sp, d = x_prompt.shape
    bs, sd, _ = x_sample.shape
    depth = w_in.shape[0]
    past = cache_mla_ckv.shape[2]
    ts_tok = bs * sd
    tab_p, tabt_p = _rope_tables(jnp.arange(sp))
    tab_s = jnp.tile(_rope_tables(past + jnp.arange(sd))[0], (bs, 1))
    gfin = _row(norm_final)

    xp = x_prompt
    xs = x_sample.reshape(1, ts_tok, d)
    outs = [[] for _ in range(8)]
    for l in range(depth):
        w = _layer_weights(w_in[l], w_uq[l], w_uk[l], w_uv[l], lru_wa[l], lru_wi[l], w_out[l], peer_wq[l],
                           peer_keys1[l], peer_keys2[l], peer_u[l], peer_v[l])
        last = l == depth - 1
        lru_args = (conv_w[l].astype(F32), _row(conv_b[l]), w["wa"], _row(lru_ba[l]), w["wi"], _row(lru_bi[l]),
                    _row(lru_lambda[l]), _row(norm_lru_out[l]))
        proj_common = (_row(norm_mix[l]), w["win"], _row(norm_kv[l]))
        proj_rows = (_row(norm_q[l]), w["wuq"], w["wuk"])
        proj_cols = (tabt_p, w["wint"], _col(norm_q[l]), _col(norm_kv[l]), w["wuqt"], w["wukt"])
        gmla = _row(norm_mla_out[l])

        def tail(x2d, mla, lru):
            t = x2d.shape[0]
            tm = _token_tile(t, TOKEN_TILE)
            x1, xn2, s2, e2, theta, e1 = _mix(x2d, mla, lru, w["wo"], _row(norm_ffn[l]), w["wqt"],
                                               w["k1"], w["k2"], tm)
            return _peer(xn2, x1, w["u"], w["vt"], s2, e2, theta, e1, gfin, tm, PEER_TILE, last)

        kcat, ckv, kr, xr, xg, qt, vt = _proj(xp, tab_p, proj_common, proj_cols, _token_tile(sp, TOKEN_TILE), True)
        mla = _attn_prompt(qt, kcat, vt, w["wuvt"], gmla, _token_tile(sp, TOKEN_TILE))
        lru, hl, nb = _lru(xr, xg, jnp.zeros((bp, CONV_WIDTH - 1, LRU_WIDTH), F32),
                           jnp.zeros((bp, 1, LRU_WIDTH), F32), *lru_args, _token_tile(sp, TOKEN_TILE))
        xp = tail(xp.reshape(bp * sp, d), mla.reshape(bp * sp, MLA_WIDTH),
                  lru.reshape(bp * sp, LRU_WIDTH)).reshape(bp, sp, d)
        for lst, v in zip(outs[:4], (ckv, kr, hl[:, 0], nb)):
            lst.append(v)

        kcat, ckv, kr, xr, xg, q = _proj(xs, tab_s, proj_common, proj_rows, _token_tile(ts_tok, TOKEN_TILE), False)
        mla = _attn_sample(q, cache_mla_ckv[l], cache_mla_krope[l], kcat, w["wuv"], gmla, bs, sd)
        lru, hl, nb = _lru(xr.reshape(bs, sd, LRU_WIDTH), xg.reshape(bs, sd, LRU_WIDTH),
                           state_lru_conv[l].astype(F32), state_lru_h[l].reshape(bs, 1, LRU_WIDTH).astype(F32),
                           *lru_args, sd)
        xs = tail(xs[0], mla[0], lru.reshape(ts_tok, LRU_WIDTH)).reshape(1, ts_tok, d)
        for lst, v in zip(outs[4:], (ckv.reshape(bs, sd, MLA_KVRANK), kr.reshape(bs, sd, MLA_ROPE), hl[:, 0], nb)):
            lst.append(v)

    return (xp, xs.reshape(bs, sd, d)) + tuple(jnp.stack(o) for o in outs)
```

```python
import functools

import jax
import jax.numpy as jnp
from jax import lax
from jax.experimental import pallas as pl
from jax.experimental.pallas import tpu as pltpu

F32 = jnp.float32
BF16 = jnp.bfloat16

LANES = 128
SUBLANES = 8
PACKED_ROWS = 2 * SUBLANES
CHUNK = 64
EPS = 1e-6
MLA_HEADS = 8
MLA_NOPE = 64
MLA_ROPE = 32
MLA_VDIM = 64
MLA_QRANK = 256
MLA_KVRANK = 128
ROPE_THETA = 10000.0
MLA_WIDTH = MLA_HEADS * MLA_VDIM
MLA_SCALE = (MLA_NOPE + MLA_ROPE) ** -0.5
QK_PAD = 256
LRU_WIDTH = 512
LRU_BLOCKS = 8
CONV_WIDTH = 4
LRU_C = 8.0
PEER_HEADS = 8
PEER_NKEYS = 128
PEER_KEY_DIM = 256
PEER_HALF = PEER_KEY_DIM // 2
PEER_TOPK = 16
TOKEN_TILE = 512
SCORE_LEAD = 2
PEER_TILE = 8 * PEER_NKEYS
VMEM_LIMIT = 56 * 1024 * 1024

F_ONES = MLA_KVRANK
F_ROPE = F_ONES + PACKED_ROWS
V_ROWS = F_ROPE
ROPE_OFF = F_ROPE - LANES
ZC_CQ = 0
ZC_CKV = ZC_CQ + MLA_QRANK
ZC_KR = ZC_CKV + MLA_KVRANK
ZC_KRS = ZC_KR + LANES
ZC_XR = ZC_KRS + LANES
ZC_XG = ZC_XR + LRU_WIDTH
ZC_END = ZC_XG + LRU_WIDTH
QC_NOPE = 0
QC_ROPE = QC_NOPE + MLA_HEADS * MLA_NOPE
QC_ROPES = QC_ROPE + MLA_HEADS * LANES
QR_NOPE = 0
QR_ROPE = QR_NOPE + MLA_HEADS * MLA_NOPE
QR_ROPES = QR_ROPE + MLA_HEADS * MLA_ROPE

_NT = (((1,), (1,)), ((), ()))


def _rms(x, g):
    return x * lax.rsqrt(jnp.mean(x * x, axis=-1, keepdims=True) + EPS) * g


def _rms_cols(xt, g):
    return xt * lax.rsqrt(jnp.mean(xt * xt, axis=0, keepdims=True) + EPS) * g


def _expm1(y):
    u = jnp.exp(y)
    um1 = u - 1.0
    return jnp.where(um1 == 0.0, y, jnp.where(um1 == -1.0, -1.0, um1 * y / jnp.log(u)))


def _gelu(x):
    return 0.5 * x * (1.0 + lax.erf(x * (2.0 ** -0.5)))


def _params(*sem):
    return pltpu.CompilerParams(dimension_semantics=sem, vmem_limit_bytes=VMEM_LIMIT)


def _full(shape):
    n = len(shape)
    return pl.BlockSpec(shape, lambda *_: (0,) * n)


def _proj_common(x_ref, tab_ref, gmix_ref, win_ref, gkv_ref, kcat_ref, ckv_ref, kr_ref, xr_ref, xg_ref):
    xn = _rms(x_ref[0], gmix_ref[...]).astype(BF16)
    z = jnp.dot(xn, win_ref[...], preferred_element_type=F32)
    ctab = tab_ref[:, :LANES]
    stab = tab_ref[:, LANES:]
    ckv = _rms(z[:, ZC_CKV:ZC_KR], gkv_ref[...])
    kr = z[:, ZC_KR:ZC_KRS] * ctab + z[:, ZC_KRS:ZC_XR] * stab
    ckv_ref[0] = ckv
    kr_ref[0] = kr[:, ROPE_OFF:ROPE_OFF + MLA_ROPE]
    kcat_ref[0, :, :LANES] = ckv.astype(BF16)
    lane = lax.broadcasted_iota(jnp.int32, kr.shape, 1)
    kcat_ref[0, :, LANES:] = jnp.where(lane < ROPE_OFF, 1.0, kr).astype(BF16)
    xr_ref[0] = z[:, ZC_XR:ZC_XG]
    xg_ref[0] = z[:, ZC_XG:ZC_END]
    return xn, z, ctab, stab


def _proj_rows_kernel(x_ref, tab_ref, gmix_ref, win_ref, gkv_ref, gq_ref, wuq_ref, wuk_ref,
                      kcat_ref, ckv_ref, kr_ref, xr_ref, xg_ref, q_ref):
    _, z, ctab, stab = _proj_common(x_ref, tab_ref, gmix_ref, win_ref, gkv_ref,
                                    kcat_ref, ckv_ref, kr_ref, xr_ref, xg_ref)
    cqn = _rms(z[:, ZC_CQ:ZC_CKV], gq_ref[...]).astype(BF16)
    q = jnp.dot(cqn, wuq_ref[...], preferred_element_type=F32)
    qlat = jnp.dot(q[:, QC_NOPE:QC_ROPE].astype(BF16), wuk_ref[...], preferred_element_type=F32)
    for h in range(MLA_HEADS):
        q_ref[0, h, :, :LANES] = (qlat[:, h * LANES:(h + 1) * LANES] * MLA_SCALE).astype(BF16)
        rp = (q[:, QC_ROPE + h * LANES:QC_ROPE + (h + 1) * LANES] * ctab
              + q[:, QC_ROPES + h * LANES:QC_ROPES + (h + 1) * LANES] * stab)
        q_ref[0, h, :, LANES:] = (rp * MLA_SCALE).astype(BF16)


def _proj_cols_kernel(x_ref, tab_ref, gmix_ref, win_ref, gkv_ref, tabt_ref, wint_ref, gqc_ref, gkvc_ref,
                      wuqt_ref, wukt_ref, kcat_ref, ckv_ref, kr_ref, xr_ref, xg_ref, qt_ref, vt_ref):
    xn, _, _, _ = _proj_common(x_ref, tab_ref, gmix_ref, win_ref, gkv_ref,
                               kcat_ref, ckv_ref, kr_ref, xr_ref, xg_ref)
    tm = xn.shape[0]
    zt = lax.dot_general(wint_ref[...], xn, _NT, preferred_element_type=F32)
    ckvt = _rms_cols(zt[MLA_QRANK:], gkvc_ref[...])
    vt_ref[0, :F_ONES, :] = ckvt.astype(BF16)
    vt_ref[0, F_ONES:, :] = jnp.ones((V_ROWS - F_ONES, tm), BF16)
    cqnt = _rms_cols(zt[:MLA_QRANK], gqc_ref[...]).astype(BF16)
    qt = jnp.dot(wuqt_ref[...], cqnt, preferred_element_type=F32)
    qlatt = jnp.dot(wukt_ref[...], qt[QR_NOPE:QR_ROPE].astype(BF16), preferred_element_type=F32)
    ctabt = tabt_ref[:MLA_ROPE, :]
    stabt = tabt_ref[MLA_ROPE:, :]
    for h in range(MLA_HEADS):
        qt_ref[0, h, :F_ONES, :] = (qlatt[h * MLA_KVRANK:(h + 1) * MLA_KVRANK] * MLA_SCALE).astype(BF16)
        qt_ref[0, h, F_ONES:F_ROPE, :] = jnp.zeros((F_ROPE - F_ONES, tm), BF16)
        rp = (qt[QR_ROPE + h * MLA_ROPE:QR_ROPE + (h + 1) * MLA_ROPE] * ctabt
              + qt[QR_ROPES + h * MLA_ROPE:QR_ROPES + (h + 1) * MLA_ROPE] * stabt)
        qt_ref[0, h, F_ROPE:F_ROPE + MLA_ROPE, :] = (rp * MLA_SCALE).astype(BF16)
        qt_ref[0, h, F_ROPE + MLA_ROPE:, :] = jnp.zeros((QK_PAD - F_ROPE - MLA_ROPE, tm), BF16)


def _proj(x, tab, common, extra, tm, cols):
    b, s, d = x.shape
    row = lambda w: pl.BlockSpec((1, tm, w), lambda bi, i: (bi, i, 0))
    shape = lambda *dims: jax.ShapeDtypeStruct((b,) + dims, F32)
    out_specs = [row(QK_PAD), row(MLA_KVRANK), row(MLA_ROPE), row(LRU_WIDTH), row(LRU_WIDTH)]
    out_shape = [jax.ShapeDtypeStruct((b, s, QK_PAD), BF16), shape(s, MLA_KVRANK), shape(s, MLA_ROPE),
                 shape(s, LRU_WIDTH), shape(s, LRU_WIDTH)]
    in_specs = [row(d), pl.BlockSpec((tm, 2 * LANES), lambda bi, i: (i, 0))] + [_full(a.shape) for a in common]
    if cols:
        tabt, rest = extra[0], extra[1:]
        in_specs += [pl.BlockSpec((2 * MLA_ROPE, tm), lambda bi, i: (0, i))] + [_full(a.shape) for a in rest]
        out_specs += [pl.BlockSpec((1, MLA_HEADS, QK_PAD, tm), lambda bi, i: (bi, 0, 0, i)),
                      pl.BlockSpec((1, V_ROWS, tm), lambda bi, i: (bi, 0, i))]
        out_shape += [jax.ShapeDtypeStruct((b, MLA_HEADS, QK_PAD, s), BF16),
                      jax.ShapeDtypeStruct((b, V_ROWS, s), BF16)]
    else:
        in_specs += [_full(a.shape) for a in extra]
        out_specs += [pl.BlockSpec((1, MLA_HEADS, tm, QK_PAD), lambda bi, i: (bi, 0, i, 0))]
        out_shape += [jax.ShapeDtypeStruct((b, MLA_HEADS, s, QK_PAD), BF16)]
    return pl.pallas_call(
        _proj_cols_kernel if cols else _proj_rows_kernel,
        grid=(b, s // tm),
        in_specs=in_specs,
        out_specs=tuple(out_specs),
        out_shape=tuple(out_shape),
        compiler_params=_params("parallel", "parallel"),
        name="proj_cols" if cols else "proj_rows",
    )(x, tab, *common, *extra)


def _attn_prompt_kernel(qt_ref, k_ref, vt_ref, wuvt_ref, g_ref, o_ref,
                        m_sc, acc_sc, st_sc, cmax_sc, *, qb):
    i = pl.program_id(1)
    m_sc[...] = jnp.full(m_sc.shape, -jnp.inf, F32)
    acc_sc[...] = jnp.zeros(acc_sc.shape, F32)

    def step(j, masked):
        ks = pl.ds(pl.multiple_of(j * qb, qb), qb)

        def scores(h):
            st = jnp.dot(k_ref[0, ks, :], qt_ref[0, h], preferred_element_type=F32)
            if masked:
                kc = lax.broadcasted_iota(jnp.int32, st.shape, 0) // CHUNK
                qc = lax.broadcasted_iota(jnp.int32, st.shape, 1) // CHUNK
                st = jnp.where(kc <= qc, st, -jnp.inf)
            st_sc[h] = st
            cmax_sc[h:h + 1, :] = jnp.max(st, axis=0, keepdims=True)

        for h in range(SCORE_LEAD):
            scores(h)
        for h in range(MLA_HEADS):
            if h + SCORE_LEAD < MLA_HEADS:
                scores(h + SCORE_LEAD)
            m_prev = m_sc[h:h + 1, :]
            m_new = jnp.maximum(m_prev, cmax_sc[h:h + 1, :])
            p = jnp.exp(st_sc[h] - m_new).astype(BF16)
            m_sc[h:h + 1, :] = m_new
            acc_sc[h] = (jnp.exp(m_prev - m_new) * acc_sc[h]
                         + jnp.dot(vt_ref[0, :, ks], p, preferred_element_type=F32))

    def body(j, carry):
        step(j, False)
        return carry

    lax.fori_loop(0, i, body, 0)
    step(i, True)
    parts = []
    for h in range(MLA_HEADS):
        acc = acc_sc[h]
        att = (acc[:F_ONES] / acc[F_ONES:F_ONES + 1]).astype(BF16)
        parts.append(jnp.dot(wuvt_ref[h], att, preferred_element_type=F32))
    mla = jnp.concatenate(parts, axis=0).T
    o_ref[0] = _rms(mla, g_ref[...]).astype(BF16)


def _attn_prompt(qt, kcat, vt, wuvt, g, qb):
    b, _, _, s = qt.shape
    return pl.pallas_call(
        functools.partial(_attn_prompt_kernel, qb=qb),
        grid=(b, s // qb),
        in_specs=[pl.BlockSpec((1, MLA_HEADS, QK_PAD, qb), lambda bi, i: (bi, 0, 0, i)),
                  pl.BlockSpec((1, s, QK_PAD), lambda bi, i: (bi, 0, 0)),
                  pl.BlockSpec((1, V_ROWS, s), lambda bi, i: (bi, 0, 0)),
                  _full(wuvt.shape), _full(g.shape)],
        out_specs=pl.BlockSpec((1, qb, MLA_WIDTH), lambda bi, i: (bi, i, 0)),
        out_shape=jax.ShapeDtypeStruct((b, s, MLA_WIDTH), BF16),
        scratch_shapes=[pltpu.VMEM((MLA_HEADS, qb), F32), pltpu.VMEM((MLA_HEADS, V_ROWS, qb), F32),
                        pltpu.VMEM((MLA_HEADS, qb, qb), F32), pltpu.VMEM((MLA_HEADS, qb), F32)],
        compiler_params=_params("parallel", "arbitrary"),
        name="attn_prompt",
    )(qt, kcat, vt, wuvt, g)


def _attn_sample_kernel(q_ref, pckv_ref, pkr_ref, k_ref, wuv_ref, g_ref, o_ref, *, sd, past):
    rows = MLA_HEADS * sd
    q = q_ref[0].reshape(rows, QK_PAD)
    knew = k_ref[0]
    s_past = (lax.dot_general(q[:, :LANES], pckv_ref[0].astype(BF16), _NT, preferred_element_type=F32)
              + lax.dot_general(q[:, F_ROPE:F_ROPE + MLA_ROPE], pkr_ref[0].astype(BF16), _NT,
                                preferred_element_type=F32))
    s_new = lax.dot_general(q, knew, _NT, preferred_element_type=F32)

    def mask(s, k0):
        qc = (past + lax.broadcasted_iota(jnp.int32, s.shape, 0) % sd) // CHUNK
        kc = (k0 + lax.broadcasted_iota(jnp.int32, s.shape, 1)) // CHUNK
        return jnp.where(kc <= qc, s, -jnp.inf)

    s_past = mask(s_past, 0)
    s_new = mask(s_new, past)
    m = jnp.maximum(jnp.max(s_past, axis=-1, keepdims=True), jnp.max(s_new, axis=-1, keepdims=True))
    p_past = jnp.exp(s_past - m).astype(BF16)
    p_new = jnp.exp(s_new - m).astype(BF16)
    acc = jnp.dot(p_new, knew, preferred_element_type=F32)
    num = acc[:, :LANES] + jnp.dot(p_past, pckv_ref[0].astype(BF16), preferred_element_type=F32)
    den = acc[:, F_ONES:F_ONES + 1] + jnp.sum(p_past.astype(F32), axis=-1, keepdims=True)
    att = (num / den).astype(BF16)
    mla = jnp.zeros((sd, MLA_WIDTH), F32)
    for h in range(MLA_HEADS):
        mla = mla + jnp.dot(att[h * sd:(h + 1) * sd], wuv_ref[h], preferred_element_type=F32)
    o_ref[0] = _rms(mla, g_ref[...]).astype(BF16)


def _attn_sample(q, pckv, pkr, kcat, wuv, g, nb, sd):
    past = pckv.shape[1]
    return pl.pallas_call(
        functools.partial(_attn_sample_kernel, sd=sd, past=past),
        grid=(nb,),
        in_specs=[pl.BlockSpec((1, MLA_HEADS, sd, QK_PAD), lambda b: (0, 0, b, 0)),
                  pl.BlockSpec((1, past, MLA_KVRANK), lambda b: (b, 0, 0)),
                  pl.BlockSpec((1, past, MLA_ROPE), lambda b: (b, 0, 0)),
                  pl.BlockSpec((1, sd, QK_PAD), lambda b: (0, b, 0)),
                  _full(wuv.shape), _full(g.shape)],
        out_specs=pl.BlockSpec((1, sd, MLA_WIDTH), lambda b: (0, b, 0)),
        out_shape=jax.ShapeDtypeStruct((1, nb * sd, MLA_WIDTH), BF16),
        compiler_params=_params("parallel"),
        name="attn_sample",
    )(q, pckv, pkr, kcat, wuv, g)


def _lru_kernel(xr_ref, xg_ref, buf0_ref, h0_ref, cw_ref, cb_ref, wa_ref, ba_ref, wi_ref, bi_ref,
                lam_ref, g_ref, o_ref, hlast_ref, newbuf_ref, xp_sc, h_sc, a_sc, b_sc, hh_sc, *, ts):
    t = pl.program_id(1)
    nt = pl.num_programs(1)
    pad = SUBLANES

    @pl.when(t == 0)
    def _():
        xp_sc[pad - (CONV_WIDTH - 1):pad, :] = buf0_ref[0]
        h_sc[...] = h0_ref[0]

    xr = xr_ref[0]
    xp_sc[pad:pad + ts, :] = xr
    xc = cb_ref[...] + cw_ref[CONV_WIDTH - 1:CONV_WIDTH, :] * xr
    for k in range(CONV_WIDTH - 1):
        xc = xc + cw_ref[k:k + 1, :] * xp_sc[pad - (CONV_WIDTH - 1) + k:pad - (CONV_WIDTH - 1) + k + ts, :]
    tail = xp_sc[ts + pad - (CONV_WIDTH - 1):ts + pad, :]
    xp_sc[pad - (CONV_WIDTH - 1):pad, :] = tail

    xcb = xc.astype(BF16)
    r = jax.nn.sigmoid(jnp.dot(xcb, wa_ref[...], preferred_element_type=F32) + ba_ref[...])
    ig = jax.nn.sigmoid(jnp.dot(xcb, wi_ref[...], preferred_element_type=F32) + bi_ref[...])
    log_a = -LRU_C * r * jax.nn.softplus(-lam_ref[...])
    a = jnp.exp(log_a)
    bt = jnp.sqrt(-_expm1(2.0 * log_a)) * (ig * xc)

    ng = ts // SUBLANES
    a3 = a.reshape(ng, SUBLANES, LRU_WIDTH)
    b3 = bt.reshape(ng, SUBLANES, LRU_WIDTH)
    row = lax.broadcasted_iota(jnp.int32, a3.shape, 1)
    for d in (1, 2, 4):
        valid = row >= d
        a_sh = pltpu.roll(a3, d, axis=1)
        b_sh = pltpu.roll(b3, d, axis=1)
        b3 = jnp.where(valid, a3 * b_sh + b3, b3)
        a3 = jnp.where(valid, a3 * a_sh, a3)
    a_sc[...] = a3.reshape(ts, LRU_WIDTH)
    b_sc[...] = b3.reshape(ts, LRU_WIDTH)

    def group(gi, hprev):
        rs = pl.ds(pl.multiple_of(gi * SUBLANES, SUBLANES), SUBLANES)
        hg = a_sc[rs, :] * hprev + b_sc[rs, :]
        hh_sc[rs, :] = hg
        return hg[SUBLANES - 1:SUBLANES, :]

    hfin = lax.fori_loop(0, ng, group, h_sc[...])
    h_sc[...] = hfin

    lru_out = hh_sc[...] * _gelu(xg_ref[0])
    o_ref[0] = _rms(lru_out, g_ref[...]).astype(BF16)

    @pl.when(t == nt - 1)
    def _():
        hlast_ref[0] = hfin
        newbuf_ref[0] = tail


def _lru(xr, xg, buf0, h0, cw, cb, wa, ba, wi, bi, lam, g, ts):
    b, s, w = xr.shape
    row = pl.BlockSpec((1, ts, w), lambda bi_, t: (bi_, t, 0))
    return pl.pallas_call(
        functools.partial(_lru_kernel, ts=ts),
        grid=(b, s // ts),
        in_specs=[row, row,
                  pl.BlockSpec((1, CONV_WIDTH - 1, w), lambda bi_, t: (bi_, 0, 0)),
                  pl.BlockSpec((1, 1, w), lambda bi_, t: (bi_, 0, 0)),
                  _full(cw.shape), _full(cb.shape), _full(wa.shape), _full(ba.shape),
                  _full(wi.shape), _full(bi.shape), _full(lam.shape), _full(g.shape)],
        out_specs=(row,
                   pl.BlockSpec((1, 1, w), lambda bi_, t: (bi_, 0, 0)),
                   pl.BlockSpec((1, CONV_WIDTH - 1, w), lambda bi_, t: (bi_, 0, 0))),
        out_shape=(jax.ShapeDtypeStruct((b, s, w), BF16),
                   jax.ShapeDtypeStruct((b, 1, w), F32),
                   jax.ShapeDtypeStruct((b, CONV_WIDTH - 1, w), F32)),
        scratch_shapes=[pltpu.VMEM((ts + SUBLANES, w), F32), pltpu.VMEM((1, w), F32),
                        pltpu.VMEM((ts, w), F32), pltpu.VMEM((ts, w), F32), pltpu.VMEM((ts, w), F32)],
        compiler_params=_params("parallel", "arbitrary"),
        name="lru",
    )(xr, xg, buf0, h0, cw, cb, wa, ba, wi, bi, lam, g)


def _oddeven_pairs(n):
    pairs = []

    def merge(lo, m, r):
        step = r * 2
        if step < m:
            merge(lo, m, step)
            merge(lo + r, m, step)
            pairs.extend((i, i + r) for i in range(lo + r, lo + m - r, step))
        else:
            pairs.append((lo, lo + r))

    def sort(lo, m):
        if m > 1:
            sort(lo, m // 2)
            sort(lo + m // 2, m // 2)
            merge(lo, m, 1)

    sort(0, n)
    return pairs


_SORT16 = _oddeven_pairs(16)


def _top_values(s, n):
    t = s.shape[0] // SUBLANES
    assert t <= 16 and s.shape[0] % SUBLANES == 0
    v = [s[i * SUBLANES:(i + 1) * SUBLANES] for i in range(t)]
    for i, j in _SORT16:
        if j < t:
            v[i], v[j] = jnp.maximum(v[i], v[j]), jnp.minimum(v[i], v[j])
    top = []
    for r in range(n):
        m = jnp.max(v[0], axis=0, keepdims=True)
        top.append(m)
        hit = v[0] == m
        for i in range(min(n - r - 1, t)):
            v[i] = jnp.where(hit, v[i + 1] if i + 1 < t else -jnp.inf, v[i])
    return top


def _mix_kernel(x_ref, mla_ref, lru_ref, wo_ref, gffn_ref, wqt_ref, k1_ref, k2_ref,
                x1_ref, xn2_ref, s2_ref, e2_ref, theta_ref, e1_ref, s1_sc, s2_sc):
    x1 = (x_ref[...]
          + jnp.dot(mla_ref[...], wo_ref[:MLA_WIDTH, :], preferred_element_type=F32)
          + jnp.dot(lru_ref[...], wo_ref[MLA_WIDTH:, :], preferred_element_type=F32))
    x1_ref[...] = x1
    xn2t = _rms(x1, gffn_ref[...]).T.astype(BF16)
    xn2_ref[0] = xn2t
    qt = jnp.dot(wqt_ref[...], xn2t, preferred_element_type=F32).astype(BF16)
    for h in range(PEER_HEADS):
        base = h * PEER_KEY_DIM
        s1_sc[h] = jnp.dot(k1_ref[h], qt[base:base + PEER_HALF], preferred_element_type=F32)
        s2_sc[h] = jnp.dot(k2_ref[h], qt[base + PEER_HALF:base + PEER_KEY_DIM], preferred_element_type=F32)
    k = PEER_TOPK
    groups = x1.shape[0] // LANES

    def unit(h, lane_group):
        ls = pl.ds(pl.multiple_of(lane_group * LANES, LANES), LANES)
        s1 = s1_sc[h, :, ls]
        s2 = s2_sc[h, :, ls]
        v1 = _top_values(s1, k)
        v2 = _top_values(s2, k)
        v1a = jnp.concatenate(v1, axis=0)
        cand = [v1a + v2[0]]
        cand += [v1a[:SUBLANES] + v2[j] for j in range(1, SUBLANES)]
        cand += [v1[0] + jnp.concatenate(v2[SUBLANES:], axis=0)]
        vs = _top_values(jnp.concatenate(cand, axis=0), k)
        tau = vs[k - 1]
        sel = [c >= tau for c in cand]
        z = sum(jnp.sum(jnp.where(m, jnp.exp(c - vs[0]), 0.0), axis=0, keepdims=True)
                for m, c in zip(sel, cand))
        ones = [jnp.where(m, 1.0, 0.0) for m in sel]
        low = sum(ones[1:SUBLANES])
        cnt = ones[0] + jnp.concatenate([low, jnp.zeros_like(low)], axis=0)
        tail = jnp.sum(ones[SUBLANES], axis=0, keepdims=True)
        cnt = cnt + jnp.where(lax.broadcasted_iota(jnp.int32, cnt.shape, 0) == 0, tail, 0.0)
        th_row = jnp.full(cnt.shape, jnp.inf, F32)
        for j in range(k):
            th_row = jnp.where(cnt == float(j + 1), v2[j], th_row)
        theta = jnp.full(s1.shape, jnp.inf, F32)
        for i in range(k):
            theta = jnp.where(s1 == v1[i], th_row[i:i + 1], theta)
        s2_ref[0, h, :, ls] = s2
        e2_ref[0, h, :, ls] = jnp.exp(s2 - v2[0]) / z
        theta_ref[0, h, :, ls] = theta
        e1_ref[0, h, :, ls] = jnp.exp(s1 - v1[0])

    per = 2 if groups % 2 == 0 else 1

    def body(it, carry):
        for u in range(per):
            unit(it // (groups // per), (it % (groups // per)) * per + u)
        return carry

    lax.fori_loop(0, PEER_HEADS * groups // per, body, 0)


def _mix(x, mla, lru, wo, gffn, wqt, k1, k2, tm):
    t, d = x.shape
    row = lambda w: pl.BlockSpec((tm, w), lambda i: (i, 0))
    sc = pl.BlockSpec((1, PEER_HEADS, PEER_NKEYS, tm), lambda i: (i, 0, 0, 0))
    sc_shape = lambda dt: jax.ShapeDtypeStruct((t // tm, PEER_HEADS, PEER_NKEYS, tm), dt)
    return pl.pallas_call(
        _mix_kernel,
        grid=(t // tm,),
        in_specs=[row(d), row(MLA_WIDTH), row(LRU_WIDTH), _full(wo.shape), _full(gffn.shape),
                  _full(wqt.shape), _full(k1.shape), _full(k2.shape)],
        out_specs=(row(d), pl.BlockSpec((1, d, tm), lambda i: (i, 0, 0)), sc, sc, sc, sc),
        out_shape=(jax.ShapeDtypeStruct((t, d), F32), jax.ShapeDtypeStruct((t // tm, d, tm), BF16),
                   sc_shape(F32), sc_shape(F32), sc_shape(F32), sc_shape(F32)),
        scratch_shapes=[pltpu.VMEM((PEER_HEADS, PEER_NKEYS, tm), F32)] * 2,
        compiler_params=_params("parallel"),
        name="mix",
    )(x, mla, lru, wo, gffn, wqt, k1, k2)


def _peer_kernel(xn2_ref, x1_ref, u_ref, vt_ref, s2_ref, e2_ref, theta_ref, e1_ref, gfin_ref,
                 y_ref, ht0_sc, ht1_sc, wt0_sc, wt1_sc, yt_sc, *, eb, tb, ne, final_norm):
    g = pl.program_id(0)
    out_tile = jnp.maximum(g - 2, 0) % ne

    @pl.when(g == 0)
    def _():
        for ref in (ht0_sc, ht1_sc, wt0_sc, wt1_sc):
            ref[...] = jnp.zeros(ref.shape, ref.dtype)

    @pl.when(out_tile == 0)
    def _():
        yt_sc[...] = jnp.zeros(yt_sc.shape, F32)

    na = eb // PEER_NKEYS
    mh_rows = eb
    nt_cols = 2 * LANES
    bq_rows = PEER_NKEYS // 4

    def stages(ht_cur, wt_cur, ht_prev, wt_prev):
        def pre_piece(mh, nt):
            ms = slice(mh * mh_rows, (mh + 1) * mh_rows)
            ns = slice(nt * nt_cols, (nt + 1) * nt_cols)
            ht_cur[ms, ns] = jnp.dot(u_ref[ms, :], xn2_ref[0, :, ns], preferred_element_type=F32)

        def out_piece(mh, nt):
            out_rows = yt_sc.shape[0] // (eb // mh_rows)
            ms = slice(mh * out_rows, (mh + 1) * out_rows)
            ns = slice(nt * nt_cols, (nt + 1) * nt_cols)
            yt_sc[ms, ns] += jnp.dot(vt_ref[0, ms, :], wt_cur[:, ns], preferred_element_type=F32)

        def mix_block(tl, bq):
            ls = slice(tl * LANES, (tl + 1) * LANES)
            bs = slice(bq * bq_rows, (bq + 1) * bq_rows)
            acc = [jnp.zeros((bq_rows, LANES), F32)] * na
            for h in range(PEER_HEADS):
                s2 = s2_ref[0, h, bs, ls]
                e2 = e2_ref[0, h, bs, ls]
                for al in range(na):
                    hit = s2 >= theta_ref[0, h, al:al + 1, ls]
                    acc[al] = acc[al] + jnp.where(hit, e1_ref[0, h, al:al + 1, ls] * e2, 0.0)
            for al in range(na):
                rs = slice(al * PEER_NKEYS + bq * bq_rows, al * PEER_NKEYS + (bq + 1) * bq_rows)
                wt_prev[rs, ls] = (acc[al] * _gelu(ht_prev[rs, ls])).astype(BF16)

        pieces = [(f, mh, nt) for nt in range(tb // nt_cols) for mh in range(eb // mh_rows)
                  for f in (pre_piece, out_piece)]
        blocks = [(tl, bq) for tl in range(tb // LANES) for bq in range(PEER_NKEYS // bq_rows)]
        per = -(-len(blocks) // len(pieces))
        for i, (f, mh, nt) in enumerate(pieces):
            f(mh, nt)
            for tl, bq in blocks[i * per:(i + 1) * per]:
                mix_block(tl, bq)

    @pl.when(g % 2 == 0)
    def _():
        stages(ht0_sc, wt0_sc, ht1_sc, wt1_sc)

    @pl.when(g % 2 == 1)
    def _():
        stages(ht1_sc, wt1_sc, ht0_sc, wt0_sc)

    @pl.when((g >= 2) & (out_tile == ne - 1))
    def _():
        y = x1_ref[...] + yt_sc[...].T
        if final_norm:
            y = _rms(y, gfin_ref[...])
        y_ref[...] = y


def _peer(xn2, x1, u, vt, s2, e2, theta, e1, gfin, tb, eb, final_norm):
    t, d = x1.shape
    ne = u.shape[0] // eb
    na = eb // PEER_NKEYS
    steps = (t // tb) * ne
    assert na == SUBLANES and u.shape[0] % eb == 0 and tb % (2 * LANES) == 0 and ne > 1

    def at(lag):
        def pos(g):
            s = jnp.clip(g - lag, 0, steps - 1)
            return s // ne, s % ne
        return pos

    pre, mid, out = at(0), at(1), at(2)
    sc = pl.BlockSpec((1, PEER_HEADS, PEER_NKEYS, tb), lambda g: (mid(g)[0], 0, 0, 0))
    rows = pl.BlockSpec((1, PEER_HEADS, na, tb), lambda g: (mid(g)[0], 0, mid(g)[1], 0))
    return pl.pallas_call(
        functools.partial(_peer_kernel, eb=eb, tb=tb, ne=ne, final_norm=final_norm),
        grid=(steps + 2,),
        in_specs=[pl.BlockSpec((1, d, tb), lambda g: (pre(g)[0], 0, 0)),
                  pl.BlockSpec((tb, d), lambda g: (out(g)[0], 0)),
                  pl.BlockSpec((eb, d), lambda g: (pre(g)[1], 0)),
                  pl.BlockSpec((1, d, eb), lambda g: (out(g)[1], 0, 0)),
                  sc, sc, rows, rows,
                  _full(gfin.shape)],
        out_specs=pl.BlockSpec((tb, d), lambda g: (out(g)[0], 0)),
        out_shape=jax.ShapeDtypeStruct((t, d), F32),
        scratch_shapes=[pltpu.VMEM((eb, tb), F32), pltpu.VMEM((eb, tb), F32),
                        pltpu.VMEM((eb, tb), BF16), pltpu.VMEM((eb, tb), BF16), pltpu.VMEM((d, tb), F32)],
        compiler_params=_params("arbitrary"),
        name="peer",
    )(xn2, x1, u, vt, s2, e2, theta, e1, gfin)


def _rope_tables(pos):
    half = MLA_ROPE // 2
    inv = ROPE_THETA ** (-jnp.arange(half, dtype=F32) / half)
    ang = pos.astype(F32)[:, None] * inv[None, :]
    cos, sin = jnp.cos(ang), jnp.sin(ang)
    c2 = jnp.concatenate([cos, cos], axis=-1)
    s2 = jnp.concatenate([-sin, sin], axis=-1)
    rows = jnp.concatenate([_place_cols(c2), _place_cols(s2)], axis=-1)
    return rows, jnp.concatenate([c2, s2], axis=-1).T


def _swap_halves(w):
    half = w.shape[-1] // 2
    return jnp.concatenate([w[..., half:], w[..., :half]], axis=-1)


def _place_cols(w):
    pad = [(0, 0)] * (w.ndim - 1) + [(ROPE_OFF, LANES - ROPE_OFF - w.shape[-1])]
    return jnp.pad(w, pad)


def _layer_weights(w_in, w_uq, w_uk, w_uv, lru_wa, lru_wi, w_out, peer_wq, peer_keys1, peer_keys2,
                   peer_u, peer_v):
    o1 = MLA_QRANK
    o2 = o1 + MLA_KVRANK
    o3 = o2 + MLA_ROPE
    kr = w_in[:, o2:o3]
    win = jnp.concatenate([w_in[:, :o2], _place_cols(kr), _place_cols(_swap_halves(kr)),
                           w_in[:, o3:]], axis=1).astype(BF16)
    wint = w_in[:, :o2].T.astype(BF16)
    wq3 = w_uq.reshape(MLA_QRANK, MLA_HEADS, MLA_NOPE + MLA_ROPE)
    nope = wq3[:, :, :MLA_NOPE].reshape(MLA_QRANK, MLA_HEADS * MLA_NOPE)
    rp = wq3[:, :, MLA_NOPE:]
    flat = lambda w: w.reshape(MLA_QRANK, -1)
    wuq = jnp.concatenate([nope, flat(_place_cols(rp)), flat(_place_cols(_swap_halves(rp)))], axis=1).astype(BF16)
    wuqt = jnp.concatenate([nope, flat(rp), flat(_swap_halves(rp))], axis=1).T.astype(BF16)
    eye_h = jnp.eye(MLA_HEADS, dtype=F32)
    wuk = jnp.einsum("rhd,hg->hdgr", w_uk, eye_h).reshape(MLA_HEADS * MLA_NOPE, MLA_HEADS * MLA_KVRANK)
    wuv = jnp.einsum("rhd,hg->hrgd", w_uv, eye_h).reshape(MLA_HEADS, MLA_KVRANK, MLA_WIDTH)
    wuvt = jnp.transpose(w_uv, (1, 2, 0))
    eye_b = jnp.eye(LRU_BLOCKS, dtype=F32)
    bd = lambda w: jnp.einsum("nde,nm->ndme", w, eye_b).reshape(LRU_WIDTH, LRU_WIDTH)
    return dict(win=win, wint=wint, wuq=wuq, wuqt=wuqt, wuk=wuk.astype(BF16), wukt=wuk.T.astype(BF16),
                wuv=wuv.astype(BF16), wuvt=wuvt.astype(BF16),
                wa=bd(lru_wa).astype(BF16), wi=bd(lru_wi).astype(BF16), wo=w_out.astype(BF16),
                wqt=peer_wq.T.astype(BF16), k1=peer_keys1.astype(BF16), k2=peer_keys2.astype(BF16),
                u=peer_u.astype(BF16),
                vt=peer_v.reshape(-1, PEER_TILE, peer_v.shape[-1]).transpose(0, 2, 1).astype(BF16))


def _col(v):
    return v.reshape(-1, 1).astype(F32)


def _row(v):
    return v.reshape(1, -1).astype(F32)


def _token_tile(n, cap):
    t = min(n, cap)
    assert n % t == 0, (n, t)
    return t


def kernel(x_prompt, x_sample, cache_mla_ckv, cache_mla_krope, state_lru_h, state_lru_conv, norm_mix, w_in, norm_q, w_uq, norm_kv, w_uk, w_uv, conv_w, conv_b, lru_wa, lru_ba, lru_wi, lru_bi, lru_lambda, norm_mla_out, norm_lru_out, w_out, norm_ffn, peer_wq, peer_keys1, peer_keys2, peer_u, peer_v, norm_final):
    bp, sp, d = x_prompt.shape
    bs, sd, _ = x_sample.shape
    depth = w_in.shape[0]
    past = cache_mla_ckv.shape[2]
    ts_tok = bs * sd
    tab_p, tabt_p = _rope_tables(jnp.arange(sp))
    tab_s = jnp.tile(_rope_tables(past + jnp.arange(sd))[0], (bs, 1))
    gfin = _row(norm_final)

    xp = x_prompt
    xs = x_sample.reshape(1, ts_tok, d)
    outs = [[] for _ in range(8)]
    for l in range(depth):
        w = _layer_weights(w_in[l], w_uq[l], w_uk[l], w_uv[l], lru_wa[l], lru_wi[l], w_out[l], peer_wq[l],
                           peer_keys1[l], peer_keys2[l], peer_u[l], peer_v[l])
        last = l == depth - 1
        lru_args = (conv_w[l].astype(F32), _row(conv_b[l]), w["wa"], _row(lru_ba[l]), w["wi"], _row(lru_bi[l]),
                    _row(lru_lambda[l]), _row(norm_lru_out[l]))
        proj_common = (_row(norm_mix[l]), w["win"], _row(norm_kv[l]))
        proj_rows = (_row(norm_q[l]), w["wuq"], w["wuk"])
        proj_cols = (tabt_p, w["wint"], _col(norm_q[l]), _col(norm_kv[l]), w["wuqt"], w["wukt"])
        gmla = _row(norm_mla_out[l])

        def tail(x2d, mla, lru):
            t = x2d.shape[0]
            tm = _token_tile(t, TOKEN_TILE)
            x1, xn2, s2, e2, theta, e1 = _mix(x2d, mla, lru, w["wo"], _row(norm_ffn[l]), w["wqt"],
                                               w["k1"], w["k2"], tm)
            return _peer(xn2, x1, w["u"], w["vt"], s2, e2, theta, e1, gfin, tm, PEER_TILE, last)

        kcat, ckv, kr, xr, xg, qt, vt = _proj(xp, tab_p, proj_common, proj_cols, _token_tile(sp, TOKEN_TILE), True)
        mla = _attn_prompt(qt, kcat, vt, w["wuvt"], gmla, _token_tile(sp, TOKEN_TILE))
        lru, hl, nb = _lru(xr, xg, jnp.zeros((bp, CONV_WIDTH - 1, LRU_WIDTH), F32),
                           jnp.zeros((bp, 1, LRU_WIDTH), F32), *lru_args, _token_tile(sp, TOKEN_TILE))
        xp = tail(xp.reshape(bp * sp, d), mla.reshape(bp * sp, MLA_WIDTH),
                  lru.reshape(bp * sp, LRU_WIDTH)).reshape(bp, sp, d)
        for lst, v in zip(outs[:4], (ckv, kr, hl[:, 0], nb)):
            lst.append(v)

        kcat, ckv, kr, xr, xg, q = _proj(xs, tab_s, proj_common, proj_rows, _token_tile(ts_tok, TOKEN_TILE), False)
        mla = _attn_sample(q, cache_mla_ckv[l], cache_mla_krope[l], kcat, w["wuv"], gmla, bs, sd)
        lru, hl, nb = _lru(xr.reshape(bs, sd, LRU_WIDTH), xg.reshape(bs, sd, LRU_WIDTH),
                           state_lru_conv[l].astype(F32), state_lru_h[l].reshape(bs, 1, LRU_WIDTH).astype(F32),
                           *lru_args, sd)
        xs = tail(xs[0], mla[0], lru.reshape(ts_tok, LRU_WIDTH)).reshape(1, ts_tok, d)
        for lst, v in zip(outs[4:], (ckv.reshape(bs, sd, MLA_KVRANK), kr.reshape(bs, sd, MLA_ROPE), hl[:, 0], nb)):
            lst.append(v)

    return (xp, xs.reshape(bs, sd, d)) + tuple(jnp.stack(o) for o in outs)
```

```python
import functools

import jax
import jax.numpy as jnp
from jax import lax
from jax.experimental import pallas as pl
from jax.experimental.pallas import tpu as pltpu

F32 = jnp.float32
BF16 = jnp.bfloat16

LANES = 128
SUBLANES = 8
PACKED_ROWS = 2 * SUBLANES
CHUNK = 64
EPS = 1e-6
MLA_HEADS = 8
MLA_NOPE = 64
MLA_ROPE = 32
MLA_VDIM = 64
MLA_QRANK = 256
MLA_KVRANK = 128
ROPE_THETA = 10000.0
MLA_WIDTH = MLA_HEADS * MLA_VDIM
MLA_SCALE = (MLA_NOPE + MLA_ROPE) ** -0.5
QK_PAD = 256
LRU_WIDTH = 512
LRU_BLOCKS = 8
CONV_WIDTH = 4
LRU_C = 8.0
PEER_HEADS = 8
PEER_NKEYS = 128
PEER_KEY_DIM = 256
PEER_HALF = PEER_KEY_DIM // 2
PEER_TOPK = 16
TOKEN_TILE = 512
SCORE_LEAD = 2
PEER_TILE = 8 * PEER_NKEYS
VMEM_LIMIT = 56 * 1024 * 1024

F_ONES = MLA_KVRANK
F_ROPE = F_ONES + PACKED_ROWS
V_ROWS = F_ROPE
ROPE_OFF = F_ROPE - LANES
ZC_CQ = 0
ZC_CKV = ZC_CQ + MLA_QRANK
ZC_KR = ZC_CKV + MLA_KVRANK
ZC_KRS = ZC_KR + LANES
ZC_XR = ZC_KRS + LANES
ZC_XG = ZC_XR + LRU_WIDTH
ZC_END = ZC_XG + LRU_WIDTH
QC_NOPE = 0
QC_ROPE = QC_NOPE + MLA_HEADS * MLA_NOPE
QC_ROPES = QC_ROPE + MLA_HEADS * LANES
QR_NOPE = 0
QR_ROPE = QR_NOPE + MLA_HEADS * MLA_NOPE
QR_ROPES = QR_ROPE + MLA_HEADS * MLA_ROPE

_NT = (((1,), (1,)), ((), ()))


def _rms(x, g):
    return x * lax.rsqrt(jnp.mean(x * x, axis=-1, keepdims=True) + EPS) * g


def _rms_cols(xt, g):
    return xt * lax.rsqrt(jnp.mean(xt * xt, axis=0, keepdims=True) + EPS) * g


def _expm1(y):
    u = jnp.exp(y)
    um1 = u - 1.0
    return jnp.where(um1 == 0.0, y, jnp.where(um1 == -1.0, -1.0, um1 * y / jnp.log(u)))


def _gelu(x):
    return 0.5 * x * (1.0 + lax.erf(x * (2.0 ** -0.5)))


def _gelu_twice(x):
    return x * (1.0 + lax.erf(x * (2.0 ** -0.5)))


def _params(*sem):
    return pltpu.CompilerParams(dimension_semantics=sem, vmem_limit_bytes=VMEM_LIMIT)


def _full(shape):
    n = len(shape)
    return pl.BlockSpec(shape, lambda *_: (0,) * n)


def _proj_common(x_ref, tab_ref, gmix_ref, win_ref, gkv_ref, kcat_ref, ckv_ref, kr_ref, xr_ref, xg_ref):
    xn = _rms(x_ref[0], gmix_ref[...]).astype(BF16)
    z = jnp.dot(xn, win_ref[...], preferred_element_type=F32)
    ctab = tab_ref[:, :LANES]
    stab = tab_ref[:, LANES:]
    ckv = _rms(z[:, ZC_CKV:ZC_KR], gkv_ref[...])
    kr = z[:, ZC_KR:ZC_KRS] * ctab + z[:, ZC_KRS:ZC_XR] * stab
    ckv_ref[0] = ckv
    kr_ref[0] = kr[:, ROPE_OFF:ROPE_OFF + MLA_ROPE]
    kcat_ref[0, :, :LANES] = ckv.astype(BF16)
    lane = lax.broadcasted_iota(jnp.int32, kr.shape, 1)
    kcat_ref[0, :, LANES:] = jnp.where(lane < ROPE_OFF, 1.0, kr).astype(BF16)
    xr_ref[0] = z[:, ZC_XR:ZC_XG]
    xg_ref[0] = z[:, ZC_XG:ZC_END]
    return xn, z, ctab, stab


def _proj_rows_kernel(x_ref, tab_ref, gmix_ref, win_ref, gkv_ref, gq_ref, wuq_ref, wuk_ref,
                      kcat_ref, ckv_ref, kr_ref, xr_ref, xg_ref, q_ref):
    _, z, ctab, stab = _proj_common(x_ref, tab_ref, gmix_ref, win_ref, gkv_ref,
                                    kcat_ref, ckv_ref, kr_ref, xr_ref, xg_ref)
    cqn = _rms(z[:, ZC_CQ:ZC_CKV], gq_ref[...]).astype(BF16)
    q = jnp.dot(cqn, wuq_ref[...], preferred_element_type=F32)
    qlat = jnp.dot(q[:, QC_NOPE:QC_ROPE].astype(BF16), wuk_ref[...], preferred_element_type=F32)
    for h in range(MLA_HEADS):
        q_ref[0, h, :, :LANES] = (qlat[:, h * LANES:(h + 1) * LANES] * MLA_SCALE).astype(BF16)
        rp = (q[:, QC_ROPE + h * LANES:QC_ROPE + (h + 1) * LANES] * ctab
              + q[:, QC_ROPES + h * LANES:QC_ROPES + (h + 1) * LANES] * stab)
        q_ref[0, h, :, LANES:] = (rp * MLA_SCALE).astype(BF16)


def _proj_cols_kernel(x_ref, tab_ref, gmix_ref, win_ref, gkv_ref, tabt_ref, wint_ref, gqc_ref, gkvc_ref,
                      wuqt_ref, wukt_ref, kcat_ref, ckv_ref, kr_ref, xr_ref, xg_ref, qt_ref, vt_ref):
    xn, _, _, _ = _proj_common(x_ref, tab_ref, gmix_ref, win_ref, gkv_ref,
                               kcat_ref, ckv_ref, kr_ref, xr_ref, xg_ref)
    tm = xn.shape[0]
    zt = lax.dot_general(wint_ref[...], xn, _NT, preferred_element_type=F32)
    ckvt = _rms_cols(zt[MLA_QRANK:], gkvc_ref[...])
    vt_ref[0, :F_ONES, :] = ckvt.astype(BF16)
    vt_ref[0, F_ONES:, :] = jnp.ones((V_ROWS - F_ONES, tm), BF16)
    cqnt = _rms_cols(zt[:MLA_QRANK], gqc_ref[...]).astype(BF16)
    qt = jnp.dot(wuqt_ref[...], cqnt, preferred_element_type=F32)
    qlatt = jnp.dot(wukt_ref[...], qt[QR_NOPE:QR_ROPE].astype(BF16), preferred_element_type=F32)
    ctabt = tabt_ref[:MLA_ROPE, :]
    stabt = tabt_ref[MLA_ROPE:, :]
    for h in range(MLA_HEADS):
        qt_ref[0, h, :F_ONES, :] = (qlatt[h * MLA_KVRANK:(h + 1) * MLA_KVRANK] * MLA_SCALE).astype(BF16)
        qt_ref[0, h, F_ONES:F_ROPE, :] = jnp.zeros((F_ROPE - F_ONES, tm), BF16)
        rp = (qt[QR_ROPE + h * MLA_ROPE:QR_ROPE + (h + 1) * MLA_ROPE] * ctabt
              + qt[QR_ROPES + h * MLA_ROPE:QR_ROPES + (h + 1) * MLA_ROPE] * stabt)
        qt_ref[0, h, F_ROPE:F_ROPE + MLA_ROPE, :] = (rp * MLA_SCALE).astype(BF16)
        qt_ref[0, h, F_ROPE + MLA_ROPE:, :] = jnp.zeros((QK_PAD - F_ROPE - MLA_ROPE, tm), BF16)


def _proj(x, tab, common, extra, tm, cols):
    b, s, d = x.shape
    row = lambda w: pl.BlockSpec((1, tm, w), lambda bi, i: (bi, i, 0))
    shape = lambda *dims: jax.ShapeDtypeStruct((b,) + dims, F32)
    out_specs = [row(QK_PAD), row(MLA_KVRANK), row(MLA_ROPE), row(LRU_WIDTH), row(LRU_WIDTH)]
    out_shape = [jax.ShapeDtypeStruct((b, s, QK_PAD), BF16), shape(s, MLA_KVRANK), shape(s, MLA_ROPE),
                 shape(s, LRU_WIDTH), shape(s, LRU_WIDTH)]
    in_specs = [row(d), pl.BlockSpec((tm, 2 * LANES), lambda bi, i: (i, 0))] + [_full(a.shape) for a in common]
    if cols:
        tabt, rest = extra[0], extra[1:]
        in_specs += [pl.BlockSpec((2 * MLA_ROPE, tm), lambda bi, i: (0, i))] + [_full(a.shape) for a in rest]
        out_specs += [pl.BlockSpec((1, MLA_HEADS, QK_PAD, tm), lambda bi, i: (bi, 0, 0, i)),
                      pl.BlockSpec((1, V_ROWS, tm), lambda bi, i: (bi, 0, i))]
        out_shape += [jax.ShapeDtypeStruct((b, MLA_HEADS, QK_PAD, s), BF16),
                      jax.ShapeDtypeStruct((b, V_ROWS, s), BF16)]
    else:
        in_specs += [_full(a.shape) for a in extra]
        out_specs += [pl.BlockSpec((1, MLA_HEADS, tm, QK_PAD), lambda bi, i: (bi, 0, i, 0))]
        out_shape += [jax.ShapeDtypeStruct((b, MLA_HEADS, s, QK_PAD), BF16)]
    return pl.pallas_call(
        _proj_cols_kernel if cols else _proj_rows_kernel,
        grid=(b, s // tm),
        in_specs=in_specs,
        out_specs=tuple(out_specs),
        out_shape=tuple(out_shape),
        compiler_params=_params("parallel", "parallel"),
        name="proj_cols" if cols else "proj_rows",
    )(x, tab, *common, *extra)


def _attn_prompt_kernel(qt_ref, k_ref, vt_ref, wuvt_ref, g_ref, o_ref,
                        m_sc, acc_sc, st_sc, cmax_sc, *, qb):
    i = pl.program_id(1)
    m_sc[...] = jnp.full(m_sc.shape, -jnp.inf, F32)
    acc_sc[...] = jnp.zeros(acc_sc.shape, F32)

    def step(j, masked):
        ks = pl.ds(pl.multiple_of(j * qb, qb), qb)

        def scores(h):
            st = jnp.dot(k_ref[0, ks, :], qt_ref[0, h], preferred_element_type=F32)
            if masked:
                kc = lax.broadcasted_iota(jnp.int32, st.shape, 0) // CHUNK
                qc = lax.broadcasted_iota(jnp.int32, st.shape, 1) // CHUNK
                st = jnp.where(kc <= qc, st, -jnp.inf)
            st_sc[h] = st
            cmax_sc[h:h + 1, :] = jnp.max(st, axis=0, keepdims=True)

        for h in range(SCORE_LEAD):
            scores(h)
        for h in range(MLA_HEADS):
            if h + SCORE_LEAD < MLA_HEADS:
                scores(h + SCORE_LEAD)
            m_prev = m_sc[h:h + 1, :]
            m_new = jnp.maximum(m_prev, cmax_sc[h:h + 1, :])
            p = jnp.exp(st_sc[h] - m_new).astype(BF16)
            m_sc[h:h + 1, :] = m_new
            acc_sc[h] = (jnp.exp(m_prev - m_new) * acc_sc[h]
                         + jnp.dot(vt_ref[0, :, ks], p, preferred_element_type=F32))

    def body(j, carry):
        step(j, False)
        return carry

    lax.fori_loop(0, i, body, 0)
    step(i, True)
    parts = []
    for h in range(MLA_HEADS):
        acc = acc_sc[h]
        att = (acc[:F_ONES] / acc[F_ONES:F_ONES + 1]).astype(BF16)
        parts.append(jnp.dot(wuvt_ref[h], att, preferred_element_type=F32))
    mla = jnp.concatenate(parts, axis=0).T
    o_ref[0] = _rms(mla, g_ref[...]).astype(BF16)


def _attn_prompt(qt, kcat, vt, wuvt, g, qb):
    b, _, _, s = qt.shape
    return pl.pallas_call(
        functools.partial(_attn_prompt_kernel, qb=qb),
        grid=(b, s // qb),
        in_specs=[pl.BlockSpec((1, MLA_HEADS, QK_PAD, qb), lambda bi, i: (bi, 0, 0, i)),
                  pl.BlockSpec((1, s, QK_PAD), lambda bi, i: (bi, 0, 0)),
                  pl.BlockSpec((1, V_ROWS, s), lambda bi, i: (bi, 0, 0)),
                  _full(wuvt.shape), _full(g.shape)],
        out_specs=pl.BlockSpec((1, qb, MLA_WIDTH), lambda bi, i: (bi, i, 0)),
        out_shape=jax.ShapeDtypeStruct((b, s, MLA_WIDTH), BF16),
        scratch_shapes=[pltpu.VMEM((MLA_HEADS, qb), F32), pltpu.VMEM((MLA_HEADS, V_ROWS, qb), F32),
                        pltpu.VMEM((MLA_HEADS, qb, qb), F32), pltpu.VMEM((MLA_HEADS, qb), F32)],
        compiler_params=_params("parallel", "arbitrary"),
        name="attn_prompt",
    )(qt, kcat, vt, wuvt, g)


def _attn_sample_kernel(q_ref, pckv_ref, pkr_ref, k_ref, wuv_ref, g_ref, o_ref, *, sd, past):
    rows = MLA_HEADS * sd
    q = q_ref[0].reshape(rows, QK_PAD)
    knew = k_ref[0]
    s_past = (lax.dot_general(q[:, :LANES], pckv_ref[0].astype(BF16), _NT, preferred_element_type=F32)
              + lax.dot_general(q[:, F_ROPE:F_ROPE + MLA_ROPE], pkr_ref[0].astype(BF16), _NT,
                                preferred_element_type=F32))
    s_new = lax.dot_general(q, knew, _NT, preferred_element_type=F32)

    def mask(s, k0):
        qc = (past + lax.broadcasted_iota(jnp.int32, s.shape, 0) % sd) // CHUNK
        kc = (k0 + lax.broadcasted_iota(jnp.int32, s.shape, 1)) // CHUNK
        return jnp.where(kc <= qc, s, -jnp.inf)

    s_past = mask(s_past, 0)
    s_new = mask(s_new, past)
    m = jnp.maximum(jnp.max(s_past, axis=-1, keepdims=True), jnp.max(s_new, axis=-1, keepdims=True))
    p_past = jnp.exp(s_past - m).astype(BF16)
    p_new = jnp.exp(s_new - m).astype(BF16)
    acc = jnp.dot(p_new, knew, preferred_element_type=F32)
    num = acc[:, :LANES] + jnp.dot(p_past, pckv_ref[0].astype(BF16), preferred_element_type=F32)
    den = acc[:, F_ONES:F_ONES + 1] + jnp.sum(p_past.astype(F32), axis=-1, keepdims=True)
    att = (num / den).astype(BF16)
    mla = jnp.zeros((sd, MLA_WIDTH), F32)
    for h in range(MLA_HEADS):
        mla = mla + jnp.dot(att[h * sd:(h + 1) * sd], wuv_ref[h], preferred_element_type=F32)
    o_ref[0] = _rms(mla, g_ref[...]).astype(BF16)


def _attn_sample(q, pckv, pkr, kcat, wuv, g, nb, sd):
    past = pckv.shape[1]
    return pl.pallas_call(
        functools.partial(_attn_sample_kernel, sd=sd, past=past),
        grid=(nb,),
        in_specs=[pl.BlockSpec((1, MLA_HEADS, sd, QK_PAD), lambda b: (0, 0, b, 0)),
                  pl.BlockSpec((1, past, MLA_KVRANK), lambda b: (b, 0, 0)),
                  pl.BlockSpec((1, past, MLA_ROPE), lambda b: (b, 0, 0)),
                  pl.BlockSpec((1, sd, QK_PAD), lambda b: (0, b, 0)),
                  _full(wuv.shape), _full(g.shape)],
        out_specs=pl.BlockSpec((1, sd, MLA_WIDTH), lambda b: (0, b, 0)),
        out_shape=jax.ShapeDtypeStruct((1, nb * sd, MLA_WIDTH), BF16),
        compiler_params=_params("parallel"),
        name="attn_sample",
    )(q, pckv, pkr, kcat, wuv, g)


def _lru_kernel(xr_ref, xg_ref, buf0_ref, h0_ref, cw_ref, cb_ref, wa_ref, ba_ref, wi_ref, bi_ref,
                lam_ref, g_ref, o_ref, hlast_ref, newbuf_ref, xp_sc, h_sc, a_sc, b_sc, hh_sc, *, ts):
    t = pl.program_id(1)
    nt = pl.num_programs(1)
    pad = SUBLANES

    @pl.when(t == 0)
    def _():
        xp_sc[pad - (CONV_WIDTH - 1):pad, :] = buf0_ref[0]
        h_sc[...] = h0_ref[0]

    xr = xr_ref[0]
    xp_sc[pad:pad + ts, :] = xr
    xc = cb_ref[...] + cw_ref[CONV_WIDTH - 1:CONV_WIDTH, :] * xr
    for k in range(CONV_WIDTH - 1):
        xc = xc + cw_ref[k:k + 1, :] * xp_sc[pad - (CONV_WIDTH - 1) + k:pad - (CONV_WIDTH - 1) + k + ts, :]
    tail = xp_sc[ts + pad - (CONV_WIDTH - 1):ts + pad, :]
    xp_sc[pad - (CONV_WIDTH - 1):pad, :] = tail

    xcb = xc.astype(BF16)
    r = jax.nn.sigmoid(jnp.dot(xcb, wa_ref[...], preferred_element_type=F32) + ba_ref[...])
    ig = jax.nn.sigmoid(jnp.dot(xcb, wi_ref[...], preferred_element_type=F32) + bi_ref[...])
    log_a = -LRU_C * r * jax.nn.softplus(-lam_ref[...])
    a = jnp.exp(log_a)
    bt = jnp.sqrt(-_expm1(2.0 * log_a)) * (ig * xc)

    ng = ts // SUBLANES
    a3 = a.reshape(ng, SUBLANES, LRU_WIDTH)
    b3 = bt.reshape(ng, SUBLANES, LRU_WIDTH)
    row = lax.broadcasted_iota(jnp.int32, a3.shape, 1)
    for d in (1, 2, 4):
        valid = row >= d
        a_sh = pltpu.roll(a3, d, axis=1)
        b_sh = pltpu.roll(b3, d, axis=1)
        b3 = jnp.where(valid, a3 * b_sh + b3, b3)
        a3 = jnp.where(valid, a3 * a_sh, a3)
    a_sc[...] = a3.reshape(ts, LRU_WIDTH)
    b_sc[...] = b3.reshape(ts, LRU_WIDTH)

    def group(gi, hprev):
        rs = pl.ds(pl.multiple_of(gi * SUBLANES, SUBLANES), SUBLANES)
        hg = a_sc[rs, :] * hprev + b_sc[rs, :]
        hh_sc[rs, :] = hg
        return hg[SUBLANES - 1:SUBLANES, :]

    hfin = lax.fori_loop(0, ng, group, h_sc[...])
    h_sc[...] = hfin

    lru_out = hh_sc[...] * _gelu(xg_ref[0])
    o_ref[0] = _rms(lru_out, g_ref[...]).astype(BF16)

    @pl.when(t == nt - 1)
    def _():
        hlast_ref[0] = hfin
        newbuf_ref[0] = tail


def _lru(xr, xg, buf0, h0, cw, cb, wa, ba, wi, bi, lam, g, ts):
    b, s, w = xr.shape
    row = pl.BlockSpec((1, ts, w), lambda bi_, t: (bi_, t, 0))
    return pl.pallas_call(
        functools.partial(_lru_kernel, ts=ts),
        grid=(b, s // ts),
        in_specs=[row, row,
                  pl.BlockSpec((1, CONV_WIDTH - 1, w), lambda bi_, t: (bi_, 0, 0)),
                  pl.BlockSpec((1, 1, w), lambda bi_, t: (bi_, 0, 0)),
                  _full(cw.shape), _full(cb.shape), _full(wa.shape), _full(ba.shape),
                  _full(wi.shape), _full(bi.shape), _full(lam.shape), _full(g.shape)],
        out_specs=(row,
                   pl.BlockSpec((1, 1, w), lambda bi_, t: (bi_, 0, 0)),
                   pl.BlockSpec((1, CONV_WIDTH - 1, w), lambda bi_, t: (bi_, 0, 0))),
        out_shape=(jax.ShapeDtypeStruct((b, s, w), BF16),
                   jax.ShapeDtypeStruct((b, 1, w), F32),
                   jax.ShapeDtypeStruct((b, CONV_WIDTH - 1, w), F32)),
        scratch_shapes=[pltpu.VMEM((ts + SUBLANES, w), F32), pltpu.VMEM((1, w), F32),
                        pltpu.VMEM((ts, w), F32), pltpu.VMEM((ts, w), F32), pltpu.VMEM((ts, w), F32)],
        compiler_params=_params("parallel", "arbitrary"),
        name="lru",
    )(xr, xg, buf0, h0, cw, cb, wa, ba, wi, bi, lam, g)


def _oddeven_pairs(n):
    pairs = []

    def merge(lo, m, r):
        step = r * 2
        if step < m:
            merge(lo, m, step)
            merge(lo + r, m, step)
            pairs.extend((i, i + r) for i in range(lo + r, lo + m - r, step))
        else:
            pairs.append((lo, lo + r))

    def sort(lo, m):
        if m > 1:
            sort(lo, m // 2)
            sort(lo + m // 2, m // 2)
            merge(lo, m, 1)

    sort(0, n)
    return pairs


_SORT16 = _oddeven_pairs(16)


def _top_values(s, n):
    t = s.shape[0] // SUBLANES
    assert t <= 16 and s.shape[0] % SUBLANES == 0
    v = [s[i * SUBLANES:(i + 1) * SUBLANES] for i in range(t)]
    for i, j in _SORT16:
        if j < t:
            v[i], v[j] = jnp.maximum(v[i], v[j]), jnp.minimum(v[i], v[j])
    top = []
    for r in range(n):
        m = jnp.max(v[0], axis=0, keepdims=True)
        top.append(m)
        hit = v[0] == m
        for i in range(min(n - r - 1, t)):
            v[i] = jnp.where(hit, v[i + 1] if i + 1 < t else -jnp.inf, v[i])
    return top


def _mix_kernel(x_ref, mla_ref, lru_ref, wo_ref, gffn_ref, wqt_ref, k1_ref, k2_ref,
                x1_ref, xn2_ref, s2_ref, e2_ref, theta_ref, e1_ref, s1_sc, s2_sc):
    x1 = (x_ref[...]
          + jnp.dot(mla_ref[...], wo_ref[:MLA_WIDTH, :], preferred_element_type=F32)
          + jnp.dot(lru_ref[...], wo_ref[MLA_WIDTH:, :], preferred_element_type=F32))
    x1_ref[...] = x1
    xn2t = _rms(x1, gffn_ref[...]).T.astype(BF16)
    xn2_ref[...] = xn2t
    qt = jnp.dot(wqt_ref[...], xn2t, preferred_element_type=F32).astype(BF16)
    for h in range(PEER_HEADS):
        base = h * PEER_KEY_DIM
        s1_sc[h] = jnp.dot(k1_ref[h], qt[base:base + PEER_HALF], preferred_element_type=F32)
        s2_sc[h] = jnp.dot(k2_ref[h], qt[base + PEER_HALF:base + PEER_KEY_DIM], preferred_element_type=F32)
    k = PEER_TOPK
    groups = x1.shape[0] // LANES

    def unit(h, lane_group):
        ls = pl.ds(pl.multiple_of(lane_group * LANES, LANES), LANES)
        s1 = s1_sc[h, :, ls]
        s2 = s2_sc[h, :, ls]
        v1 = _top_values(s1, k)
        v2 = _top_values(s2, k)
        v1a = jnp.concatenate(v1, axis=0)
        cand = [v1a + v2[0]]
        cand += [v1a[:SUBLANES] + v2[j] for j in range(1, SUBLANES)]
        cand += [v1[0] + jnp.concatenate(v2[SUBLANES:], axis=0)]
        vs = _top_values(jnp.concatenate(cand, axis=0), k)
        tau = vs[k - 1]
        sel = [c >= tau for c in cand]
        z = sum(jnp.sum(jnp.where(m, jnp.exp(c - vs[0]), 0.0), axis=0, keepdims=True)
                for m, c in zip(sel, cand))
        ones = [jnp.where(m, 1.0, 0.0) for m in sel]
        low = sum(ones[1:SUBLANES])
        cnt = ones[0] + jnp.concatenate([low, jnp.zeros_like(low)], axis=0)
        tail = jnp.sum(ones[SUBLANES], axis=0, keepdims=True)
        cnt = cnt + jnp.where(lax.broadcasted_iota(jnp.int32, cnt.shape, 0) == 0, tail, 0.0)
        th_row = jnp.full(cnt.shape, jnp.inf, F32)
        for j in range(k):
            th_row = jnp.where(cnt == float(j + 1), v2[j], th_row)
        theta = jnp.full(s1.shape, jnp.inf, F32)
        for i in range(k):
            theta = jnp.where(s1 == v1[i], th_row[i:i + 1], theta)
        s2_ref[h, :, ls] = s2
        e2_ref[h, :, ls] = jnp.exp(s2 - v2[0]) * (0.5 / z)
        theta_ref[h, :, ls] = theta
        e1_ref[h, :, ls] = jnp.exp(s1 - v1[0])

    per = 2 if groups % 2 == 0 else 1

    def body(it, carry):
        for u in range(per):
            unit(it // (groups // per), (it % (groups // per)) * per + u)
        return carry

    lax.fori_loop(0, PEER_HEADS * groups // per, body, 0)


def _mix(x, mla, lru, wo, gffn, wqt, k1, k2, tm):
    t, d = x.shape
    row = lambda w: pl.BlockSpec((tm, w), lambda i: (i, 0))
    sc = pl.BlockSpec((PEER_HEADS, PEER_NKEYS, tm), lambda i: (0, 0, i))
    sc_shape = lambda dt: jax.ShapeDtypeStruct((PEER_HEADS, PEER_NKEYS, t), dt)
    return pl.pallas_call(
        _mix_kernel,
        grid=(t // tm,),
        in_specs=[row(d), row(MLA_WIDTH), row(LRU_WIDTH), _full(wo.shape), _full(gffn.shape),
                  _full(wqt.shape), _full(k1.shape), _full(k2.shape)],
        out_specs=(row(d), pl.BlockSpec((d, tm), lambda i: (0, i)), sc, sc, sc, sc),
        out_shape=(jax.ShapeDtypeStruct((t, d), F32), jax.ShapeDtypeStruct((d, t), BF16),
                   sc_shape(F32), sc_shape(F32), sc_shape(F32), sc_shape(F32)),
        scratch_shapes=[pltpu.VMEM((PEER_HEADS, PEER_NKEYS, tm), F32)] * 2,
        compiler_params=_params("parallel"),
        name="mix",
    )(x, mla, lru, wo, gffn, wqt, k1, k2)


def _peer_kernel(xn2_ref, x1_ref, u_ref, vt_ref, s2_ref, e2_ref, theta_ref, e1_ref, gfin_ref,
                 y_ref, ht0_sc, ht1_sc, wt0_sc, wt1_sc, yt_sc, *, eb, tb, ne, final_norm):
    g = pl.program_id(0)
    out_tile = jnp.maximum(g - 2, 0) % ne

    @pl.when(g == 0)
    def _():
        for ref in (ht0_sc, ht1_sc, wt0_sc, wt1_sc):
            ref[...] = jnp.zeros(ref.shape, ref.dtype)

    @pl.when(out_tile == 0)
    def _():
        yt_sc[...] = jnp.zeros(yt_sc.shape, F32)

    na = eb // PEER_NKEYS
    mh_rows = eb // 2
    nt_cols = 2 * LANES
    bq_rows = PEER_NKEYS // 4

    def stages(ht_cur, wt_cur, ht_prev, wt_prev):
        def pre_piece(mh, nt):
            ms = slice(mh * mh_rows, (mh + 1) * mh_rows)
            ns = slice(nt * nt_cols, (nt + 1) * nt_cols)
            ht_cur[ms, ns] = jnp.dot(u_ref[ms, :], xn2_ref[:, ns], preferred_element_type=F32)

        def out_piece(mh, nt):
            out_rows = yt_sc.shape[0] // (eb // mh_rows)
            ms = slice(mh * out_rows, (mh + 1) * out_rows)
            ns = slice(nt * nt_cols, (nt + 1) * nt_cols)
            yt_sc[ms, ns] += jnp.dot(vt_ref[ms, :], wt_cur[:, ns], preferred_element_type=F32)

        def mix_block(tl, bq):
            ls = slice(tl * LANES, (tl + 1) * LANES)
            bs = slice(bq * bq_rows, (bq + 1) * bq_rows)
            acc = [jnp.zeros((bq_rows, LANES), F32)] * na
            for h in range(PEER_HEADS):
                s2 = s2_ref[h, bs, ls]
                e2 = e2_ref[h, bs, ls]
                for al in range(na):
                    hit = s2 >= theta_ref[h, al:al + 1, ls]
                    acc[al] = acc[al] + jnp.where(hit, e1_ref[h, al:al + 1, ls] * e2, 0.0)
            for al in range(na):
                rs = slice(al * PEER_NKEYS + bq * bq_rows, al * PEER_NKEYS + (bq + 1) * bq_rows)
                wt_prev[rs, ls] = (acc[al] * _gelu_twice(ht_prev[rs, ls])).astype(BF16)

        pieces = [(f, mh, nt) for nt in range(tb // nt_cols) for mh in range(eb // mh_rows)
                  for f in (pre_piece, out_piece)]
        blocks = [(tl, bq) for tl in range(tb // LANES) for bq in range(PEER_NKEYS // bq_rows)]
        per = -(-len(blocks) // len(pieces))
        for i, (f, mh, nt) in enumerate(pieces):
            f(mh, nt)
            for tl, bq in blocks[i * per:(i + 1) * per]:
                mix_block(tl, bq)

    @pl.when(g % 2 == 0)
    def _():
        stages(ht0_sc, wt0_sc, ht1_sc, wt1_sc)

    @pl.when(g % 2 == 1)
    def _():
        stages(ht1_sc, wt1_sc, ht0_sc, wt0_sc)

    @pl.when((g >= 2) & (out_tile == ne - 1))
    def _():
        y = x1_ref[...] + yt_sc[...].T
        if final_norm:
            y = _rms(y, gfin_ref[...])
        y_ref[...] = y


def _peer(xn2, x1, u, vt, s2, e2, theta, e1, gfin, tb, eb, final_norm):
    t, d = x1.shape
    ne = u.shape[0] // eb
    na = eb // PEER_NKEYS
    steps = (t // tb) * ne
    assert na == SUBLANES and u.shape[0] % eb == 0 and tb % (2 * LANES) == 0 and ne > 1

    def at(lag):
        def pos(g):
            s = jnp.clip(g - lag, 0, steps - 1)
            return s // ne, s % ne
        return pos

    pre, mid, out = at(0), at(1), at(2)
    sc = pl.BlockSpec((PEER_HEADS, PEER_NKEYS, tb), lambda g: (0, 0, mid(g)[0]))
    rows = pl.BlockSpec((PEER_HEADS, na, tb), lambda g: (0, mid(g)[1], mid(g)[0]))
    return pl.pallas_call(
        functools.partial(_peer_kernel, eb=eb, tb=tb, ne=ne, final_norm=final_norm),
        grid=(steps + 2,),
        in_specs=[pl.BlockSpec((d, tb), lambda g: (0, pre(g)[0])),
                  pl.BlockSpec((tb, d), lambda g: (out(g)[0], 0)),
                  pl.BlockSpec((eb, d), lambda g: (pre(g)[1], 0)),
                  pl.BlockSpec((d, eb), lambda g: (0, out(g)[1])),
                  sc, sc, rows, rows,
                  _full(gfin.shape)],
        out_specs=pl.BlockSpec((tb, d), lambda g: (out(g)[0], 0)),
        out_shape=jax.ShapeDtypeStruct((t, d), F32),
        scratch_shapes=[pltpu.VMEM((eb, tb), F32), pltpu.VMEM((eb, tb), F32),
                        pltpu.VMEM((eb, tb), BF16), pltpu.VMEM((eb, tb), BF16), pltpu.VMEM((d, tb), F32)],
        compiler_params=_params("arbitrary"),
        name="peer",
    )(xn2, x1, u, vt, s2, e2, theta, e1, gfin)


def _rope_tables(pos):
    half = MLA_ROPE // 2
    inv = ROPE_THETA ** (-jnp.arange(half, dtype=F32) / half)
    ang = pos.astype(F32)[:, None] * inv[None, :]
    cos, sin = jnp.cos(ang), jnp.sin(ang)
    c2 = jnp.concatenate([cos, cos], axis=-1)
    s2 = jnp.concatenate([-sin, sin], axis=-1)
    rows = jnp.concatenate([_place_cols(c2), _place_cols(s2)], axis=-1)
    return rows, jnp.concatenate([c2, s2], axis=-1).T


def _swap_halves(w):
    half = w.shape[-1] // 2
    return jnp.concatenate([w[..., half:], w[..., :half]], axis=-1)


def _place_cols(w):
    pad = [(0, 0)] * (w.ndim - 1) + [(ROPE_OFF, LANES - ROPE_OFF - w.shape[-1])]
    return jnp.pad(w, pad)


def _layer_weights(w_in, w_uq, w_uk, w_uv, lru_wa, lru_wi, w_out, peer_wq, peer_keys1, peer_keys2,
                   peer_u, peer_v):
    o1 = MLA_QRANK
    o2 = o1 + MLA_KVRANK
    o3 = o2 + MLA_ROPE
    kr = w_in[:, o2:o3]
    win = jnp.concatenate([w_in[:, :o2], _place_cols(kr), _place_cols(_swap_halves(kr)),
                           w_in[:, o3:]], axis=1).astype(BF16)
    wint = w_in[:, :o2].T.astype(BF16)
    wq3 = w_uq.reshape(MLA_QRANK, MLA_HEADS, MLA_NOPE + MLA_ROPE)
    nope = wq3[:, :, :MLA_NOPE].reshape(MLA_QRANK, MLA_HEADS * MLA_NOPE)
    rp = wq3[:, :, MLA_NOPE:]
    flat = lambda w: w.reshape(MLA_QRANK, -1)
    wuq = jnp.concatenate([nope, flat(_place_cols(rp)), flat(_place_cols(_swap_halves(rp)))], axis=1).astype(BF16)
    wuqt = jnp.concatenate([nope, flat(rp), flat(_swap_halves(rp))], axis=1).T.astype(BF16)
    eye_h = jnp.eye(MLA_HEADS, dtype=F32)
    wuk = jnp.einsum("rhd,hg->hdgr", w_uk, eye_h).reshape(MLA_HEADS * MLA_NOPE, MLA_HEADS * MLA_KVRANK)
    wuv = jnp.einsum("rhd,hg->hrgd", w_uv, eye_h).reshape(MLA_HEADS, MLA_KVRANK, MLA_WIDTH)
    wuvt = jnp.transpose(w_uv, (1, 2, 0))
    eye_b = jnp.eye(LRU_BLOCKS, dtype=F32)
    bd = lambda w: jnp.einsum("nde,nm->ndme", w, eye_b).reshape(LRU_WIDTH, LRU_WIDTH)
    return dict(win=win, wint=wint, wuq=wuq, wuqt=wuqt, wuk=wuk.astype(BF16), wukt=wuk.T.astype(BF16),
                wuv=wuv.astype(BF16), wuvt=wuvt.astype(BF16),
                wa=bd(lru_wa).astype(BF16), wi=bd(lru_wi).astype(BF16), wo=w_out.astype(BF16),
                wqt=peer_wq.T.astype(BF16), k1=peer_keys1.astype(BF16), k2=peer_keys2.astype(BF16),
                u=peer_u.astype(BF16), vt=peer_v.T.astype(BF16))


def _col(v):
    return v.reshape(-1, 1).astype(F32)


def _row(v):
    return v.reshape(1, -1).astype(F32)


def _token_tile(n, cap):
    t = min(n, cap)
    assert n % t == 0, (n, t)
    return t


def kernel(x_prompt, x_sample, cache_mla_ckv, cache_mla_krope, state_lru_h, state_lru_conv, norm_mix, w_in, norm_q, w_uq, norm_kv, w_uk, w_uv, conv_w, conv_b, lru_wa, lru_ba, lru_wi, lru_bi, lru_lambda, norm_mla_out, norm_lru_out, w_out, norm_ffn, peer_wq, peer_keys1, peer_keys2, peer_u, peer_v, norm_final):
    bp, sp, d = x_prompt.shape
    bs, sd, _ = x_sample.shape
    depth = w_in.shape[0]
    past = cache_mla_ckv.shape[2]
    ts_tok = bs * sd
    tab_p, tabt_p = _rope_tables(jnp.arange(sp))
    tab_s = jnp.tile(_rope_tables(past + jnp.arange(sd))[0], (bs, 1))
    gfin = _row(norm_final)

    xp = x_prompt
    xs = x_sample.reshape(1, ts_tok, d)
    outs = [[] for _ in range(8)]
    for l in range(depth):
        w = _layer_weights(w_in[l], w_uq[l], w_uk[l], w_uv[l], lru_wa[l], lru_wi[l], w_out[l], peer_wq[l],
                           peer_keys1[l], peer_keys2[l], peer_u[l], peer_v[l])
        last = l == depth - 1
        lru_args = (conv_w[l].astype(F32), _row(conv_b[l]), w["wa"], _row(lru_ba[l]), w["wi"], _row(lru_bi[l]),
                    _row(lru_lambda[l]), _row(norm_lru_out[l]))
        proj_common = (_row(norm_mix[l]), w["win"], _row(norm_kv[l]))
        proj_rows = (_row(norm_q[l]), w["wuq"], w["wuk"])
        proj_cols = (tabt_p, w["wint"], _col(norm_q[l]), _col(norm_kv[l]), w["wuqt"], w["wukt"])
        gmla = _row(norm_mla_out[l])

        def tail(x2d, mla, lru):
            t = x2d.shape[0]
            tm = _token_tile(t, TOKEN_TILE)
            x1, xn2, s2, e2, theta, e1 = _mix(x2d, mla, lru, w["wo"], _row(norm_ffn[l]), w["wqt"],
                                               w["k1"], w["k2"], tm)
            return _peer(xn2, x1, w["u"], w["vt"], s2, e2, theta, e1, gfin, tm, PEER_TILE, last)

        kcat, ckv, kr, xr, xg, qt, vt = _proj(xp, tab_p, proj_common, proj_cols, _token_tile(sp, TOKEN_TILE), True)
        mla = _attn_prompt(qt, kcat, vt, w["wuvt"], gmla, _token_tile(sp, TOKEN_TILE))
        lru, hl, nb = _lru(xr, xg, jnp.zeros((bp, CONV_WIDTH - 1, LRU_WIDTH), F32),
                           jnp.zeros((bp, 1, LRU_WIDTH), F32), *lru_args, _token_tile(sp, TOKEN_TILE))
        xp = tail(xp.reshape(bp * sp, d), mla.reshape(bp * sp, MLA_WIDTH),
                  lru.reshape(bp * sp, LRU_WIDTH)).reshape(bp, sp, d)
        for lst, v in zip(outs[:4], (ckv, kr, hl[:, 0], nb)):
            lst.append(v)

        kcat, ckv, kr, xr, xg, q = _proj(xs, tab_s, proj_common, proj_rows, _token_tile(ts_tok, TOKEN_TILE), False)
        mla = _attn_sample(q, cache_mla_ckv[l], cache_mla_krope[l], kcat, w["wuv"], gmla, bs, sd)
        lru, hl, nb = _lru(xr.reshape(bs, sd, LRU_WIDTH), xg.reshape(bs, sd, LRU_WIDTH),
                           state_lru_conv[l].astype(F32), state_lru_h[l].reshape(bs, 1, LRU_WIDTH).astype(F32),
                           *lru_args, sd)
        xs = tail(xs[0], mla[0], lru.reshape(ts_tok, LRU_WIDTH)).reshape(1, ts_tok, d)
        for lst, v in zip(outs[4:], (ckv.reshape(bs, sd, MLA_KVRANK), kr.reshape(bs, sd, MLA_ROPE), hl[:, 0], nb)):
            lst.append(v)

    return (xp, xs.reshape(bs, sd, d)) + tuple(jnp.stack(o) for o in outs)
```

```python
import functools

import jax
import jax.numpy as jnp
from jax import lax
from jax.experimental import pallas as pl
from jax.experimental.pallas import tpu as pltpu

F32 = jnp.float32
BF16 = jnp.bfloat16

LANES = 128
SUBLANES = 8
PACKED_ROWS = 2 * SUBLANES
CHUNK = 64
EPS = 1e-6
MLA_HEADS = 8
MLA_NOPE = 64
MLA_ROPE = 32
MLA_VDIM = 64
MLA_QRANK = 256
MLA_KVRANK = 128
ROPE_THETA = 10000.0
MLA_WIDTH = MLA_HEADS * MLA_VDIM
MLA_SCALE = (MLA_NOPE + MLA_ROPE) ** -0.5
QK_PAD = 256
LRU_WIDTH = 512
LRU_BLOCKS = 8
CONV_WIDTH = 4
LRU_C = 8.0
PEER_HEADS = 8
PEER_NKEYS = 128
PEER_KEY_DIM = 256
PEER_HALF = PEER_KEY_DIM // 2
PEER_TOPK = 16
TOKEN_TILE = 512
SCORE_LEAD = 2
PEER_TILE = 8 * PEER_NKEYS
VMEM_LIMIT = 56 * 1024 * 1024

F_ONES = MLA_KVRANK
F_ROPE = F_ONES + PACKED_ROWS
V_ROWS = F_ROPE
ROPE_OFF = F_ROPE - LANES
ZC_CQ = 0
ZC_CKV = ZC_CQ + MLA_QRANK
ZC_KR = ZC_CKV + MLA_KVRANK
ZC_KRS = ZC_KR + LANES
ZC_XR = ZC_KRS + LANES
ZC_XG = ZC_XR + LRU_WIDTH
ZC_END = ZC_XG + LRU_WIDTH
QC_NOPE = 0
QC_ROPE = QC_NOPE + MLA_HEADS * MLA_NOPE
QC_ROPES = QC_ROPE + MLA_HEADS * LANES
QR_NOPE = 0
QR_ROPE = QR_NOPE + MLA_HEADS * MLA_NOPE
QR_ROPES = QR_ROPE + MLA_HEADS * MLA_ROPE

_NT = (((1,), (1,)), ((), ()))


def _rms(x, g):
    return x * lax.rsqrt(jnp.mean(x * x, axis=-1, keepdims=True) + EPS) * g


def _rms_cols(xt, g):
    return xt * lax.rsqrt(jnp.mean(xt * xt, axis=0, keepdims=True) + EPS) * g


def _expm1(y):
    u = jnp.exp(y)
    um1 = u - 1.0
    return jnp.where(um1 == 0.0, y, jnp.where(um1 == -1.0, -1.0, um1 * y / jnp.log(u)))


def _gelu(x):
    return 0.5 * x * (1.0 + lax.erf(x * (2.0 ** -0.5)))


def _gelu_twice(x):
    return x * (1.0 + lax.erf(x * (2.0 ** -0.5)))


def _params(*sem):
    return pltpu.CompilerParams(dimension_semantics=sem, vmem_limit_bytes=VMEM_LIMIT)


def _full(shape):
    n = len(shape)
    return pl.BlockSpec(shape, lambda *_: (0,) * n)


def _proj_common(x_ref, tab_ref, gmix_ref, win_ref, gkv_ref, kcat_ref, ckv_ref, kr_ref, xr_ref, xg_ref):
    xn = _rms(x_ref[0], gmix_ref[...]).astype(BF16)
    z = jnp.dot(xn, win_ref[...], preferred_element_type=F32)
    ctab = tab_ref[:, :LANES]
    stab = tab_ref[:, LANES:]
    ckv = _rms(z[:, ZC_CKV:ZC_KR], gkv_ref[...])
    kr = z[:, ZC_KR:ZC_KRS] * ctab + z[:, ZC_KRS:ZC_XR] * stab
    ckv_ref[0] = ckv
    kr_ref[0] = kr[:, ROPE_OFF:ROPE_OFF + MLA_ROPE]
    kcat_ref[0, :, :LANES] = ckv.astype(BF16)
    lane = lax.broadcasted_iota(jnp.int32, kr.shape, 1)
    kcat_ref[0, :, LANES:] = jnp.where(lane < ROPE_OFF, 1.0, kr).astype(BF16)
    xr_ref[0] = z[:, ZC_XR:ZC_XG]
    xg_ref[0] = z[:, ZC_XG:ZC_END]
    return xn, z, ctab, stab


def _proj_rows_kernel(x_ref, tab_ref, gmix_ref, win_ref, gkv_ref, gq_ref, wuq_ref, wuk_ref,
                      kcat_ref, ckv_ref, kr_ref, xr_ref, xg_ref, q_ref):
    _, z, ctab, stab = _proj_common(x_ref, tab_ref, gmix_ref, win_ref, gkv_ref,
                                    kcat_ref, ckv_ref, kr_ref, xr_ref, xg_ref)
    cqn = _rms(z[:, ZC_CQ:ZC_CKV], gq_ref[...]).astype(BF16)
    q = jnp.dot(cqn, wuq_ref[...], preferred_element_type=F32)
    qlat = jnp.dot(q[:, QC_NOPE:QC_ROPE].astype(BF16), wuk_ref[...], preferred_element_type=F32)
    for h in range(MLA_HEADS):
        q_ref[0, h, :, :LANES] = (qlat[:, h * LANES:(h + 1) * LANES] * MLA_SCALE).astype(BF16)
        rp = (q[:, QC_ROPE + h * LANES:QC_ROPE + (h + 1) * LANES] * ctab
              + q[:, QC_ROPES + h * LANES:QC_ROPES + (h + 1) * LANES] * stab)
        q_ref[0, h, :, LANES:] = (rp * MLA_SCALE).astype(BF16)


def _proj_cols_kernel(x_ref, tab_ref, gmix_ref, win_ref, gkv_ref, tabt_ref, wint_ref, gqc_ref, gkvc_ref,
                      wuqt_ref, wukt_ref, kcat_ref, ckv_ref, kr_ref, xr_ref, xg_ref, qt_ref, vt_ref):
    xn, _, _, _ = _proj_common(x_ref, tab_ref, gmix_ref, win_ref, gkv_ref,
                               kcat_ref, ckv_ref, kr_ref, xr_ref, xg_ref)
    tm = xn.shape[0]
    zt = lax.dot_general(wint_ref[...], xn, _NT, preferred_element_type=F32)
    ckvt = _rms_cols(zt[MLA_QRANK:], gkvc_ref[...])
    vt_ref[0, :F_ONES, :] = ckvt.astype(BF16)
    vt_ref[0, F_ONES:, :] = jnp.ones((V_ROWS - F_ONES, tm), BF16)
    cqnt = _rms_cols(zt[:MLA_QRANK], gqc_ref[...]).astype(BF16)
    qt = jnp.dot(wuqt_ref[...], cqnt, preferred_element_type=F32)
    qlatt = jnp.dot(wukt_ref[...], qt[QR_NOPE:QR_ROPE].astype(BF16), preferred_element_type=F32)
    ctabt = tabt_ref[:MLA_ROPE, :]
    stabt = tabt_ref[MLA_ROPE:, :]
    for h in range(MLA_HEADS):
        qt_ref[0, h, :F_ONES, :] = (qlatt[h * MLA_KVRANK:(h + 1) * MLA_KVRANK] * MLA_SCALE).astype(BF16)
        qt_ref[0, h, F_ONES:F_ROPE, :] = jnp.zeros((F_ROPE - F_ONES, tm), BF16)
        rp = (qt[QR_ROPE + h * MLA_ROPE:QR_ROPE + (h + 1) * MLA_ROPE] * ctabt
              + qt[QR_ROPES + h * MLA_ROPE:QR_ROPES + (h + 1) * MLA_ROPE] * stabt)
        qt_ref[0, h, F_ROPE:F_ROPE + MLA_ROPE, :] = (rp * MLA_SCALE).astype(BF16)
        qt_ref[0, h, F_ROPE + MLA_ROPE:, :] = jnp.zeros((QK_PAD - F_ROPE - MLA_ROPE, tm), BF16)


def _proj(x, tab, common, extra, tm, cols):
    b, s, d = x.shape
    row = lambda w: pl.BlockSpec((1, tm, w), lambda bi, i: (bi, i, 0))
    shape = lambda *dims: jax.ShapeDtypeStruct((b,) + dims, F32)
    out_specs = [row(QK_PAD), row(MLA_KVRANK), row(MLA_ROPE), row(LRU_WIDTH), row(LRU_WIDTH)]
    out_shape = [jax.ShapeDtypeStruct((b, s, QK_PAD), BF16), shape(s, MLA_KVRANK), shape(s, MLA_ROPE),
                 shape(s, LRU_WIDTH), shape(s, LRU_WIDTH)]
    in_specs = [row(d), pl.BlockSpec((tm, 2 * LANES), lambda bi, i: (i, 0))] + [_full(a.shape) for a in common]
    if cols:
        tabt, rest = extra[0], extra[1:]
        in_specs += [pl.BlockSpec((2 * MLA_ROPE, tm), lambda bi, i: (0, i))] + [_full(a.shape) for a in rest]
        out_specs += [pl.BlockSpec((1, MLA_HEADS, QK_PAD, tm), lambda bi, i: (bi, 0, 0, i)),
                      pl.BlockSpec((1, V_ROWS, tm), lambda bi, i: (bi, 0, i))]
        out_shape += [jax.ShapeDtypeStruct((b, MLA_HEADS, QK_PAD, s), BF16),
                      jax.ShapeDtypeStruct((b, V_ROWS, s), BF16)]
    else:
        in_specs += [_full(a.shape) for a in extra]
        out_specs += [pl.BlockSpec((1, MLA_HEADS, tm, QK_PAD), lambda bi, i: (bi, 0, i, 0))]
        out_shape += [jax.ShapeDtypeStruct((b, MLA_HEADS, s, QK_PAD), BF16)]
    return pl.pallas_call(
        _proj_cols_kernel if cols else _proj_rows_kernel,
        grid=(b, s // tm),
        in_specs=in_specs,
        out_specs=tuple(out_specs),
        out_shape=tuple(out_shape),
        compiler_params=_params("parallel", "parallel"),
        name="proj_cols" if cols else "proj_rows",
    )(x, tab, *common, *extra)


def _attn_prompt_kernel(qt_ref, k_ref, vt_ref, wuvt_ref, g_ref, o_ref,
                        m_sc, acc_sc, st_sc, cmax_sc, *, qb):
    i = pl.program_id(1)
    m_sc[...] = jnp.full(m_sc.shape, -jnp.inf, F32)
    acc_sc[...] = jnp.zeros(acc_sc.shape, F32)

    def step(j, masked):
        ks = pl.ds(pl.multiple_of(j * qb, qb), qb)

        def scores(h):
            st = jnp.dot(k_ref[0, ks, :], qt_ref[0, h], preferred_element_type=F32)
            if masked:
                kc = lax.broadcasted_iota(jnp.int32, st.shape, 0) // CHUNK
                qc = lax.broadcasted_iota(jnp.int32, st.shape, 1) // CHUNK
                st = jnp.where(kc <= qc, st, -jnp.inf)
            st_sc[h] = st
            cmax_sc[h:h + 1, :] = jnp.max(st, axis=0, keepdims=True)

        for h in range(SCORE_LEAD):
            scores(h)
        for h in range(MLA_HEADS):
            if h + SCORE_LEAD < MLA_HEADS:
                scores(h + SCORE_LEAD)
            m_prev = m_sc[h:h + 1, :]
            m_new = jnp.maximum(m_prev, cmax_sc[h:h + 1, :])
            p = jnp.exp(st_sc[h] - m_new).astype(BF16)
            m_sc[h:h + 1, :] = m_new
            acc_sc[h] = (jnp.exp(m_prev - m_new) * acc_sc[h]
                         + jnp.dot(vt_ref[0, :, ks], p, preferred_element_type=F32))

    def body(j, carry):
        step(j, False)
        return carry

    lax.fori_loop(0, i, body, 0)
    step(i, True)
    parts = []
    for h in range(MLA_HEADS):
        acc = acc_sc[h]
        att = (acc[:F_ONES] / acc[F_ONES:F_ONES + 1]).astype(BF16)
        parts.append(jnp.dot(wuvt_ref[h], att, preferred_element_type=F32))
    mla = jnp.concatenate(parts, axis=0).T
    o_ref[0] = _rms(mla, g_ref[...]).astype(BF16)


def _attn_prompt(qt, kcat, vt, wuvt, g, qb):
    b, _, _, s = qt.shape
    return pl.pallas_call(
        functools.partial(_attn_prompt_kernel, qb=qb),
        grid=(b, s // qb),
        in_specs=[pl.BlockSpec((1, MLA_HEADS, QK_PAD, qb), lambda bi, i: (bi, 0, 0, i)),
                  pl.BlockSpec((1, s, QK_PAD), lambda bi, i: (bi, 0, 0)),
                  pl.BlockSpec((1, V_ROWS, s), lambda bi, i: (bi, 0, 0)),
                  _full(wuvt.shape), _full(g.shape)],
        out_specs=pl.BlockSpec((1, qb, MLA_WIDTH), lambda bi, i: (bi, i, 0)),
        out_shape=jax.ShapeDtypeStruct((b, s, MLA_WIDTH), BF16),
        scratch_shapes=[pltpu.VMEM((MLA_HEADS, qb), F32), pltpu.VMEM((MLA_HEADS, V_ROWS, qb), F32),
                        pltpu.VMEM((MLA_HEADS, qb, qb), F32), pltpu.VMEM((MLA_HEADS, qb), F32)],
        compiler_params=_params("parallel", "arbitrary"),
        name="attn_prompt",
    )(qt, kcat, vt, wuvt, g)


def _attn_sample_kernel(q_ref, pckv_ref, pkr_ref, k_ref, wuv_ref, g_ref, o_ref, *, sd, past):
    rows = MLA_HEADS * sd
    q = q_ref[0].reshape(rows, QK_PAD)
    knew = k_ref[0]
    s_past = (lax.dot_general(q[:, :LANES], pckv_ref[0].astype(BF16), _NT, preferred_element_type=F32)
              + lax.dot_general(q[:, F_ROPE:F_ROPE + MLA_ROPE], pkr_ref[0].astype(BF16), _NT,
                                preferred_element_type=F32))
    s_new = lax.dot_general(q, knew, _NT, preferred_element_type=F32)

    def mask(s, k0):
        qc = (past + lax.broadcasted_iota(jnp.int32, s.shape, 0) % sd) // CHUNK
        kc = (k0 + lax.broadcasted_iota(jnp.int32, s.shape, 1)) // CHUNK
        return jnp.where(kc <= qc, s, -jnp.inf)

    s_past = mask(s_past, 0)
    s_new = mask(s_new, past)
    m = jnp.maximum(jnp.max(s_past, axis=-1, keepdims=True), jnp.max(s_new, axis=-1, keepdims=True))
    p_past = jnp.exp(s_past - m).astype(BF16)
    p_new = jnp.exp(s_new - m).astype(BF16)
    acc = jnp.dot(p_new, knew, preferred_element_type=F32)
    num = acc[:, :LANES] + jnp.dot(p_past, pckv_ref[0].astype(BF16), preferred_element_type=F32)
    den = acc[:, F_ONES:F_ONES + 1] + jnp.sum(p_past.astype(F32), axis=-1, keepdims=True)
    att = (num / den).astype(BF16)
    mla = jnp.zeros((sd, MLA_WIDTH), F32)
    for h in range(MLA_HEADS):
        mla = mla + jnp.dot(att[h * sd:(h + 1) * sd], wuv_ref[h], preferred_element_type=F32)
    o_ref[0] = _rms(mla, g_ref[...]).astype(BF16)


def _attn_sample(q, pckv, pkr, kcat, wuv, g, nb, sd):
    past = pckv.shape[1]
    return pl.pallas_call(
        functools.partial(_attn_sample_kernel, sd=sd, past=past),
        grid=(nb,),
        in_specs=[pl.BlockSpec((1, MLA_HEADS, sd, QK_PAD), lambda b: (0, 0, b, 0)),
                  pl.BlockSpec((1, past, MLA_KVRANK), lambda b: (b, 0, 0)),
                  pl.BlockSpec((1, past, MLA_ROPE), lambda b: (b, 0, 0)),
                  pl.BlockSpec((1, sd, QK_PAD), lambda b: (0, b, 0)),
                  _full(wuv.shape), _full(g.shape)],
        out_specs=pl.BlockSpec((1, sd, MLA_WIDTH), lambda b: (0, b, 0)),
        out_shape=jax.ShapeDtypeStruct((1, nb * sd, MLA_WIDTH), BF16),
        compiler_params=_params("parallel"),
        name="attn_sample",
    )(q, pckv, pkr, kcat, wuv, g)


def _lru_kernel(xr_ref, xg_ref, buf0_ref, h0_ref, cw_ref, cb_ref, wa_ref, ba_ref, wi_ref, bi_ref,
                lam_ref, g_ref, o_ref, hlast_ref, newbuf_ref, xp_sc, h_sc, a_sc, b_sc, hh_sc, *, ts):
    t = pl.program_id(1)
    nt = pl.num_programs(1)
    pad = SUBLANES

    @pl.when(t == 0)
    def _():
        xp_sc[pad - (CONV_WIDTH - 1):pad, :] = buf0_ref[0]
        h_sc[...] = h0_ref[0]

    xr = xr_ref[0]
    xp_sc[pad:pad + ts, :] = xr
    xc = cb_ref[...] + cw_ref[CONV_WIDTH - 1:CONV_WIDTH, :] * xr
    for k in range(CONV_WIDTH - 1):
        xc = xc + cw_ref[k:k + 1, :] * xp_sc[pad - (CONV_WIDTH - 1) + k:pad - (CONV_WIDTH - 1) + k + ts, :]
    tail = xp_sc[ts + pad - (CONV_WIDTH - 1):ts + pad, :]
    xp_sc[pad - (CONV_WIDTH - 1):pad, :] = tail

    xcb = xc.astype(BF16)
    r = jax.nn.sigmoid(jnp.dot(xcb, wa_ref[...], preferred_element_type=F32) + ba_ref[...])
    ig = jax.nn.sigmoid(jnp.dot(xcb, wi_ref[...], preferred_element_type=F32) + bi_ref[...])
    log_a = -LRU_C * r * jax.nn.softplus(-lam_ref[...])
    a = jnp.exp(log_a)
    bt = jnp.sqrt(-_expm1(2.0 * log_a)) * (ig * xc)

    ng = ts // SUBLANES
    a3 = a.reshape(ng, SUBLANES, LRU_WIDTH)
    b3 = bt.reshape(ng, SUBLANES, LRU_WIDTH)
    row = lax.broadcasted_iota(jnp.int32, a3.shape, 1)
    for d in (1, 2, 4):
        valid = row >= d
        a_sh = pltpu.roll(a3, d, axis=1)
        b_sh = pltpu.roll(b3, d, axis=1)
        b3 = jnp.where(valid, a3 * b_sh + b3, b3)
        a3 = jnp.where(valid, a3 * a_sh, a3)
    a_sc[...] = a3.reshape(ts, LRU_WIDTH)
    b_sc[...] = b3.reshape(ts, LRU_WIDTH)

    def group(gi, hprev):
        rs = pl.ds(pl.multiple_of(gi * SUBLANES, SUBLANES), SUBLANES)
        hg = a_sc[rs, :] * hprev + b_sc[rs, :]
        hh_sc[rs, :] = hg
        return hg[SUBLANES - 1:SUBLANES, :]

    hfin = lax.fori_loop(0, ng, group, h_sc[...])
    h_sc[...] = hfin

    lru_out = hh_sc[...] * _gelu(xg_ref[0])
    o_ref[0] = _rms(lru_out, g_ref[...]).astype(BF16)

    @pl.when(t == nt - 1)
    def _():
        hlast_ref[0] = hfin
        newbuf_ref[0] = tail


def _lru(xr, xg, buf0, h0, cw, cb, wa, ba, wi, bi, lam, g, ts):
    b, s, w = xr.shape
    row = pl.BlockSpec((1, ts, w), lambda bi_, t: (bi_, t, 0))
    return pl.pallas_call(
        functools.partial(_lru_kernel, ts=ts),
        grid=(b, s // ts),
        in_specs=[row, row,
                  pl.BlockSpec((1, CONV_WIDTH - 1, w), lambda bi_, t: (bi_, 0, 0)),
                  pl.BlockSpec((1, 1, w), lambda bi_, t: (bi_, 0, 0)),
                  _full(cw.shape), _full(cb.shape), _full(wa.shape), _full(ba.shape),
                  _full(wi.shape), _full(bi.shape), _full(lam.shape), _full(g.shape)],
        out_specs=(row,
                   pl.BlockSpec((1, 1, w), lambda bi_, t: (bi_, 0, 0)),
                   pl.BlockSpec((1, CONV_WIDTH - 1, w), lambda bi_, t: (bi_, 0, 0))),
        out_shape=(jax.ShapeDtypeStruct((b, s, w), BF16),
                   jax.ShapeDtypeStruct((b, 1, w), F32),
                   jax.ShapeDtypeStruct((b, CONV_WIDTH - 1, w), F32)),
        scratch_shapes=[pltpu.VMEM((ts + SUBLANES, w), F32), pltpu.VMEM((1, w), F32),
                        pltpu.VMEM((ts, w), F32), pltpu.VMEM((ts, w), F32), pltpu.VMEM((ts, w), F32)],
        compiler_params=_params("parallel", "arbitrary"),
        name="lru",
    )(xr, xg, buf0, h0, cw, cb, wa, ba, wi, bi, lam, g)


def _oddeven_pairs(n):
    pairs = []

    def merge(lo, m, r):
        step = r * 2
        if step < m:
            merge(lo, m, step)
            merge(lo + r, m, step)
            pairs.extend((i, i + r) for i in range(lo + r, lo + m - r, step))
        else:
            pairs.append((lo, lo + r))

    def sort(lo, m):
        if m > 1:
            sort(lo, m // 2)
            sort(lo + m // 2, m // 2)
            merge(lo, m, 1)

    sort(0, n)
    return pairs


_SORT16 = _oddeven_pairs(16)


def _top_values(s, n):
    t = s.shape[0] // SUBLANES
    assert t <= 16 and s.shape[0] % SUBLANES == 0
    v = [s[i * SUBLANES:(i + 1) * SUBLANES] for i in range(t)]
    for i, j in _SORT16:
        if j < t:
            v[i], v[j] = jnp.maximum(v[i], v[j]), jnp.minimum(v[i], v[j])
    top = []
    for r in range(n):
        m = jnp.max(v[0], axis=0, keepdims=True)
        top.append(m)
        hit = v[0] == m
        for i in range(min(n - r - 1, t)):
            v[i] = jnp.where(hit, v[i + 1] if i + 1 < t else -jnp.inf, v[i])
    return top


def _mix_kernel(x_ref, mla_ref, lru_ref, wo_ref, gffn_ref, wqt_ref, k1_ref, k2_ref,
                x1_ref, xn2_ref, s2_ref, e2_ref, theta_ref, e1_ref, s1_sc, s2_sc):
    x1 = (x_ref[...]
          + jnp.dot(mla_ref[...], wo_ref[:MLA_WIDTH, :], preferred_element_type=F32)
          + jnp.dot(lru_ref[...], wo_ref[MLA_WIDTH:, :], preferred_element_type=F32))
    x1_ref[...] = x1
    xn2t = _rms(x1, gffn_ref[...]).T.astype(BF16)
    xn2_ref[...] = xn2t
    qt = jnp.dot(wqt_ref[...], xn2t, preferred_element_type=F32).astype(BF16)
    for h in range(PEER_HEADS):
        base = h * PEER_KEY_DIM
        s1_sc[h] = jnp.dot(k1_ref[h], qt[base:base + PEER_HALF], preferred_element_type=F32)
        s2_sc[h] = jnp.dot(k2_ref[h], qt[base + PEER_HALF:base + PEER_KEY_DIM], preferred_element_type=F32)
    k = PEER_TOPK
    groups = x1.shape[0] // LANES

    def unit(h, lane_group):
        ls = pl.ds(pl.multiple_of(lane_group * LANES, LANES), LANES)
        s1 = s1_sc[h, :, ls]
        s2 = s2_sc[h, :, ls]
        v1 = _top_values(s1, k)
        v2 = _top_values(s2, k)
        v1a = jnp.concatenate(v1, axis=0)
        cand = [v1a + v2[0]]
        cand += [v1a[:SUBLANES] + v2[j] for j in range(1, SUBLANES)]
        cand += [v1[0] + jnp.concatenate(v2[SUBLANES:], axis=0)]
        vs = _top_values(jnp.concatenate(cand, axis=0), k)
        tau = vs[k - 1]
        sel = [c >= tau for c in cand]
        z = sum(jnp.sum(jnp.where(m, jnp.exp(c - vs[0]), 0.0), axis=0, keepdims=True)
                for m, c in zip(sel, cand))
        ones = [jnp.where(m, 1.0, 0.0) for m in sel]
        low = sum(ones[1:SUBLANES])
        cnt = ones[0] + jnp.concatenate([low, jnp.zeros_like(low)], axis=0)
        tail = jnp.sum(ones[SUBLANES], axis=0, keepdims=True)
        cnt = cnt + jnp.where(lax.broadcasted_iota(jnp.int32, cnt.shape, 0) == 0, tail, 0.0)
        th_row = jnp.full(cnt.shape, jnp.inf, F32)
        for j in range(k):
            th_row = jnp.where(cnt == float(j + 1), v2[j], th_row)
        theta = jnp.full(s1.shape, jnp.inf, F32)
        for i in range(k):
            theta = jnp.where(s1 == v1[i], th_row[i:i + 1], theta)
        s2_ref[h, :, ls] = s2
        e2_ref[h, :, ls] = jnp.exp(s2 - v2[0]) * (0.5 / z)
        theta_ref[h, :, ls] = theta
        e1_ref[h, :, ls] = jnp.exp(s1 - v1[0])

    per = 2 if groups % 2 == 0 else 1

    def body(it, carry):
        for u in range(per):
            unit(it // (groups // per), (it % (groups // per)) * per + u)
        return carry

    lax.fori_loop(0, PEER_HEADS * groups // per, body, 0)


def _mix(x, mla, lru, wo, gffn, wqt, k1, k2, tm):
    t, d = x.shape
    row = lambda w: pl.BlockSpec((tm, w), lambda i: (i, 0))
    sc = pl.BlockSpec((PEER_HEADS, PEER_NKEYS, tm), lambda i: (0, 0, i))
    sc_shape = lambda dt: jax.ShapeDtypeStruct((PEER_HEADS, PEER_NKEYS, t), dt)
    return pl.pallas_call(
        _mix_kernel,
        grid=(t // tm,),
        in_specs=[row(d), row(MLA_WIDTH), row(LRU_WIDTH), _full(wo.shape), _full(gffn.shape),
                  _full(wqt.shape), _full(k1.shape), _full(k2.shape)],
        out_specs=(row(d), pl.BlockSpec((d, tm), lambda i: (0, i)), sc, sc, sc, sc),
        out_shape=(jax.ShapeDtypeStruct((t, d), F32), jax.ShapeDtypeStruct((d, t), BF16),
                   sc_shape(F32), sc_shape(F32), sc_shape(F32), sc_shape(F32)),
        scratch_shapes=[pltpu.VMEM((PEER_HEADS, PEER_NKEYS, tm), F32)] * 2,
        compiler_params=_params("parallel"),
        name="mix",
    )(x, mla, lru, wo, gffn, wqt, k1, k2)


def _peer_kernel(xn2_ref, x1_ref, u_ref, vt_ref, s2_ref, e2_ref, theta_ref, e1_ref, gfin_ref,
                 y_ref, ht0_sc, ht1_sc, wt0_sc, wt1_sc, yt_sc, *, eb, tb, ne, final_norm):
    g = pl.program_id(0)
    out_tile = jnp.maximum(g - 2, 0) % ne

    @pl.when(g == 0)
    def _():
        for ref in (ht0_sc, ht1_sc, wt0_sc, wt1_sc):
            ref[...] = jnp.zeros(ref.shape, ref.dtype)

    @pl.when(out_tile == 0)
    def _():
        yt_sc[...] = jnp.zeros(yt_sc.shape, F32)

    na = eb // PEER_NKEYS
    mh_rows = eb // 2
    nt_cols = 2 * LANES
    bq_rows = PEER_NKEYS // 4

    def stages(ht_cur, wt_cur, ht_prev, wt_prev):
        def pre_piece(mh, nt):
            ms = slice(mh * mh_rows, (mh + 1) * mh_rows)
            ns = slice(nt * nt_cols, (nt + 1) * nt_cols)
            pre = jnp.dot(u_ref[ms, :], xn2_ref[:, ns], preferred_element_type=F32)
            ht_cur[ms, ns] = _gelu_twice(pre).astype(BF16)

        def out_piece(mh, nt):
            out_rows = yt_sc.shape[0] // (eb // mh_rows)
            ms = slice(mh * out_rows, (mh + 1) * out_rows)
            ns = slice(nt * nt_cols, (nt + 1) * nt_cols)
            yt_sc[ms, ns] += jnp.dot(vt_ref[ms, :], wt_cur[:, ns], preferred_element_type=F32)

        def mix_block(tl, bq):
            ls = slice(tl * LANES, (tl + 1) * LANES)
            bs = slice(bq * bq_rows, (bq + 1) * bq_rows)
            acc = [jnp.zeros((bq_rows, LANES), F32)] * na
            for h in range(PEER_HEADS):
                s2 = s2_ref[h, bs, ls]
                e2 = e2_ref[h, bs, ls]
                for al in range(na):
                    hit = s2 >= theta_ref[h, al:al + 1, ls]
                    acc[al] = acc[al] + jnp.where(hit, e1_ref[h, al:al + 1, ls] * e2, 0.0)
            for al in range(na):
                rs = slice(al * PEER_NKEYS + bq * bq_rows, al * PEER_NKEYS + (bq + 1) * bq_rows)
                wt_prev[rs, ls] = (acc[al] * ht_prev[rs, ls].astype(F32)).astype(BF16)

        pieces = [(f, mh, nt) for nt in range(tb // nt_cols) for mh in range(eb // mh_rows)
                  for f in (pre_piece, out_piece)]
        blocks = [(tl, bq) for tl in range(tb // LANES) for bq in range(PEER_NKEYS // bq_rows)]
        per = -(-len(blocks) // len(pieces))
        for i, (f, mh, nt) in enumerate(pieces):
            f(mh, nt)
            for tl, bq in blocks[i * per:(i + 1) * per]:
                mix_block(tl, bq)

    @pl.when(g % 2 == 0)
    def _():
        stages(ht0_sc, wt0_sc, ht1_sc, wt1_sc)

    @pl.when(g % 2 == 1)
    def _():
        stages(ht1_sc, wt1_sc, ht0_sc, wt0_sc)

    @pl.when((g >= 2) & (out_tile == ne - 1))
    def _():
        y = x1_ref[...] + yt_sc[...].T
        if final_norm:
            y = _rms(y, gfin_ref[...])
        y_ref[...] = y


def _peer(xn2, x1, u, vt, s2, e2, theta, e1, gfin, tb, eb, final_norm):
    t, d = x1.shape
    ne = u.shape[0] // eb
    na = eb // PEER_NKEYS
    steps = (t // tb) * ne
    assert na == SUBLANES and u.shape[0] % eb == 0 and tb % (2 * LANES) == 0 and ne > 1

    def at(lag):
        def pos(g):
            s = jnp.clip(g - lag, 0, steps - 1)
            return s // ne, s % ne
        return pos

    pre, mid, out = at(0), at(1), at(2)
    sc = pl.BlockSpec((PEER_HEADS, PEER_NKEYS, tb), lambda g: (0, 0, mid(g)[0]))
    rows = pl.BlockSpec((PEER_HEADS, na, tb), lambda g: (0, mid(g)[1], mid(g)[0]))
    return pl.pallas_call(
        functools.partial(_peer_kernel, eb=eb, tb=tb, ne=ne, final_norm=final_norm),
        grid=(steps + 2,),
        in_specs=[pl.BlockSpec((d, tb), lambda g: (0, pre(g)[0])),
                  pl.BlockSpec((tb, d), lambda g: (out(g)[0], 0)),
                  pl.BlockSpec((eb, d), lambda g: (pre(g)[1], 0)),
                  pl.BlockSpec((d, eb), lambda g: (0, out(g)[1])),
                  sc, sc, rows, rows,
                  _full(gfin.shape)],
        out_specs=pl.BlockSpec((tb, d), lambda g: (out(g)[0], 0)),
        out_shape=jax.ShapeDtypeStruct((t, d), F32),
        scratch_shapes=[pltpu.VMEM((eb, tb), BF16), pltpu.VMEM((eb, tb), BF16),
                        pltpu.VMEM((eb, tb), BF16), pltpu.VMEM((eb, tb), BF16), pltpu.VMEM((d, tb), F32)],
        compiler_params=_params("arbitrary"),
        name="peer",
    )(xn2, x1, u, vt, s2, e2, theta, e1, gfin)


def _rope_tables(pos):
    half = MLA_ROPE // 2
    inv = ROPE_THETA ** (-jnp.arange(half, dtype=F32) / half)
    ang = pos.astype(F32)[:, None] * inv[None, :]
    cos, sin = jnp.cos(ang), jnp.sin(ang)
    c2 = jnp.concatenate([cos, cos], axis=-1)
    s2 = jnp.concatenate([-sin, sin], axis=-1)
    rows = jnp.concatenate([_place_cols(c2), _place_cols(s2)], axis=-1)
    return rows, jnp.concatenate([c2, s2], axis=-1).T


def _swap_halves(w):
    half = w.shape[-1] // 2
    return jnp.concatenate([w[..., half:], w[..., :half]], axis=-1)


def _place_cols(w):
    pad = [(0, 0)] * (w.ndim - 1) + [(ROPE_OFF, LANES - ROPE_OFF - w.shape[-1])]
    return jnp.pad(w, pad)


def _layer_weights(w_in, w_uq, w_uk, w_uv, lru_wa, lru_wi, w_out, peer_wq, peer_keys1, peer_keys2,
                   peer_u, peer_v):
    o1 = MLA_QRANK
    o2 = o1 + MLA_KVRANK
    o3 = o2 + MLA_ROPE
    kr = w_in[:, o2:o3]
    win = jnp.concatenate([w_in[:, :o2], _place_cols(kr), _place_cols(_swap_halves(kr)),
                           w_in[:, o3:]], axis=1).astype(BF16)
    wint = w_in[:, :o2].T.astype(BF16)
    wq3 = w_uq.reshape(MLA_QRANK, MLA_HEADS, MLA_NOPE + MLA_ROPE)
    nope = wq3[:, :, :MLA_NOPE].reshape(MLA_QRANK, MLA_HEADS * MLA_NOPE)
    rp = wq3[:, :, MLA_NOPE:]
    flat = lambda w: w.reshape(MLA_QRANK, -1)
    wuq = jnp.concatenate([nope, flat(_place_cols(rp)), flat(_place_cols(_swap_halves(rp)))], axis=1).astype(BF16)
    wuqt = jnp.concatenate([nope, flat(rp), flat(_swap_halves(rp))], axis=1).T.astype(BF16)
    eye_h = jnp.eye(MLA_HEADS, dtype=F32)
    wuk = jnp.einsum("rhd,hg->hdgr", w_uk, eye_h).reshape(MLA_HEADS * MLA_NOPE, MLA_HEADS * MLA_KVRANK)
    wuv = jnp.einsum("rhd,hg->hrgd", w_uv, eye_h).reshape(MLA_HEADS, MLA_KVRANK, MLA_WIDTH)
    wuvt = jnp.transpose(w_uv, (1, 2, 0))
    eye_b = jnp.eye(LRU_BLOCKS, dtype=F32)
    bd = lambda w: jnp.einsum("nde,nm->ndme", w, eye_b).reshape(LRU_WIDTH, LRU_WIDTH)
    return dict(win=win, wint=wint, wuq=wuq, wuqt=wuqt, wuk=wuk.astype(BF16), wukt=wuk.T.astype(BF16),
                wuv=wuv.astype(BF16), wuvt=wuvt.astype(BF16),
                wa=bd(lru_wa).astype(BF16), wi=bd(lru_wi).astype(BF16), wo=w_out.astype(BF16),
                wqt=peer_wq.T.astype(BF16), k1=peer_keys1.astype(BF16), k2=peer_keys2.astype(BF16),
                u=peer_u.astype(BF16), vt=peer_v.T.astype(BF16))


def _col(v):
    return v.reshape(-1, 1).astype(F32)


def _row(v):
    return v.reshape(1, -1).astype(F32)


def _token_tile(n, cap):
    t = min(n, cap)
    assert n % t == 0, (n, t)
    return t


def kernel(x_prompt, x_sample, cache_mla_ckv, cache_mla_krope, state_lru_h, state_lru_conv, norm_mix, w_in, norm_q, w_uq, norm_kv, w_uk, w_uv, conv_w, conv_b, lru_wa, lru_ba, lru_wi, lru_bi, lru_lambda, norm_mla_out, norm_lru_out, w_out, norm_ffn, peer_wq, peer_keys1, peer_keys2, peer_u, peer_v, norm_final):
    bp, sp, d = x_prompt.shape
    bs, sd, _ = x_sample.shape
    depth = w_in.shape[0]
    past = cache_mla_ckv.shape[2]
    ts_tok = bs * sd
    tab_p, tabt_p = _rope_tables(jnp.arange(sp))
    tab_s = jnp.tile(_rope_tables(past + jnp.arange(sd))[0], (bs, 1))
    gfin = _row(norm_final)

    xp = x_prompt
    xs = x_sample.reshape(1, ts_tok, d)
    outs = [[] for _ in range(8)]
    for l in range(depth):
        w = _layer_weights(w_in[l], w_uq[l], w_uk[l], w_uv[l], lru_wa[l], lru_wi[l], w_out[l], peer_wq[l],
                           peer_keys1[l], peer_keys2[l], peer_u[l], peer_v[l])
        last = l == depth - 1
        lru_args = (conv_w[l].astype(F32), _row(conv_b[l]), w["wa"], _row(lru_ba[l]), w["wi"], _row(lru_bi[l]),
                    _row(lru_lambda[l]), _row(norm_lru_out[l]))
        proj_common = (_row(norm_mix[l]), w["win"], _row(norm_kv[l]))
        proj_rows = (_row(norm_q[l]), w["wuq"], w["wuk"])
        proj_cols = (tabt_p, w["wint"], _col(norm_q[l]), _col(norm_kv[l]), w["wuqt"], w["wukt"])
        gmla = _row(norm_mla_out[l])

        def tail(x2d, mla, lru):
            t = x2d.shape[0]
            tm = _token_tile(t, TOKEN_TILE)
            x1, xn2, s2, e2, theta, e1 = _mix(x2d, mla, lru, w["wo"], _row(norm_ffn[l]), w["wqt"],
                                               w["k1"], w["k2"], tm)
            return _peer(xn2, x1, w["u"], w["vt"], s2, e2, theta, e1, gfin, tm, PEER_TILE, last)

        kcat, ckv, kr, xr, xg, qt, vt = _proj(xp, tab_p, proj_common, proj_cols, _token_tile(sp, TOKEN_TILE), True)
        mla = _attn_prompt(qt, kcat, vt, w["wuvt"], gmla, _token_tile(sp, TOKEN_TILE))
        lru, hl, nb = _lru(xr, xg, jnp.zeros((bp, CONV_WIDTH - 1, LRU_WIDTH), F32),
                           jnp.zeros((bp, 1, LRU_WIDTH), F32), *lru_args, _token_tile(sp, TOKEN_TILE))
        xp = tail(xp.reshape(bp * sp, d), mla.reshape(bp * sp, MLA_WIDTH),
                  lru.reshape(bp * sp, LRU_WIDTH)).reshape(bp, sp, d)
        for lst, v in zip(outs[:4], (ckv, kr, hl[:, 0], nb)):
            lst.append(v)

        kcat, ckv, kr, xr, xg, q = _proj(xs, tab_s, proj_common, proj_rows, _token_tile(ts_tok, TOKEN_TILE), False)
        mla = _attn_sample(q, cache_mla_ckv[l], cache_mla_krope[l], kcat, w["wuv"], gmla, bs, sd)
        lru, hl, nb = _lru(xr.reshape(bs, sd, LRU_WIDTH), xg.reshape(bs, sd, LRU_WIDTH),
                           state_lru_conv[l].astype(F32), state_lru_h[l].reshape(bs, 1, LRU_WIDTH).astype(F32),
                           *lru_args, sd)
        xs = tail(xs[0], mla[0], lru.reshape(ts_tok, LRU_WIDTH)).reshape(1, ts_tok, d)
        for lst, v in zip(outs[4:], (ckv.reshape(bs, sd, MLA_KVRANK), kr.reshape(bs, sd, MLA_ROPE), hl[:, 0], nb)):
            lst.append(v)

    return (xp, xs.reshape(bs, sd, d)) + tuple(jnp.stack(o) for o in outs)
```
